```python
import jax, jax.numpy as jnp
from jax import lax
import numpy as np

D_MODEL = 1024
BATCH = 4
SEQ = 4096
DEPTH = 2
DEC_BATCH = 32
DEC_SEQ = 4
PAST_LEN = 8192
PAGE_SIZE = 128

N_HEADS = 16
N_KV_HEADS = 4
HEAD_DIM = 64
D_ATTN = N_HEADS * HEAD_DIM
IDX_HEADS = 8
IDX_DIM = 64
INDEX_TOPK = 256
ATTN_Q_BLOCK = 128
ROPE_THETA = 10000.0
POOL_WINDOWS = (2, 4, 8, 16)
D_POOL = D_MODEL
N_POOL_GROUPS = len(POOL_WINDOWS)
POOL_GROUP = D_POOL // N_POOL_GROUPS
POOL_CTX = max(POOL_WINDOWS) - 1
SSD_EXPAND = 2
D_SSD = SSD_EXPAND * D_MODEL
SSD_HEAD_DIM = 64
SSD_HEADS = D_SSD // SSD_HEAD_DIM
SSD_GROUPS = 4
SSD_STATE = 128
SSD_CONV = 4
SSD_CHUNK = 128
D_SSD_CONV = D_SSD + 2 * SSD_GROUPS * SSD_STATE
D_FF = 2816
FFN_CONV = 3
N_BRANCH = 3
EPS = 1e-6

_SPLITS = (D_ATTN, N_KV_HEADS * HEAD_DIM, N_KV_HEADS * HEAD_DIM, IDX_HEADS * IDX_DIM, IDX_DIM,
           IDX_HEADS, D_POOL, D_SSD, D_SSD_CONV, SSD_HEADS, N_BRANCH * D_MODEL)
D_IN = sum(_SPLITS)
_OFFSETS = tuple(int(v) for v in np.cumsum(_SPLITS)[:-1])

kernel_name = 'hybrid_dsa_pool_ssd_convffn_adaln_step'


def _rmsnorm(x, g):
    xf = x.astype(jnp.float32)
    y = xf * lax.rsqrt(jnp.mean(xf * xf, axis=-1, keepdims=True) + EPS)
    return (y * g.astype(jnp.float32)).astype(x.dtype)


def _rope(x, pos):
    half = x.shape[-1] // 2
    freqs = ROPE_THETA ** (-jnp.arange(half, dtype=jnp.float32) / half)
    ang = pos.astype(jnp.float32)[:, None] * freqs[None, :]
    cos = jnp.cos(ang)[:, None, :]
    sin = jnp.sin(ang)[:, None, :]
    xf = x.astype(jnp.float32)
    x1, x2 = xf[..., :half], xf[..., half:]
    return jnp.concatenate([x1 * cos - x2 * sin, x1 * sin + x2 * cos], axis=-1).astype(x.dtype)


def _causal_dwconv(u, prev, w, b):
    width = w.shape[0]
    T = u.shape[1]
    ext = jnp.concatenate([prev.astype(u.dtype), u], axis=1)
    out = b + ext[:, 0:T] * w[0]
    for j in range(1, width):
        out = out + ext[:, j:j + T] * w[j]
    return out, ext[:, T:]


def _gather_pages(pool, table):
    g = pool[table]
    return g.reshape((table.shape[0], table.shape[1] * pool.shape[1]) + pool.shape[2:])


def _sparse_attention(q, k_all, v_all, qi, ki_all, wi, qpos, topk):
    B, T = q.shape[0], q.shape[1]
    S = k_all.shape[1]
    blk = ATTN_Q_BLOCK if T % ATTN_Q_BLOCK == 0 else T
    nb = T // blk
    kpos = jnp.arange(S)
    ki_f = ki_all.astype(jnp.float32)

    def to_blocks(a):
        return jnp.moveaxis(a.reshape((B, nb, blk) + a.shape[2:]), 1, 0)

    def block(args):
        q_b, qi_b, wi_b, p_b = args
        rel = jax.nn.relu(jnp.einsum('bqhd,bsd->bqsh', qi_b.astype(jnp.float32), ki_f) * IDX_DIM ** -0.5)
        score = jnp.einsum('bqsh,bqh->bqs', rel, wi_b.astype(jnp.float32))
        visible = kpos[None, None, :] <= p_b[None, :, None]
        score = jnp.where(visible, score, -jnp.inf)
        _, idx = lax.top_k(score, topk)
        valid = idx <= p_b[None, :, None]
        ks = jax.vmap(lambda kk, ii: kk[ii])(k_all, idx).astype(jnp.float32)
        vs = jax.vmap(lambda vv, ii: vv[ii])(v_all, idx).astype(jnp.float32)
        qg = q_b.reshape(B, blk, N_KV_HEADS, N_HEADS // N_KV_HEADS, HEAD_DIM).astype(jnp.float32)
        s = jnp.einsum('bqgrd,bqkgd->bqgrk', qg, ks) * HEAD_DIM ** -0.5
        s = jnp.where(valid[:, :, None, None, :], s, -jnp.inf)
        p = jax.nn.softmax(s, axis=-1)
        o = jnp.einsum('bqgrk,bqkgd->bqgrd', p, vs)
        return o.reshape(B, blk, D_ATTN).astype(q.dtype)

    out = lax.map(block, (to_blocks(q), to_blocks(qi), to_blocks(wi), qpos.reshape(nb, blk)))
    return jnp.moveaxis(out, 0, 1).reshape(B, T, D_ATTN)


def _pool_mix(u, prev, pos, w_grp, scale):
    B, T, C = u.shape
    ext = jnp.concatenate([prev.astype(u.dtype), u], axis=1)
    cs = jnp.cumsum(ext.astype(jnp.float32), axis=1)
    cs = jnp.concatenate([jnp.zeros((B, 1, C), jnp.float32), cs], axis=1)
    hi = cs[:, POOL_CTX + 1:]
    groups = []
    for gi, w in enumerate(POOL_WINDOWS):
        sl = slice(gi * POOL_GROUP, (gi + 1) * POOL_GROUP)
        lo = cs[:, POOL_CTX + 1 - w:POOL_CTX + 1 - w + T, sl]
        cnt = jnp.minimum(w, pos + 1).astype(jnp.float32)[None, :, None]
        groups.append((hi[..., sl] - lo) / cnt)
    pooled = jnp.concatenate(groups, axis=-1) - u.astype(jnp.float32)
    mixed = jnp.einsum('btgc,gcd->btgd', pooled.reshape(B, T, N_POOL_GROUPS, POOL_GROUP),
                       w_grp.astype(jnp.float32)).reshape(B, T, C) * scale.astype(jnp.float32)
    return mixed.astype(u.dtype), ext[:, T:]


def _ssd_scan(x, dt, A, Bm, Cm, h0):
    b, T, H, P = x.shape
    G, N = Bm.shape[2], Bm.shape[3]
    R = H // G
    Q = min(SSD_CHUNK, T)
    pad = (-T) % Q
    if pad:
        x = jnp.pad(x, ((0, 0), (0, pad), (0, 0), (0, 0)))
        dt = jnp.pad(dt, ((0, 0), (0, pad), (0, 0)))
        Bm = jnp.pad(Bm, ((0, 0), (0, pad), (0, 0), (0, 0)))
        Cm = jnp.pad(Cm, ((0, 0), (0, pad), (0, 0), (0, 0)))
    nc = (T + pad) // Q
    x = x.reshape(b, nc, Q, G, R, P)
    dt = dt.reshape(b, nc, Q, G, R)
    Bm = Bm.reshape(b, nc, Q, G, N)
    Cm = Cm.reshape(b, nc, Q, G, N)
    cs = jnp.cumsum(dt * A.reshape(G, R), axis=2)
    cs_t = jnp.moveaxis(cs, 2, -1)
    seg = cs_t[..., :, None] - cs_t[..., None, :]
    causal = jnp.tril(jnp.ones((Q, Q), dtype=bool))
    Lmat = jnp.where(causal, jnp.exp(jnp.where(causal, seg, 0.0)), 0.0)
    CB = jnp.einsum('bclgn,bcsgn->bcgls', Cm, Bm)
    xdt = x * dt[..., None]
    y_diag = jnp.einsum('bcgrls,bcsgrp->bclgrp', CB[:, :, :, None] * Lmat, xdt)
    decay = jnp.exp(cs[:, :, -1:] - cs)
    states = jnp.einsum('bclgn,bclgrp->bcgrpn', Bm, xdt * decay[..., None])
    chunk_decay = jnp.exp(cs[:, :, -1])

    def step(h, inp):
        s, d = inp
        return h * d[..., None, None] + s, h

    hT, prev = lax.scan(step, h0.reshape(b, G, R, P, N),
                        (jnp.moveaxis(states, 1, 0), jnp.moveaxis(chunk_decay, 1, 0)))
    prev = jnp.moveaxis(prev, 0, 1)
    y_off = jnp.einsum('bclgn,bcgrpn->bclgrp', Cm, prev) * jnp.exp(cs)[..., None]
    y = (y_diag + y_off).reshape(b, nc * Q, H, P)[:, :T]
    return y, hT.reshape(b, H, P, N)


def _ssd_mix(z, xbc, dt_raw, conv_prev, h0, conv_w, conv_b, dt_bias, a_log, d_skip, norm_w):
    B, T, _ = z.shape
    xbc, conv_state = _causal_dwconv(xbc, conv_prev, conv_w, conv_b)
    xbc = jax.nn.silu(xbc.astype(jnp.float32))
    xs = xbc[..., :D_SSD].reshape(B, T, SSD_HEADS, SSD_HEAD_DIM)
    Bm = xbc[..., D_SSD:D_SSD + SSD_GROUPS * SSD_STATE].reshape(B, T, SSD_GROUPS, SSD_STATE)
    Cm = xbc[..., D_SSD + SSD_GROUPS * SSD_STATE:].reshape(B, T, SSD_GROUPS, SSD_STATE)
    dt = jax.nn.softplus(dt_raw.astype(jnp.float32) + dt_bias.astype(jnp.float32))
    A = -jnp.exp(a_log.astype(jnp.float32))
    y, hT = _ssd_scan(xs, dt, A, Bm, Cm, h0.astype(jnp.float32))
    y = y + d_skip.astype(jnp.float32)[:, None] * xs
    y = y.reshape(B, T, D_SSD) * jax.nn.silu(z.astype(jnp.float32))
    yg = y.reshape(B, T, SSD_GROUPS, D_SSD // SSD_GROUPS)
    yg = yg * lax.rsqrt(jnp.mean(yg * yg, axis=-1, keepdims=True) + EPS)
    y = yg.reshape(B, T, D_SSD) * norm_w.astype(jnp.float32)
    return y.astype(z.dtype), conv_state, hT.astype(h0.dtype)


def _trunk_layer(x, c, pos, kv_past, pool_prev, sconv_prev, ssd_h0, fconv_prev, lw):
    B, T, _ = x.shape
    mod = (jax.nn.silu(c) @ lw['w_ada'] + lw['b_ada'])[:, None, :]
    sh1, sc1, g1, sh2, sc2, g2 = jnp.split(mod, 6, axis=-1)

    h = _rmsnorm(x, lw['norm1']) * (1 + sc1) + sh1
    proj = h @ lw['w_in']
    q, k, v, qi, ki, wi, pool_u, z, xbc, dt_raw, gates = jnp.split(proj, _OFFSETS, axis=-1)

    q = _rope(_rmsnorm(q.reshape(B, T, N_HEADS, HEAD_DIM), lw['q_norm']), pos)
    k = _rope(_rmsnorm(k.reshape(B, T, N_KV_HEADS, HEAD_DIM), lw['k_norm']), pos)
    v = v.reshape(B, T, N_KV_HEADS, HEAD_DIM)
    qi = _rope(qi.reshape(B, T, IDX_HEADS, IDX_DIM), pos)
    ki = _rope(ki.reshape(B, T, 1, IDX_DIM), pos)[:, :, 0]
    wi = wi * IDX_HEADS ** -0.5
    if kv_past is None:
        k_all, v_all, ki_all = k, v, ki
    else:
        kp, vp, kip = kv_past
        k_all = jnp.concatenate([kp.astype(k.dtype), k], axis=1)
        v_all = jnp.concatenate([vp.astype(v.dtype), v], axis=1)
        ki_all = jnp.concatenate([kip.astype(ki.dtype), ki], axis=1)
    topk = min(INDEX_TOPK, k_all.shape[1] // 4)
    o_attn = _sparse_attention(q, k_all, v_all, qi, ki_all, wi, pos, topk)

    o_pool, pool_state = _pool_mix(pool_u, pool_prev, pos, lw['pool_w'], lw['pool_scale'])

    o_ssd, sconv_state, ssd_state = _ssd_mix(z, xbc, dt_raw, sconv_prev, ssd_h0, lw['ssd_conv_w'],
                                             lw['ssd_conv_b'], lw['ssd_dt_bias'], lw['ssd_a_log'],
                                             lw['ssd_d'], lw['ssd_norm'])

    gt = jax.nn.sigmoid(gates.astype(jnp.float32)).reshape(B, T, N_BRANCH, D_MODEL)
    merged = (gt[:, :, 0] * (o_attn @ lw['w_branch_attn'])
              + gt[:, :, 1] * (o_pool @ lw['w_branch_pool'])
              + gt[:, :, 2] * (o_ssd @ lw['w_branch_ssd']))
    x = x + g1 * (merged.astype(x.dtype) @ lw['w_out'])

    h2 = _rmsnorm(x, lw['norm2']) * (1 + sc2) + sh2
    u = h2 @ lw['ffn_up']
    uc, fconv_state = _causal_dwconv(u, fconv_prev, lw['ffn_conv_w'], lw['ffn_conv_b'])
    a, gg = jnp.split(uc, 2, axis=-1)
    x = x + g2 * ((jax.nn.silu(gg) * a) @ lw['ffn_down'])
    return x, (k, v, ki, pool_state, sconv_state, ssd_state, fconv_state)


def setup_inputs(seed: int = 0) -> dict:
    key = jax.random.key(seed)
    ks = jax.random.split(key, 40)
    f32 = jnp.float32

    def nrm(k, shape, scale):
        return jax.random.normal(k, shape, f32) * scale

    n_pages = PAST_LEN // PAGE_SIZE
    n_used = DEC_BATCH * n_pages
    n_pool = n_used + n_used // 4
    page_table = jax.random.permutation(ks[0], n_pool)[:n_used].reshape(DEC_BATCH, n_pages).astype(jnp.int32)
    dt0 = jnp.exp(jax.random.uniform(ks[1], (DEPTH, SSD_HEADS), f32, np.log(1e-3), np.log(1e-1)))
    return {
        'x_prompt': nrm(ks[2], (BATCH, SEQ, D_MODEL), 1.0),
        'x_sample': nrm(ks[3], (DEC_BATCH, DEC_SEQ, D_MODEL), 1.0),
        'c_prompt': nrm(ks[4], (BATCH, D_MODEL), 1.0),
        'c_sample': nrm(ks[5], (DEC_BATCH, D_MODEL), 1.0),
        'cache_k': nrm(ks[6], (DEPTH, n_pool, PAGE_SIZE, N_KV_HEADS, HEAD_DIM), 1.0),
        'cache_v': nrm(ks[7], (DEPTH, n_pool, PAGE_SIZE, N_KV_HEADS, HEAD_DIM), 1.0),
        'cache_kidx': nrm(ks[8], (DEPTH, n_pool, PAGE_SIZE, IDX_DIM), 1.0),
        'page_table': page_table,
        'state_pool': nrm(ks[9], (DEPTH, DEC_BATCH, POOL_CTX, D_POOL), 1.0),
        'state_ssd_conv': nrm(ks[10], (DEPTH, DEC_BATCH, SSD_CONV - 1, D_SSD_CONV), 1.0),
        'state_ssd': nrm(ks[11], (DEPTH, DEC_BATCH, SSD_HEADS, SSD_HEAD_DIM, SSD_STATE), 0.1),
        'state_ffn_conv': nrm(ks[12], (DEPTH, DEC_BATCH, FFN_CONV - 1, 2 * D_FF), 1.0),
        'w_ada': nrm(ks[13], (DEPTH, D_MODEL, 6 * D_MODEL), 0.5 * D_MODEL ** -0.5),
        'b_ada': nrm(ks[14], (DEPTH, 6 * D_MODEL), 0.02),
        'norm1': 1.0 + nrm(ks[15], (DEPTH, D_MODEL), 0.05),
        'norm2': 1.0 + nrm(ks[16], (DEPTH, D_MODEL), 0.05),
        'w_in': nrm(ks[17], (DEPTH, D_MODEL, D_IN), D_MODEL ** -0.5),
        'q_norm': 1.0 + nrm(ks[18], (DEPTH, HEAD_DIM), 0.05),
        'k_norm': 1.0 + nrm(ks[19], (DEPTH, HEAD_DIM), 0.05),
        'pool_w': nrm(ks[20], (DEPTH, N_POOL_GROUPS, POOL_GROUP, POOL_GROUP), POOL_GROUP ** -0.5),
        'pool_scale': 1.0 + nrm(ks[21], (DEPTH, D_POOL), 0.1),
        'ssd_conv_w': nrm(ks[22], (DEPTH, SSD_CONV, D_SSD_CONV), SSD_CONV ** -0.5),
        'ssd_conv_b': nrm(ks[23], (DEPTH, D_SSD_CONV), 0.02),
        'ssd_dt_bias': dt0 + jnp.log(-jnp.expm1(-dt0)),
        'ssd_a_log': jnp.log(jax.random.uniform(ks[24], (DEPTH, SSD_HEADS), f32, 1.0, 16.0)),
        'ssd_d': 1.0 + nrm(ks[25], (DEPTH, SSD_HEADS), 0.1),
        'ssd_norm': 1.0 + nrm(ks[26], (DEPTH, D_SSD), 0.05),
        'w_branch_attn': nrm(ks[27], (DEPTH, D_ATTN, D_MODEL), D_ATTN ** -0.5),
        'w_branch_pool': nrm(ks[28], (DEPTH, D_POOL, D_MODEL), D_POOL ** -0.5),
        'w_branch_ssd': nrm(ks[29], (DEPTH, D_SSD, D_MODEL), D_SSD ** -0.5),
        'w_out': nrm(ks[30], (DEPTH, D_MODEL, D_MODEL), D_MODEL ** -0.5),
        'ffn_up': nrm(ks[31], (DEPTH, D_MODEL, 2 * D_FF), D_MODEL ** -0.5),
        'ffn_conv_w': nrm(ks[32], (DEPTH, FFN_CONV, 2 * D_FF), FFN_CONV ** -0.5),
        'ffn_conv_b': nrm(ks[33], (DEPTH, 2 * D_FF), 0.02),
        'ffn_down': nrm(ks[34], (DEPTH, D_FF, D_MODEL), D_FF ** -0.5),
    }


def reference(x_prompt, x_sample, c_prompt, c_sample, cache_k, cache_v, cache_kidx, page_table,
              state_pool, state_ssd_conv, state_ssd, state_ffn_conv, w_ada, b_ada, norm1, norm2, w_in,
              q_norm, k_norm, pool_w, pool_scale, ssd_conv_w, ssd_conv_b, ssd_dt_bias, ssd_a_log, ssd_d,
              ssd_norm, w_branch_attn, w_branch_pool, w_branch_ssd, w_out, ffn_up, ffn_conv_w, ffn_conv_b,
              ffn_down):
    bp, tp = x_prompt.shape[0], x_prompt.shape[1]
    ts = x_sample.shape[1]
    past_len = page_table.shape[1] * cache_k.shape[2]
    pos_p = jnp.arange(tp)
    pos_s = past_len + jnp.arange(ts)
    dtp = x_prompt.dtype
    xp, xs = x_prompt, x_sample
    new_p = [[] for _ in range(7)]
    new_s = [[] for _ in range(7)]
    for l in range(DEPTH):
        lw = dict(w_ada=w_ada[l], b_ada=b_ada[l], norm1=norm1[l], norm2=norm2[l], w_in=w_in[l],
                  q_norm=q_norm[l], k_norm=k_norm[l], pool_w=pool_w[l], pool_scale=pool_scale[l],
                  ssd_conv_w=ssd_conv_w[l], ssd_conv_b=ssd_conv_b[l], ssd_dt_bias=ssd_dt_bias[l],
                  ssd_a_log=ssd_a_log[l], ssd_d=ssd_d[l], ssd_norm=ssd_norm[l],
                  w_branch_attn=w_branch_attn[l], w_branch_pool=w_branch_pool[l],
                  w_branch_ssd=w_branch_ssd[l], w_out=w_out[l], ffn_up=ffn_up[l],
                  ffn_conv_w=ffn_conv_w[l], ffn_conv_b=ffn_conv_b[l], ffn_down=ffn_down[l])
        xp, st_p = _trunk_layer(
            xp, c_prompt, pos_p, None,
            jnp.zeros((bp, POOL_CTX, D_POOL), dtp),
            jnp.zeros((bp, SSD_CONV - 1, D_SSD_CONV), dtp),
            jnp.zeros((bp, SSD_HEADS, SSD_HEAD_DIM, SSD_STATE), dtp),
            jnp.zeros((bp, FFN_CONV - 1, 2 * D_FF), dtp), lw)
        kv_past = (_gather_pages(cache_k[l], page_table), _gather_pages(cache_v[l], page_table),
                   _gather_pages(cache_kidx[l], page_table))
        xs, st_s = _trunk_layer(xs, c_sample, pos_s, kv_past, state_pool[l], state_ssd_conv[l],
                                state_ssd[l], state_ffn_conv[l], lw)
        for j in range(7):
            new_p[j].append(st_p[j])
            new_s[j].append(st_s[j])
    nk_p, nv_p, nki_p, npool_p, nsconv_p, nssd_p, nfconv_p = [jnp.stack(s) for s in new_p]
    nk_s, nv_s, nki_s, npool_s, nsconv_s, nssd_s, nfconv_s = [jnp.stack(s) for s in new_s]
    return (xp, xs, nk_p, nv_p, nki_p, npool_p, nsconv_p, nssd_p, nfconv_p,
            nk_s, nv_s, nki_s, npool_s, nsconv_s, nssd_s, nfconv_s)
```

```python
import functools

import numpy as np
import jax
import jax.numpy as jnp
from jax import lax
from jax.experimental import pallas as pl
from jax.experimental.pallas import tpu as pltpu

D_MODEL = 1024
N_HEADS = 16
N_KV_HEADS = 4
HEAD_DIM = 64
D_ATTN = N_HEADS * HEAD_DIM
D_KV = N_KV_HEADS * HEAD_DIM
IDX_HEADS = 8
IDX_DIM = 64
INDEX_TOPK = 256
ROPE_THETA = 10000.0
POOL_WINDOWS = (2, 4, 8, 16)
D_POOL = D_MODEL
POOL_GROUP = D_POOL // len(POOL_WINDOWS)
POOL_CTX = max(POOL_WINDOWS) - 1
D_SSD = 2 * D_MODEL
SSD_HEAD_DIM = 64
SSD_HEADS = D_SSD // SSD_HEAD_DIM
SSD_GROUPS = 4
SSD_STATE = 128
SSD_CONV = 4
D_SSD_CONV = D_SSD + 2 * SSD_GROUPS * SSD_STATE
D_FF = 2816
FFN_CONV = 3
N_BRANCH = 3
EPS = 1e-6
_SPLITS = (D_ATTN, D_KV, D_KV, IDX_HEADS * IDX_DIM, IDX_DIM, IDX_HEADS, D_POOL, D_SSD, D_SSD_CONV,
           SSD_HEADS, N_BRANCH * D_MODEL)
_OFFS = tuple(int(v) for v in np.cumsum((0,) + _SPLITS))

F32 = jnp.float32
BF16 = jnp.bfloat16
LANES = 128
SUBLANES = 8
SEQ_BLOCK = 128
KV_BLOCK = 128
VMEM_LIMIT = 56 * 1024 * 1024
NEG = -1e30
N_BISECT = 24
NT = (((1,), (1,)), ((), ()))
TN = (((0,), (0,)), ((), ()))


def _cparams(*sem):
    return pltpu.CompilerParams(dimension_semantics=sem, vmem_limit_bytes=VMEM_LIMIT)


def _silu(v):
    return v * jax.nn.sigmoid(v)


def _split3(v):
    h1 = v.astype(BF16)
    r1 = v - h1.astype(F32)
    h2 = r1.astype(BF16)
    h3 = (r1 - h2.astype(F32)).astype(BF16)
    return h1, h2, h3


def _dot3(v, w):
    h1, h2, h3 = _split3(v)
    return (jnp.dot(h1, w, preferred_element_type=F32) + jnp.dot(h2, w, preferred_element_type=F32)
            + jnp.dot(h3, w, preferred_element_type=F32))


def _mod_kernel(c_ref, w_ref, b_ref, o_ref):
    s = _silu(c_ref[...])
    o_ref[...] = jnp.dot(s.astype(BF16), w_ref[...], preferred_element_type=F32) + b_ref[...]


def _mod(c_pad, w, b):
    rows, d = c_pad.shape
    n = w.shape[1]
    tn = n // 4
    return pl.pallas_call(
        _mod_kernel,
        out_shape=jax.ShapeDtypeStruct((rows, n), F32),
        grid=(n // tn,),
        in_specs=[pl.BlockSpec((rows, d), lambda j: (0, 0)),
                  pl.BlockSpec((d, tn), lambda j: (0, j)),
                  pl.BlockSpec((1, tn), lambda j: (0, j))],
        out_specs=pl.BlockSpec((rows, tn), lambda j: (0, j)),
        compiler_params=_cparams("parallel"),
    )(c_pad, w, b)


def _norm_matmul_kernel(x_ref, sh_ref, sc_ref, g_ref, w_ref, o_ref):
    x = x_ref[...]
    y = x * lax.rsqrt(jnp.mean(x * x, axis=-1, keepdims=True) + EPS) * g_ref[...]
    h = y * (1.0 + sc_ref[0]) + sh_ref[0]
    o_ref[...] = jnp.dot(h.astype(BF16), w_ref[...], preferred_element_type=F32).astype(o_ref.dtype)


def _norm_matmul(x, mv_sh, mv_sc, mv_idx, g, w, tm):
    n, d = x.shape
    nout = w.shape[1]
    rb = mv_sh.shape[1]
    return pl.pallas_call(
        _norm_matmul_kernel,
        out_shape=jax.ShapeDtypeStruct((n, nout), F32),
        grid=(n // tm,),
        in_specs=[pl.BlockSpec((tm, d), lambda i: (i, 0)),
                  pl.BlockSpec((1, rb, d), lambda i: (mv_idx(i), 0, 0)),
                  pl.BlockSpec((1, rb, d), lambda i: (mv_idx(i), 0, 0)),
                  pl.BlockSpec((1, d), lambda i: (0, 0)),
                  pl.BlockSpec((d, nout), lambda i: (0, 0))],
        out_specs=pl.BlockSpec((tm, nout), lambda i: (i, 0)),
        compiler_params=_cparams("parallel"),
    )(x, mv_sh, mv_sc, g, w)


def _attn_prep_kernel(q_ref, k_ref, qi_ref, kw_ref, cos_ref, sin_ref, qn_ref, kn_ref, seg_ref,
                      qo_ref, ko_ref, qio_ref, kio_ref, wio_ref):
    tm = q_ref.shape[0]
    lane = lax.broadcasted_iota(jnp.int32, (tm, LANES), 1)
    first_half = (lane % HEAD_DIM) < (HEAD_DIM // 2)
    cos = cos_ref[0]
    sin = sin_ref[0]
    seg = seg_ref[...]

    def rope(v):
        up = pltpu.roll(v, LANES - HEAD_DIM // 2, 1)
        dn = pltpu.roll(v, HEAD_DIM // 2, 1)
        return v * cos + jnp.where(first_half, up, dn) * sin

    def headnorm(v, gain):
        sq = v * v
        hi = sq.astype(BF16)
        lo = (sq - hi.astype(F32)).astype(BF16)
        ss = jnp.dot(hi, seg, preferred_element_type=F32) + jnp.dot(lo, seg, preferred_element_type=F32)
        return v * lax.rsqrt(ss * (1.0 / HEAD_DIM) + EPS) * gain

    qn = qn_ref[...]
    kn = kn_ref[...]
    for j in range(D_ATTN // LANES):
        sl = slice(j * LANES, (j + 1) * LANES)
        qo_ref[:, sl] = rope(headnorm(q_ref[:, sl], qn)) * (HEAD_DIM ** -0.5)
    for j in range(D_KV // LANES):
        sl = slice(j * LANES, (j + 1) * LANES)
        ko_ref[:, sl] = rope(headnorm(k_ref[:, sl], kn))
    for j in range(IDX_HEADS * IDX_DIM // LANES):
        sl = slice(j * LANES, (j + 1) * LANES)
        qio_ref[:, sl] = rope(qi_ref[:, sl])
    kw = kw_ref[...]
    kr = rope(kw)
    kio_ref[...] = jnp.where(lane < IDX_DIM, kr, pltpu.roll(kr, IDX_DIM, 1))
    wio_ref[...] = pltpu.roll(kw, LANES - IDX_DIM, 1) * (IDX_HEADS ** -0.5 * IDX_DIM ** -0.5)


def _attn_prep(pa, cos_t, sin_t, tab_idx, qn, kn, seg, tm):
    n = pa.shape[0]
    blk = lambda w, c: pl.BlockSpec((tm, w), lambda i: (i, c))
    tab = pl.BlockSpec((1, tm, LANES), lambda i: (tab_idx(i), 0, 0))
    cst = lambda r, c: pl.BlockSpec((r, c), lambda i: (0, 0))
    return pl.pallas_call(
        _attn_prep_kernel,
        out_shape=[jax.ShapeDtypeStruct((n, D_ATTN), F32), jax.ShapeDtypeStruct((n, D_KV), F32),
                   jax.ShapeDtypeStruct((n, IDX_HEADS * IDX_DIM), F32),
                   jax.ShapeDtypeStruct((n, LANES), F32), jax.ShapeDtypeStruct((n, LANES), F32)],
        grid=(n // tm,),
        in_specs=[blk(D_ATTN, 0), blk(D_KV, D_ATTN // D_KV), blk(IDX_HEADS * IDX_DIM, 3), blk(LANES, 16),
                  tab, tab, cst(1, LANES), cst(1, LANES), cst(LANES, LANES)],
        out_specs=[blk(D_ATTN, 0), blk(D_KV, 0), blk(IDX_HEADS * IDX_DIM, 0), blk(LANES, 0), blk(LANES, 0)],
        compiler_params=_cparams("parallel"),
    )(pa, pa, pa, pa, cos_t, sin_t, qn, kn, seg)


def _page_gather_kernel(pt_ref, ck_ref, cv_ref, ci_ref, nk_ref, nv_ref, ni_ref, ok_ref, ov_ref, oi_ref, *, n_pages):
    p = pl.program_id(1)
    past = p < n_pages
    ok_ref[0] = jnp.where(past, ck_ref[0], nk_ref[0]).astype(ok_ref.dtype)
    ov_ref[0] = jnp.where(past, cv_ref[0], nv_ref[0]).astype(ov_ref.dtype)
    oi_ref[0] = jnp.where(past, ci_ref[0], ni_ref[0]).astype(oi_ref.dtype)


def _page_gather(page_table, ck, cv, ci, nk, nv, ni):
    bd, n_pages = page_table.shape
    page = ck.shape[1]
    cache = lambda w: pl.BlockSpec((1, page, w), lambda b, p, pt: (pt[b, jnp.minimum(p, n_pages - 1)], 0, 0))
    new = lambda w: pl.BlockSpec((1, page, w), lambda b, p, pt: (b, 0, 0))
    out = lambda w: pl.BlockSpec((1, page, w), lambda b, p, pt: (b, p, 0))
    s_pad = (n_pages + 1) * page
    return pl.pallas_call(
        functools.partial(_page_gather_kernel, n_pages=n_pages),
        out_shape=[jax.ShapeDtypeStruct((bd, s_pad, D_KV), BF16), jax.ShapeDtypeStruct((bd, s_pad, D_KV), BF16),
                   jax.ShapeDtypeStruct((bd, s_pad, IDX_DIM), BF16)],
        grid_spec=pltpu.PrefetchScalarGridSpec(
            num_scalar_prefetch=1, grid=(bd, n_pages + 1),
            in_specs=[cache(D_KV), cache(D_KV), cache(IDX_DIM), new(D_KV), new(D_KV), new(IDX_DIM)],
            out_specs=[out(D_KV), out(D_KV), out(IDX_DIM)]),
        compiler_params=_cparams("parallel", "parallel"),
    )(page_table, ck, cv, ci, nk, nv, ni)


def _attn_kernel(q_ref, qi_ref, wi_ref, k_ref, v_ref, ki_ref, o_ref,
                 sc_ref, qh_ref, qim_ref, wb_ref, m_ref, l_ref, acc_ref, *, tq, nblk, pos0, s_valid, kk):
    q0 = pl.program_id(1) * tq
    lane = lax.broadcasted_iota(jnp.int32, (tq, LANES), 1)
    row = lax.broadcasted_iota(jnp.int32, (tq, LANES), 0)
    qpos = pos0 + q0 + row
    lo_half = lane < HEAD_DIM
    n_kv = jnp.minimum(nblk, (pos0 + q0 + tq - 1) // KV_BLOCK + 1)
    kkf = float(kk)

    def bcast(col):
        return jnp.broadcast_to(col, (tq, LANES))

    wi = wi_ref[0]
    for j in range(IDX_HEADS // 2):
        t = qi_ref[0, :, j * LANES:(j + 1) * LANES]
        qim_ref[2 * j] = jnp.where(lo_half, t, 0.0).astype(BF16)
        qim_ref[2 * j + 1] = jnp.where(lo_half, 0.0, t).astype(BF16)
    for h in range(IDX_HEADS):
        wb_ref[h] = bcast(wi[:, h:h + 1])
    for h in range(N_HEADS):
        t = q_ref[0, :, (h // 2) * LANES:(h // 2 + 1) * LANES]
        kv_half = (h // (N_HEADS // N_KV_HEADS)) % 2
        if h % 2 != kv_half:
            t = pltpu.roll(t, HEAD_DIM, 1)
        keep = lo_half if kv_half == 0 else jnp.logical_not(lo_half)
        qh_ref[h] = jnp.where(keep, t, 0.0).astype(BF16)
        m_ref[h] = jnp.full((tq, LANES), NEG, F32)
        l_ref[h] = jnp.zeros((tq, LANES), F32)
        acc_ref[h] = jnp.zeros((tq, LANES), F32)

    def p1(c, carry):
        kid = ki_ref[0, pl.ds(pl.multiple_of(c * KV_BLOCK, KV_BLOCK), KV_BLOCK), :]
        acc = jnp.zeros((tq, LANES), F32)
        for h in range(IDX_HEADS):
            y = lax.dot_general(qim_ref[h], kid, NT, preferred_element_type=F32)
            acc = acc + wb_ref[h] * jnp.maximum(y, 0.0)
        kpos = c * KV_BLOCK + lane
        vis = jnp.logical_and(kpos <= qpos, kpos < s_valid)
        sc_ref[c] = jnp.where(vis, acc, -jnp.inf)
        return carry

    lax.fori_loop(0, n_kv, p1, 0)

    def count_gt(t_b):
        def body(c, acc):
            return acc + jnp.where(sc_ref[c] > t_b, 1.0, 0.0)
        acc = lax.fori_loop(0, n_kv, body, jnp.zeros((tq, LANES), F32))
        return bcast(jnp.sum(acc, axis=1, keepdims=True))

    def next_above(t_b):
        def body(c, acc):
            x = sc_ref[c]
            return jnp.minimum(acc, jnp.where(x > t_b, x, jnp.inf))
        acc = lax.fori_loop(0, n_kv, body, jnp.full((tq, LANES), jnp.inf, F32))
        return bcast(jnp.min(acc, axis=1, keepdims=True))

    def stat(c, carry):
        mn, mx = carry
        x = sc_ref[c]
        return jnp.minimum(mn, jnp.where(x == -jnp.inf, jnp.inf, x)), jnp.maximum(mx, x)

    mn, mx = lax.fori_loop(0, n_kv, stat, (jnp.full((tq, LANES), jnp.inf, F32), jnp.full((tq, LANES), -jnp.inf, F32)))
    rmin = bcast(jnp.min(mn, axis=1, keepdims=True))
    rmax = bcast(jnp.max(mx, axis=1, keepdims=True))
    c_min = count_gt(rmin)
    at_min = c_min < kkf

    def bisect(i, carry):
        lo, hi = carry
        mid = lo + 0.5 * (hi - lo)
        up = count_gt(mid) >= kkf
        return jnp.where(up, mid, lo), jnp.where(up, hi, mid)

    lo, hi = lax.fori_loop(0, N_BISECT, bisect, (rmin, rmax))
    tau = next_above(lo)
    ct = count_gt(tau)

    def pending(ct_):
        return jnp.logical_and(ct_ >= kkf, jnp.logical_not(at_min))

    def w_cond(st):
        return jnp.max(jnp.where(pending(st[2]), 1.0, 0.0)) > 0.5

    def w_body(st):
        lo_, tau_, ct_ = st
        need = pending(ct_)
        lo2 = jnp.where(need, tau_, lo_)
        tau2 = next_above(lo2)
        ct2 = count_gt(tau2)
        return lo2, jnp.where(need, tau2, tau_), jnp.where(need, ct2, ct_)

    lo, tau, ct = lax.while_loop(w_cond, w_body, (lo, tau, ct))
    tau = jnp.where(at_min, rmin, tau)
    m_need = kkf - jnp.where(at_min, c_min, ct)

    r128 = lax.broadcasted_iota(jnp.int32, (KV_BLOCK, KV_BLOCK), 0)
    c128 = lax.broadcasted_iota(jnp.int32, (KV_BLOCK, KV_BLOCK), 1)
    before = jnp.where(r128 < c128, 1.0, 0.0).astype(BF16)

    def p2(c, cum):
        x = sc_ref[c]
        eq = x == tau
        eqf = jnp.where(eq, 1.0, 0.0)
        pre = jnp.dot(eqf.astype(BF16), before, preferred_element_type=F32) + cum
        sel = jnp.where(x > tau, 1.0, jnp.where(eq, jnp.where(pre < m_need, 1.0, 0.0), 0.0)) > 0.5
        cum = cum + bcast(jnp.sum(eqf, axis=1, keepdims=True))
        koff = pl.multiple_of(c * KV_BLOCK, KV_BLOCK)
        for h in range(N_HEADS):
            pair = (h // (N_HEADS // N_KV_HEADS)) // 2
            kp = k_ref[0, pl.ds(koff, KV_BLOCK), pair * LANES:(pair + 1) * LANES]
            vp = v_ref[0, pl.ds(koff, KV_BLOCK), pair * LANES:(pair + 1) * LANES]
            s = lax.dot_general(qh_ref[h], kp, NT, preferred_element_type=F32)
            sm = jnp.where(sel, s, NEG)
            m_old = m_ref[h]
            m_new = jnp.maximum(m_old, bcast(jnp.max(sm, axis=1, keepdims=True)))
            pexp = jnp.where(sel, jnp.exp(sm - m_new), 0.0)
            alpha = jnp.exp(m_old - m_new)
            l_ref[h] = alpha * l_ref[h] + bcast(jnp.sum(pexp, axis=1, keepdims=True))
            acc_ref[h] = alpha * acc_ref[h] + jnp.dot(pexp.astype(BF16), vp, preferred_element_type=F32)
            m_ref[h] = m_new
        return cum

    lax.fori_loop(0, n_kv, p2, jnp.zeros((tq, LANES), F32))

    def head_out(h):
        r = acc_ref[h] / l_ref[h]
        if (h // (N_HEADS // N_KV_HEADS)) % 2 != h % 2:
            r = pltpu.roll(r, HEAD_DIM, 1)
        return r

    for j in range(N_HEADS // 2):
        o_ref[0, :, j * LANES:(j + 1) * LANES] = jnp.where(lo_half, head_out(2 * j), head_out(2 * j + 1)).astype(o_ref.dtype)


def _attention(q, qi, wi, k_all, v_all, ki2, *, tq, pos0, s_valid, kk):
    b, tq_pad, _ = q.shape
    s_pad = k_all.shape[1]
    nblk = s_pad // KV_BLOCK
    qblk = lambda w: pl.BlockSpec((1, tq, w), lambda bi, qi_: (bi, qi_, 0))
    seq = lambda w: pl.BlockSpec((1, s_pad, w), lambda bi, qi_: (bi, 0, 0))
    tile = lambda n, dt: pltpu.VMEM((n, tq, LANES), dt)
    return pl.pallas_call(
        functools.partial(_attn_kernel, tq=tq, nblk=nblk, pos0=pos0, s_valid=s_valid, kk=kk),
        out_shape=jax.ShapeDtypeStruct((b, tq_pad, D_ATTN), BF16),
        grid=(b, tq_pad // tq),
        in_specs=[qblk(D_ATTN), qblk(IDX_HEADS * IDX_DIM), qblk(LANES), seq(D_KV), seq(D_KV), seq(LANES)],
        out_specs=qblk(D_ATTN),
        scratch_shapes=[tile(nblk, F32), tile(N_HEADS, BF16), tile(IDX_HEADS, BF16), tile(IDX_HEADS, F32),
                        tile(N_HEADS, F32), tile(N_HEADS, F32), tile(N_HEADS, F32)],
        compiler_params=_cparams("parallel", "parallel"),
    )(q, qi, wi, k_all, v_all, ki2)


POOL_HALO = 16


def _pool_kernel(u_ref, halo_ref, st_ref, w_ref, scale_ref, o_ref, ext_ref, *, tb, pos0):
    i = pl.program_id(1)
    ext_ref[0:POOL_HALO] = jnp.where(i == 0, st_ref[0], halo_ref[0])
    ext_ref[POOL_HALO:POOL_HALO + tb] = u_ref[0]
    pos = pos0 + i * tb + lax.broadcasted_iota(jnp.int32, (tb, 1), 0)
    for gi, w in enumerate(POOL_WINDOWS):
        sl = slice(gi * POOL_GROUP, (gi + 1) * POOL_GROUP)
        tot = ext_ref[POOL_HALO:POOL_HALO + tb, sl]
        for k in range(1, w):
            tot = tot + ext_ref[POOL_HALO - k:POOL_HALO - k + tb, sl]
        cnt = jnp.minimum(w, pos + 1).astype(F32)
        pooled = tot / cnt - u_ref[0, :, sl]
        mixed = jnp.dot(pooled.astype(BF16), w_ref[gi], preferred_element_type=F32) * scale_ref[:, sl]
        o_ref[0, :, sl] = mixed.astype(o_ref.dtype)


def _pool(u, state16, w, scale, *, pos0):
    b, t, c = u.shape
    tb = SEQ_BLOCK
    hb = tb // POOL_HALO
    return pl.pallas_call(
        functools.partial(_pool_kernel, tb=tb, pos0=pos0),
        out_shape=jax.ShapeDtypeStruct((b, t, c), BF16),
        grid=(b, t // tb),
        in_specs=[pl.BlockSpec((1, tb, c), lambda bi, i: (bi, i, 0)),
                  pl.BlockSpec((1, POOL_HALO, c), lambda bi, i: (bi, jnp.maximum(i * hb - 1, 0), 0)),
                  pl.BlockSpec((1, POOL_HALO, c), lambda bi, i: (bi, 0, 0)),
                  pl.BlockSpec((len(POOL_WINDOWS), POOL_GROUP, POOL_GROUP), lambda bi, i: (0, 0, 0)),
                  pl.BlockSpec((1, c), lambda bi, i: (0, 0))],
        out_specs=pl.BlockSpec((1, tb, c), lambda bi, i: (bi, i, 0)),
        scratch_shapes=[pltpu.VMEM((POOL_HALO + tb, c), F32)],
        compiler_params=_cparams("parallel", "parallel"),
    )(u, u, state16, w, scale)


CONV_HALO = SUBLANES
SSD_GH = SSD_HEADS // SSD_GROUPS
SSD_GW = SSD_GH * SSD_HEAD_DIM


def _ssd_kernel(z_ref, dt_ref, xbc_ref, halo_ref, st_ref, h0_ref, cw_ref, cb_ref, dtb_ref, alog_ref, dsk_ref,
                nw_ref, ex_ref, ext_t_ref, y_ref, hT_ref, ext_ref, h_ref, *, q, t_valid):
    c = pl.program_id(1)

    @pl.when(c == 0)
    def _():
        h_ref[...] = h0_ref[0]

    ext_ref[0:CONV_HALO] = jnp.where(c == 0, st_ref[0], halo_ref[0])
    ext_ref[CONV_HALO:CONV_HALO + q] = xbc_ref[0]
    xc = cb_ref[...]
    for j in range(SSD_CONV):
        off = CONV_HALO - (SSD_CONV - 1) + j
        xc = xc + ext_ref[off:off + q] * cw_ref[j:j + 1, :]
    xs = _silu(xc)
    x = xs[:, :D_SSD]

    lane = lax.broadcasted_iota(jnp.int32, (q, LANES), 1)
    row = lax.broadcasted_iota(jnp.int32, (q, LANES), 0)
    v = dt_ref[0] + dtb_ref[...]
    dt = jnp.maximum(v, 0.0) + jnp.log1p(jnp.exp(-jnp.abs(v)))
    dt = jnp.where(jnp.logical_and(lane < SSD_HEADS, c * q + row < t_valid), dt, 0.0)
    a = dt * (-jnp.exp(alog_ref[...]))
    tri = jnp.where(lane <= row, 1.0, 0.0).astype(BF16)
    causal = lane <= row
    cs = _dot3_left(tri, a)
    cs_t = cs.T
    ex = ex_ref[...]
    cs_last = cs[q - 1:q, :]
    dt_x = _dot3(dt, ex)
    ecs_x = jnp.exp(_dot3(cs, ex))
    dec_x = jnp.exp(_dot3(cs_last - cs, ex))
    xdt = x * dt_x
    xdec = (xdt * dec_x).astype(BF16)
    xdt16 = xdt.astype(BF16)
    rowdec = jnp.exp(_dot3_left(ext_t_ref[...], jnp.broadcast_to(cs_t[:, q - 1:q], (LANES, LANES))))
    lo_half = lane < SSD_HEAD_DIM

    for g in range(SSD_GROUPS):
        bg = xs[:, D_SSD + g * SSD_STATE:D_SSD + (g + 1) * SSD_STATE].astype(BF16)
        cg = xs[:, D_SSD + SSD_GROUPS * SSD_STATE + g * SSD_STATE:D_SSD + SSD_GROUPS * SSD_STATE + (g + 1) * SSD_STATE].astype(BF16)
        cb = lax.dot_general(cg, bg, NT, preferred_element_type=F32)
        hs = slice(g * SSD_GW, (g + 1) * SSD_GW)
        hprev = h_ref[hs, :]
        y_off = lax.dot_general(cg, hprev.astype(BF16), NT, preferred_element_type=F32) * ecs_x[:, hs]
        for jp in range(SSD_GH // 2):
            cols = slice(g * SSD_GW + jp * LANES, g * SSD_GW + (jp + 1) * LANES)
            xp = xdt16[:, cols]
            halves = []
            for e in range(2):
                hd = g * SSD_GH + 2 * jp + e
                seg = cs[:, hd:hd + 1] - cs_t[hd:hd + 1, :]
                lmat = jnp.where(causal, jnp.exp(jnp.where(causal, seg, 0.0)), 0.0)
                halves.append(jnp.dot((cb * lmat).astype(BF16), xp, preferred_element_type=F32))
            y_diag = jnp.where(lo_half, halves[0], halves[1])
            yv = y_diag + y_off[:, jp * LANES:(jp + 1) * LANES] + dsk_ref[:, cols] * x[:, cols]
            zv = z_ref[0, :, cols]
            ext_ref[0:q, cols] = yv * _silu(zv)
        st_new = lax.dot_general(xdec[:, hs], bg, TN, preferred_element_type=F32)
        h_ref[hs, :] = hprev * rowdec[hs, :] + st_new

    for g in range(SSD_GROUPS):
        hs = slice(g * SSD_GW, (g + 1) * SSD_GW)
        yg = ext_ref[0:q, hs]
        yn = yg * lax.rsqrt(jnp.mean(yg * yg, axis=-1, keepdims=True) + EPS) * nw_ref[:, hs]
        y_ref[0, :, hs] = yn.astype(y_ref.dtype)

    @pl.when(c == pl.num_programs(1) - 1)
    def _():
        hT_ref[0] = h_ref[...]


def _dot3_left(w, v):
    h1, h2, h3 = _split3(v)
    return (jnp.dot(w, h1, preferred_element_type=F32) + jnp.dot(w, h2, preferred_element_type=F32)
            + jnp.dot(w, h3, preferred_element_type=F32))


def _ssd(zdt, xbc, state8, h0, cw, cb, dtb, alog, dsk, nw, ex, ext_t, *, t_valid):
    b, t, _ = xbc.shape
    q = SEQ_BLOCK
    hb = q // CONV_HALO
    cst = lambda r, c_: pl.BlockSpec((r, c_), lambda bi, i: (0, 0))
    return pl.pallas_call(
        functools.partial(_ssd_kernel, q=q, t_valid=t_valid),
        out_shape=[jax.ShapeDtypeStruct((b, t, D_SSD), BF16), jax.ShapeDtypeStruct((b, D_SSD, SSD_STATE), F32)],
        grid=(b, t // q),
        in_specs=[pl.BlockSpec((1, q, D_SSD), lambda bi, i: (bi, i, 0)),
                  pl.BlockSpec((1, q, LANES), lambda bi, i: (bi, i, D_SSD // LANES)),
                  pl.BlockSpec((1, q, D_SSD_CONV), lambda bi, i: (bi, i, 0)),
                  pl.BlockSpec((1, CONV_HALO, D_SSD_CONV), lambda bi, i: (bi, jnp.maximum(i * hb - 1, 0), 0)),
                  pl.BlockSpec((1, CONV_HALO, D_SSD_CONV), lambda bi, i: (bi, 0, 0)),
                  pl.BlockSpec((1, D_SSD, SSD_STATE), lambda bi, i: (bi, 0, 0)),
                  cst(SSD_CONV, D_SSD_CONV), cst(1, D_SSD_CONV), cst(1, LANES), cst(1, LANES), cst(1, D_SSD),
                  cst(1, D_SSD), cst(LANES, D_SSD), cst(D_SSD, LANES)],
        out_specs=[pl.BlockSpec((1, q, D_SSD), lambda bi, i: (bi, i, 0)),
                   pl.BlockSpec((1, D_SSD, SSD_STATE), lambda bi, i: (bi, 0, 0))],
        scratch_shapes=[pltpu.VMEM((CONV_HALO + q, D_SSD_CONV), F32), pltpu.VMEM((D_SSD, SSD_STATE), F32)],
        compiler_params=_cparams("parallel", "arbitrary"),
    )(zdt, zdt, xbc, xbc, state8, h0, cw, cb, dtb, alog, dsk, nw, ex, ext_t)


def _merge_kernel(x_ref, oa_ref, op_ref, os_ref, gt_ref, g1_ref, wa_ref, wp_ref, ws_ref, wo_ref, o_ref):
    def gate(bi):
        return jax.nn.sigmoid(gt_ref[:, bi * D_MODEL:(bi + 1) * D_MODEL])
    merged = (gate(0) * jnp.dot(oa_ref[...], wa_ref[...], preferred_element_type=F32)
              + gate(1) * jnp.dot(op_ref[...], wp_ref[...], preferred_element_type=F32)
              + gate(2) * jnp.dot(os_ref[...], ws_ref[...], preferred_element_type=F32))
    o_ref[...] = x_ref[...] + g1_ref[0] * jnp.dot(merged.astype(BF16), wo_ref[...], preferred_element_type=F32)


def _merge(x, oa, op, os_, gates, mv_g1, mv_idx, wa, wp, ws, wo, tm):
    n, d = x.shape
    rb = mv_g1.shape[1]
    rowb = lambda w: pl.BlockSpec((tm, w), lambda i: (i, 0))
    cst = lambda r, c: pl.BlockSpec((r, c), lambda i: (0, 0))
    return pl.pallas_call(
        _merge_kernel,
        out_shape=jax.ShapeDtypeStruct((n, d), F32),
        grid=(n // tm,),
        in_specs=[rowb(d), rowb(D_ATTN), rowb(D_POOL), rowb(D_SSD), rowb(N_BRANCH * D_MODEL),
                  pl.BlockSpec((1, rb, d), lambda i: (mv_idx(i), 0, 0)),
                  cst(D_ATTN, d), cst(D_POOL, d), cst(D_SSD, d), cst(d, d)],
        out_specs=rowb(d),
        compiler_params=_cparams("parallel"),
    )(x, oa, op, os_, gates, mv_g1, wa, wp, ws, wo)


def _ffn_down_kernel(x_ref, ua_ref, ug_ref, ha_ref, hg_ref, sa_ref, sg_ref, g2_ref, cwa_ref, cwg_ref, cba_ref,
                     cbg_ref, wd_ref, o_ref, ea_ref, eg_ref, *, tb):
    i = pl.program_id(1)
    first = i == 0

    def conv(e_ref, u_ref, h_ref, s_ref, cw_ref, cb_ref):
        e_ref[0:CONV_HALO] = jnp.where(first, s_ref[0], h_ref[0])
        e_ref[CONV_HALO:CONV_HALO + tb] = u_ref[0]
        out = cb_ref[...]
        for j in range(FFN_CONV):
            off = CONV_HALO - (FFN_CONV - 1) + j
            out = out + e_ref[off:off + tb] * cw_ref[j:j + 1, :]
        return out

    a = conv(ea_ref, ua_ref, ha_ref, sa_ref, cwa_ref, cba_ref)
    g = conv(eg_ref, ug_ref, hg_ref, sg_ref, cwg_ref, cbg_ref)
    act = (_silu(g) * a).astype(BF16)
    o_ref[0] = x_ref[0] + g2_ref[0] * jnp.dot(act, wd_ref[...], preferred_element_type=F32)


def _ffn_down(x, ua, ug, sa, sg, g2, cwa, cwg, cba, cbg, wd):
    b, t, d = x.shape
    f = ua.shape[2]
    tb = SEQ_BLOCK
    hb = tb // CONV_HALO
    blk = lambda w: pl.BlockSpec((1, tb, w), lambda bi, i: (bi, i, 0))
    halo = pl.BlockSpec((1, CONV_HALO, f), lambda bi, i: (bi, jnp.maximum(i * hb - 1, 0), 0))
    st = pl.BlockSpec((1, CONV_HALO, f), lambda bi, i: (bi, 0, 0))
    cst = lambda r, c: pl.BlockSpec((r, c), lambda bi, i: (0, 0))
    return pl.pallas_call(
        functools.partial(_ffn_down_kernel, tb=tb),
        out_shape=jax.ShapeDtypeStruct((b, t, d), F32),
        grid=(b, t // tb),
        in_specs=[blk(d), blk(f), blk(f), halo, halo, st, st,
                  pl.BlockSpec((1, 1, d), lambda bi, i: (bi, 0, 0)),
                  cst(FFN_CONV, f), cst(FFN_CONV, f), cst(1, f), cst(1, f), cst(f, d)],
        out_specs=blk(d),
        scratch_shapes=[pltpu.VMEM((CONV_HALO + tb, f), F32), pltpu.VMEM((CONV_HALO + tb, f), F32)],
        compiler_params=_cparams("parallel", "parallel"),
    )(x, ua, ug, ua, ug, sa, sg, g2, cwa, cwg, cba, cbg, wd)


def _pad_rows(a, rows, front=False):
    extra = rows - a.shape[1]
    if extra == 0:
        return a
    return jnp.pad(a, ((0, 0), (extra, 0) if front else (0, extra), (0, 0)))


def _rope_tables(pos):
    half = HEAD_DIM // 2
    freqs = ROPE_THETA ** (-jnp.arange(half, dtype=jnp.float32) / half)
    ang = pos.astype(jnp.float32)[:, None] * freqs[None, :]
    cos, sin = jnp.cos(ang), jnp.sin(ang)
    return jnp.concatenate([cos, cos, cos, cos], axis=-1), jnp.concatenate([-sin, sin, -sin, sin], axis=-1)


def _layer_weights(l, w_ada, b_ada, norm1, norm2, w_in, q_norm, k_norm, pool_w, pool_scale, ssd_conv_w, ssd_conv_b,
                   ssd_dt_bias, ssd_a_log, ssd_d, ssd_norm, w_branch_attn, w_branch_pool, w_branch_ssd, w_out,
                   ffn_up, ffn_conv_w, ffn_conv_b, ffn_down):
    o = _OFFS
    win = w_in[l]
    col = lambda a, b: win[:, a:b]
    zpad = lambda n: jnp.zeros((D_MODEL, n), win.dtype)
    row = lambda v: v.reshape(1, -1).astype(F32)
    lanepad = lambda v: jnp.pad(v.astype(F32), (0, LANES - v.shape[0])).reshape(1, LANES)
    return dict(
        w_ada=w_ada[l].astype(BF16), b_ada=row(b_ada[l]), norm1=row(norm1[l]), norm2=row(norm2[l]),
        w_attn=jnp.concatenate([col(o[0], o[6]), zpad(LANES - IDX_DIM - IDX_HEADS)], axis=1).astype(BF16),
        w_pool=col(o[6], o[7]).astype(BF16),
        w_zdt=jnp.concatenate([col(o[7], o[8]), col(o[9], o[10]), zpad(LANES - SSD_HEADS)], axis=1).astype(BF16),
        w_xbc=col(o[8], o[9]).astype(BF16),
        w_gate=col(o[10], o[11]).astype(BF16),
        qn=row(jnp.tile(q_norm[l], LANES // HEAD_DIM)), kn=row(jnp.tile(k_norm[l], LANES // HEAD_DIM)),
        pool_w=pool_w[l].astype(BF16), pool_scale=row(pool_scale[l]),
        cw=ssd_conv_w[l].astype(F32), cb=row(ssd_conv_b[l]), dtb=lanepad(ssd_dt_bias[l]), alog=lanepad(ssd_a_log[l]),
        dsk=row(jnp.repeat(ssd_d[l], SSD_HEAD_DIM)), nw=row(ssd_norm[l]),
        wa=w_branch_attn[l].astype(BF16), wp=w_branch_pool[l].astype(BF16), ws=w_branch_ssd[l].astype(BF16),
        wo=w_out[l].astype(BF16),
        up_a=ffn_up[l][:, :D_FF].astype(BF16), up_g=ffn_up[l][:, D_FF:].astype(BF16),
        cwa=ffn_conv_w[l][:, :D_FF].astype(F32), cwg=ffn_conv_w[l][:, D_FF:].astype(F32),
        cba=row(ffn_conv_b[l][:D_FF]), cbg=row(ffn_conv_b[l][D_FF:]), wd=ffn_down[l].astype(BF16))


def _group_layer(x, mod, lw, consts, *, pos0, past, states):
    b, t, d = x.shape
    n = b * t
    tp = -(-t // SEQ_BLOCK) * SEQ_BLOCK
    per_seq = t % SEQ_BLOCK == 0
    tm = 256 if per_seq else n
    sh1, sc1, g1, sh2, sc2, g2 = [mod[:, i * d:(i + 1) * d] for i in range(6)]
    pos = pos0 + jnp.arange(t)
    cos_t, sin_t = _rope_tables(pos)
    if per_seq:
        view = lambda m: m.reshape(b, 1, d)
        mv_idx = lambda i: i // (t // tm)
        tab = lambda a: a.reshape(t // tm, tm, LANES)
        tab_idx = lambda i: i % (t // tm)
    else:
        view = lambda m: jnp.repeat(m, t, axis=0).reshape(1, n, d)
        mv_idx = lambda i: 0
        tab = lambda a: jnp.tile(a, (b, 1)).reshape(1, n, LANES)
        tab_idx = lambda i: 0
    xf = x.reshape(n, d)
    nm = lambda w: _norm_matmul(xf, view(sh1), view(sc1), mv_idx, lw['norm1'], w, tm)
    pa, pool_u, zdt, xbc, gates = nm(lw['w_attn']), nm(lw['w_pool']), nm(lw['w_zdt']), nm(lw['w_xbc']), nm(lw['w_gate'])

    q_r, k_r, qi_r, ki2, wi_s = _attn_prep(pa, tab(cos_t), tab(sin_t), tab_idx, lw['qn'], lw['kn'], consts['seg'], tm)
    v_new = pa[:, D_ATTN + D_KV:D_ATTN + 2 * D_KV]
    seq3 = lambda a: a.reshape(b, t, a.shape[-1])
    if past is None:
        tq = KV_BLOCK
        k_all, v_all, ki_all = seq3(k_r).astype(BF16), seq3(v_new).astype(BF16), seq3(ki2).astype(BF16)
        s_valid = t
        qpad = lambda a: seq3(a)
    else:
        tq = 2 * SUBLANES
        page_table, ck, cv, ci = past
        page = ck.shape[1]
        newp = lambda a: _pad_rows(seq3(a), page)
        k_all, v_all, ki_past = _page_gather(page_table, ck, cv, ci, newp(k_r), newp(v_new), newp(ki2[:, :IDX_DIM]))
        ki_all = jnp.concatenate([ki_past, ki_past], axis=-1)
        s_valid = pos0 + t
        qpad = lambda a: _pad_rows(seq3(a), tq)
    kk = min(INDEX_TOPK, s_valid // 4)
    o_attn = _attention(qpad(q_r), qpad(qi_r), qpad(wi_s), k_all, v_all, ki_all,
                        tq=tq, pos0=pos0, s_valid=s_valid, kk=kk)[:, :t].reshape(n, D_ATTN)

    st_pool, st_sconv, st_ssd, st_fconv = states
    seqp = lambda a: _pad_rows(seq3(a), tp)
    o_pool = _pool(seqp(pool_u), _pad_rows(st_pool, POOL_HALO, front=True), lw['pool_w'], lw['pool_scale'], pos0=pos0)
    o_ssd, ssd_state = _ssd(seqp(zdt), seqp(xbc), _pad_rows(st_sconv, CONV_HALO, front=True),
                            st_ssd.reshape(b, D_SSD, SSD_STATE), lw['cw'], lw['cb'], lw['dtb'], lw['alog'], lw['dsk'],
                            lw['nw'], consts['ex'], consts['ext_t'], t_valid=t)
    unpad = lambda a: a[:, :t].reshape(n, a.shape[-1])
    x1 = _merge(xf, o_attn, unpad(o_pool), unpad(o_ssd), gates, view(g1), mv_idx,
                lw['wa'], lw['wp'], lw['ws'], lw['wo'], tm)

    nm2 = lambda w: _norm_matmul(x1, view(sh2), view(sc2), mv_idx, lw['norm2'], w, tm)
    ua, ug = nm2(lw['up_a']), nm2(lw['up_g'])
    st_f = _pad_rows(st_fconv, CONV_HALO, front=True)
    x2 = _ffn_down(seqp(x1), seqp(ua), seqp(ug), st_f[:, :, :D_FF], st_f[:, :, D_FF:], g2.reshape(b, 1, d),
                   lw['cwa'], lw['cwg'], lw['cba'], lw['cbg'], lw['wd'])[:, :t]

    tail = lambda prev, new, keep: jnp.concatenate([prev, seq3(new)], axis=1)[:, -keep:]
    new_states = (seq3(k_r).reshape(b, t, N_KV_HEADS, HEAD_DIM), seq3(v_new).reshape(b, t, N_KV_HEADS, HEAD_DIM),
                  seq3(ki2)[:, :, :IDX_DIM], tail(st_pool, pool_u, POOL_CTX), tail(st_sconv, xbc, SSD_CONV - 1),
                  ssd_state.reshape(b, SSD_HEADS, SSD_HEAD_DIM, SSD_STATE),
                  tail(st_fconv, jnp.concatenate([ua, ug], axis=-1), FFN_CONV - 1))
    return x2, new_states


def kernel(x_prompt, x_sample, c_prompt, c_sample, cache_k, cache_v, cache_kidx, page_table, state_pool, state_ssd_conv, state_ssd, state_ffn_conv, w_ada, b_ada, norm1, norm2, w_in, q_norm, k_norm, pool_w, pool_scale, ssd_conv_w, ssd_conv_b, ssd_dt_bias, ssd_a_log, ssd_d, ssd_norm, w_branch_attn, w_branch_pool, w_branch_ssd, w_out, ffn_up, ffn_conv_w, ffn_conv_b, ffn_down):
    bp, tp, d = x_prompt.shape
    bs, ts, _ = x_sample.shape
    depth = w_ada.shape[0]
    n_pool, page = cache_k.shape[1], cache_k.shape[2]
    past_len = page_table.shape[1] * page
    dt = x_prompt.dtype

    r = lax.broadcasted_iota(jnp.int32, (LANES, LANES), 0)
    c = lax.broadcasted_iota(jnp.int32, (LANES, LANES), 1)
    hr = lax.broadcasted_iota(jnp.int32, (LANES, D_SSD), 0)
    hc = lax.broadcasted_iota(jnp.int32, (LANES, D_SSD), 1)
    ex = (hr == hc // SSD_HEAD_DIM).astype(BF16)
    consts = dict(seg=(r // HEAD_DIM == c // HEAD_DIM).astype(BF16), ex=ex, ext_t=ex.T)

    c_all = jnp.concatenate([c_prompt, c_sample], axis=0)
    rows = -(-c_all.shape[0] // SUBLANES) * SUBLANES
    c_pad = jnp.pad(c_all, ((0, rows - c_all.shape[0]), (0, 0)))

    zeros = lambda *s: jnp.zeros(s, dt)
    xp, xs = x_prompt, x_sample
    new_p, new_s = [[] for _ in range(7)], [[] for _ in range(7)]
    for l in range(depth):
        lw = _layer_weights(l, w_ada, b_ada, norm1, norm2, w_in, q_norm, k_norm, pool_w, pool_scale, ssd_conv_w,
                            ssd_conv_b, ssd_dt_bias, ssd_a_log, ssd_d, ssd_norm, w_branch_attn, w_branch_pool,
                            w_branch_ssd, w_out, ffn_up, ffn_conv_w, ffn_conv_b, ffn_down)
        mod = _mod(c_pad, lw['w_ada'], lw['b_ada'])
        xp, st_p = _group_layer(
            xp, mod[:bp], lw, consts, pos0=0, past=None,
            states=(zeros(bp, POOL_CTX, D_POOL), zeros(bp, SSD_CONV - 1, D_SSD_CONV),
                    zeros(bp, SSD_HEADS, SSD_HEAD_DIM, SSD_STATE), zeros(bp, FFN_CONV - 1, 2 * D_FF)))
        past = (page_table, cache_k[l].reshape(n_pool, page, D_KV), cache_v[l].reshape(n_pool, page, D_KV), cache_kidx[l])
        xs, st_s = _group_layer(xs, mod[bp:bp + bs], lw, consts, pos0=past_len, past=past,
                                states=(state_pool[l], state_ssd_conv[l], state_ssd[l], state_ffn_conv[l]))
        for j in range(7):
            new_p[j].append(st_p[j])
            new_s[j].append(st_s[j])
    outs_p = [jnp.stack(s) for s in new_p]
    outs_s = [jnp.stack(s) for s in new_s]
    return (xp, xs, *outs_p, *outs_s)
```

```python
import functools

import numpy as np
import jax
import jax.numpy as jnp
from jax import lax
from jax.experimental import pallas as pl
from jax.experimental.pallas import tpu as pltpu

D_MODEL = 1024
N_HEADS = 16
N_KV_HEADS = 4
HEAD_DIM = 64
D_ATTN = N_HEADS * HEAD_DIM
D_KV = N_KV_HEADS * HEAD_DIM
IDX_HEADS = 8
IDX_DIM = 64
INDEX_TOPK = 256
ROPE_THETA = 10000.0
POOL_WINDOWS = (2, 4, 8, 16)
D_POOL = D_MODEL
POOL_GROUP = D_POOL // len(POOL_WINDOWS)
POOL_CTX = max(POOL_WINDOWS) - 1
D_SSD = 2 * D_MODEL
SSD_HEAD_DIM = 64
SSD_HEADS = D_SSD // SSD_HEAD_DIM
SSD_GROUPS = 4
SSD_STATE = 128
SSD_CONV = 4
D_SSD_CONV = D_SSD + 2 * SSD_GROUPS * SSD_STATE
D_FF = 2816
FFN_CONV = 3
N_BRANCH = 3
EPS = 1e-6
_SPLITS = (D_ATTN, D_KV, D_KV, IDX_HEADS * IDX_DIM, IDX_DIM, IDX_HEADS, D_POOL, D_SSD, D_SSD_CONV,
           SSD_HEADS, N_BRANCH * D_MODEL)
_OFFS = tuple(int(v) for v in np.cumsum((0,) + _SPLITS))

F32 = jnp.float32
BF16 = jnp.bfloat16
LANES = 128
SUBLANES = 8
SEQ_BLOCK = 128
ATTN_Q_ROWS = 128
ATTN_KEYS = 256
ATTN_KEYS_DECODE = 512
VMEM_LIMIT = 56 * 1024 * 1024
NEG = -1e30
N_BISECT = 24
NT = (((1,), (1,)), ((), ()))
TN = (((0,), (0,)), ((), ()))


def _cparams(*sem):
    return pltpu.CompilerParams(dimension_semantics=sem, vmem_limit_bytes=VMEM_LIMIT)


def _silu(v):
    return v * jax.nn.sigmoid(v)


def _split3(v):
    h1 = v.astype(BF16)
    r1 = v - h1.astype(F32)
    h2 = r1.astype(BF16)
    h3 = (r1 - h2.astype(F32)).astype(BF16)
    return h1, h2, h3


def _dot3(v, w):
    h1, h2, h3 = _split3(v)
    return (jnp.dot(h1, w, preferred_element_type=F32) + jnp.dot(h2, w, preferred_element_type=F32)
            + jnp.dot(h3, w, preferred_element_type=F32))


def _mod_kernel(c_ref, w_ref, b_ref, o_ref):
    s = _silu(c_ref[...])
    o_ref[...] = jnp.dot(s.astype(BF16), w_ref[...], preferred_element_type=F32) + b_ref[...]


def _mod(c_pad, w, b):
    rows, d = c_pad.shape
    n = w.shape[1]
    tn = n // 4
    return pl.pallas_call(
        _mod_kernel,
        name="adaln_mod",
        out_shape=jax.ShapeDtypeStruct((rows, n), F32),
        grid=(n // tn,),
        in_specs=[pl.BlockSpec((rows, d), lambda j: (0, 0)),
                  pl.BlockSpec((d, tn), lambda j: (0, j)),
                  pl.BlockSpec((1, tn), lambda j: (0, j))],
        out_specs=pl.BlockSpec((rows, tn), lambda j: (0, j)),
        compiler_params=_cparams("parallel"),
    )(c_pad, w, b)


def _norm_matmul_kernel(x_ref, sh_ref, sc_ref, g_ref, w_ref, o_ref):
    x = x_ref[...]
    y = x * lax.rsqrt(jnp.mean(x * x, axis=-1, keepdims=True) + EPS) * g_ref[...]
    h = y * (1.0 + sc_ref[0]) + sh_ref[0]
    o_ref[...] = jnp.dot(h.astype(BF16), w_ref[...], preferred_element_type=F32).astype(o_ref.dtype)


def _norm_matmul(x, mv_sh, mv_sc, mv_idx, g, w, tm):
    n, d = x.shape
    nout = w.shape[1]
    rb = mv_sh.shape[1]
    return pl.pallas_call(
        _norm_matmul_kernel,
        name="norm_matmul",
        out_shape=jax.ShapeDtypeStruct((n, nout), F32),
        grid=(n // tm,),
        in_specs=[pl.BlockSpec((tm, d), lambda i: (i, 0)),
                  pl.BlockSpec((1, rb, d), lambda i: (mv_idx(i), 0, 0)),
                  pl.BlockSpec((1, rb, d), lambda i: (mv_idx(i), 0, 0)),
                  pl.BlockSpec((1, d), lambda i: (0, 0)),
                  pl.BlockSpec((d, nout), lambda i: (0, 0))],
        out_specs=pl.BlockSpec((tm, nout), lambda i: (i, 0)),
        compiler_params=_cparams("parallel"),
    )(x, mv_sh, mv_sc, g, w)


def _attn_prep_kernel(q_ref, k_ref, qi_ref, kw_ref, cos_ref, sin_ref, qn_ref, kn_ref, seg_ref,
                      qo_ref, ko_ref, qio_ref, kio_ref, wio_ref):
    tm = q_ref.shape[0]
    lane = lax.broadcasted_iota(jnp.int32, (tm, LANES), 1)
    first_half = (lane % HEAD_DIM) < (HEAD_DIM // 2)
    cos = cos_ref[0]
    sin = sin_ref[0]
    seg = seg_ref[...]

    def rope(v):
        up = pltpu.roll(v, LANES - HEAD_DIM // 2, 1)
        dn = pltpu.roll(v, HEAD_DIM // 2, 1)
        return v * cos + jnp.where(first_half, up, dn) * sin

    def headnorm(v, gain):
        sq = v * v
        hi = sq.astype(BF16)
        lo = (sq - hi.astype(F32)).astype(BF16)
        ss = jnp.dot(hi, seg, preferred_element_type=F32) + jnp.dot(lo, seg, preferred_element_type=F32)
        return v * lax.rsqrt(ss * (1.0 / HEAD_DIM) + EPS) * gain

    qn = qn_ref[...]
    kn = kn_ref[...]
    for j in range(D_ATTN // LANES):
        sl = slice(j * LANES, (j + 1) * LANES)
        qo_ref[:, sl] = rope(headnorm(q_ref[:, sl], qn)) * (HEAD_DIM ** -0.5)
    for j in range(D_KV // LANES):
        sl = slice(j * LANES, (j + 1) * LANES)
        ko_ref[:, sl] = rope(headnorm(k_ref[:, sl], kn))
    for j in range(IDX_HEADS * IDX_DIM // LANES):
        sl = slice(j * LANES, (j + 1) * LANES)
        qio_ref[:, sl] = rope(qi_ref[:, sl])
    kw = kw_ref[...]
    kr = rope(kw)
    kio_ref[...] = jnp.where(lane < IDX_DIM, kr, pltpu.roll(kr, IDX_DIM, 1))
    wio_ref[...] = pltpu.roll(kw, LANES - IDX_DIM, 1) * (IDX_HEADS ** -0.5 * IDX_DIM ** -0.5)


def _attn_prep(pa, cos_t, sin_t, tab_idx, qn, kn, seg, tm):
    n = pa.shape[0]
    blk = lambda w, c: pl.BlockSpec((tm, w), lambda i: (i, c))
    tab = pl.BlockSpec((1, tm, LANES), lambda i: (tab_idx(i), 0, 0))
    cst = lambda r, c: pl.BlockSpec((r, c), lambda i: (0, 0))
    return pl.pallas_call(
        _attn_prep_kernel,
        name="attn_prep",
        out_shape=[jax.ShapeDtypeStruct((n, D_ATTN), F32), jax.ShapeDtypeStruct((n, D_KV), F32),
                   jax.ShapeDtypeStruct((n, IDX_HEADS * IDX_DIM), F32),
                   jax.ShapeDtypeStruct((n, LANES), F32), jax.ShapeDtypeStruct((n, LANES), F32)],
        grid=(n // tm,),
        in_specs=[blk(D_ATTN, 0), blk(D_KV, D_ATTN // D_KV), blk(IDX_HEADS * IDX_DIM, 3), blk(LANES, 16),
                  tab, tab, cst(1, LANES), cst(1, LANES), cst(LANES, LANES)],
        out_specs=[blk(D_ATTN, 0), blk(D_KV, 0), blk(IDX_HEADS * IDX_DIM, 0), blk(LANES, 0), blk(LANES, 0)],
        compiler_params=_cparams("parallel"),
    )(pa, pa, pa, pa, cos_t, sin_t, qn, kn, seg)


GATHER_PAGES = 8


def _page_gather_kernel(pt_ref, *refs, n_steps, pg, page):
    ins, outs = refs[:3 * pg + 3], refs[3 * pg + 3:]
    past = pl.program_id(1) < n_steps
    for t in range(3):
        new_ref, out_ref = ins[3 * pg + t], outs[t]
        for i in range(pg):
            rows = slice(i * page, (i + 1) * page)
            val = jnp.where(past, ins[t * pg + i][0], new_ref[0, rows, :])
            if t == 2:
                val = jnp.concatenate([val, val], axis=-1)
            out_ref[0, rows, :] = val.astype(out_ref.dtype)


def _page_gather(page_table, ck, cv, ci, nk, nv, ni):
    bd, n_pages = page_table.shape
    page = ck.shape[1]
    pg = max(g for g in range(1, GATHER_PAGES + 1) if n_pages % g == 0)
    n_steps = n_pages // pg
    nk, nv, ni = [_pad_rows(a, pg * page) for a in (nk, nv, ni)]

    def cache(w, i):
        return pl.BlockSpec((1, page, w), lambda b, p, pt: (pt[b, jnp.minimum(p, n_steps - 1) * pg + i], 0, 0))

    new = lambda w: pl.BlockSpec((1, pg * page, w), lambda b, p, pt: (b, 0, 0))
    out = lambda w: pl.BlockSpec((1, pg * page, w), lambda b, p, pt: (b, p, 0))
    s_pad = (n_steps + 1) * pg * page
    in_specs = ([cache(D_KV, i) for i in range(pg)] + [cache(D_KV, i) for i in range(pg)]
                + [cache(IDX_DIM, i) for i in range(pg)] + [new(D_KV), new(D_KV), new(IDX_DIM)])
    return pl.pallas_call(
        functools.partial(_page_gather_kernel, n_steps=n_steps, pg=pg, page=page),
        out_shape=[jax.ShapeDtypeStruct((bd, s_pad, D_KV), BF16), jax.ShapeDtypeStruct((bd, s_pad, D_KV), BF16),
                   jax.ShapeDtypeStruct((bd, s_pad, 2 * IDX_DIM), BF16)],
        grid_spec=pltpu.PrefetchScalarGridSpec(
            num_scalar_prefetch=1, grid=(bd, n_steps + 1), in_specs=in_specs,
            out_specs=[out(D_KV), out(D_KV), out(2 * IDX_DIM)]),
        compiler_params=_cparams("parallel", "parallel"),
        name="page_gather",
    )(page_table, *([ck] * pg), *([cv] * pg), *([ci] * pg), nk, nv, ni)


HEADS_PER_PAIR = 2 * (N_HEADS // N_KV_HEADS)


def _attn_kernel(q_ref, qi_ref, wi_ref, k_ref, v_ref, ki_ref, o_ref,
                 sc_ref, qh_ref, qim_ref, wb_ref, m_ref, l_ref, acc_ref, *, tq, tk, nblk, pos0, s_valid, kk):
    q0 = pl.program_id(1) * tq
    nch = tk // LANES
    lane = lax.broadcasted_iota(jnp.int32, (tq, LANES), 1)
    row = lax.broadcasted_iota(jnp.int32, (tq, LANES), 0)
    qpos = pos0 + q0 + row
    lo_half = lane < HEAD_DIM
    n_kv = jnp.minimum(nblk, (pos0 + q0 + tq - 1) // tk + 1)
    kkf = float(kk)
    chunk = lambda a, j: a[..., j * LANES:(j + 1) * LANES]

    def bcast(col):
        return jnp.broadcast_to(col, col.shape[:-1] + (LANES,))

    wi = wi_ref[0]
    for j in range(IDX_HEADS // 2):
        t = qi_ref[0, :, j * LANES:(j + 1) * LANES]
        qim_ref[2 * j * tq:(2 * j + 1) * tq] = jnp.where(lo_half, t, 0.0).astype(BF16)
        qim_ref[(2 * j + 1) * tq:(2 * j + 2) * tq] = jnp.where(lo_half, 0.0, t).astype(BF16)
    for h in range(IDX_HEADS):
        wb_ref[h] = bcast(wi[:, h:h + 1])
    for h in range(N_HEADS):
        t = q_ref[0, :, (h // 2) * LANES:(h // 2 + 1) * LANES]
        kv_half = (h // (N_HEADS // N_KV_HEADS)) % 2
        if h % 2 != kv_half:
            t = pltpu.roll(t, HEAD_DIM, 1)
        keep = lo_half if kv_half == 0 else jnp.logical_not(lo_half)
        hh = h % HEADS_PER_PAIR
        qh_ref[h // HEADS_PER_PAIR, hh * tq:(hh + 1) * tq] = jnp.where(keep, t, 0.0).astype(BF16)
    m_ref[...] = jnp.full(m_ref.shape, NEG, F32)
    l_ref[...] = jnp.zeros(l_ref.shape, F32)
    acc_ref[...] = jnp.zeros(acc_ref.shape, F32)

    def p1(c, carry):
        kid = ki_ref[0, pl.ds(pl.multiple_of(c * tk, tk), tk), :]
        y = lax.dot_general(qim_ref[...], kid, NT, preferred_element_type=F32)
        y = jnp.maximum(y, 0.0).reshape(IDX_HEADS, tq, tk)
        for j in range(nch):
            acc = wb_ref[0] * chunk(y[0], j)
            for h in range(1, IDX_HEADS):
                acc = acc + wb_ref[h] * chunk(y[h], j)
            kpos = c * tk + j * LANES + lane
            vis = jnp.logical_and(kpos <= qpos, kpos < s_valid)
            sc_ref[c, :, j * LANES:(j + 1) * LANES] = jnp.where(vis, acc, -jnp.inf)
        return carry

    lax.fori_loop(0, n_kv, p1, 0)

    def count_gt(t_b):
        def body(c, acc):
            x = sc_ref[c]
            for j in range(nch):
                acc = acc + jnp.where(chunk(x, j) > t_b, 1.0, 0.0)
            return acc
        acc = lax.fori_loop(0, n_kv, body, jnp.zeros((tq, LANES), F32))
        return bcast(jnp.sum(acc, axis=1, keepdims=True))

    def next_above(t_b):
        def body(c, acc):
            x = sc_ref[c]
            for j in range(nch):
                xj = chunk(x, j)
                acc = jnp.minimum(acc, jnp.where(xj > t_b, xj, jnp.inf))
            return acc
        acc = lax.fori_loop(0, n_kv, body, jnp.full((tq, LANES), jnp.inf, F32))
        return bcast(jnp.min(acc, axis=1, keepdims=True))

    def stat(c, carry):
        mn, mx = carry
        x = sc_ref[c]
        for j in range(nch):
            xj = chunk(x, j)
            mn = jnp.minimum(mn, jnp.where(xj == -jnp.inf, jnp.inf, xj))
            mx = jnp.maximum(mx, xj)
        return mn, mx

    mn, mx = lax.fori_loop(0, n_kv, stat, (jnp.full((tq, LANES), jnp.inf, F32), jnp.full((tq, LANES), -jnp.inf, F32)))
    rmin = bcast(jnp.min(mn, axis=1, keepdims=True))
    rmax = bcast(jnp.max(mx, axis=1, keepdims=True))
    c_min = count_gt(rmin)
    at_min = c_min < kkf

    def bisect(i, carry):
        lo, hi = carry
        mid = lo + 0.5 * (hi - lo)
        up = count_gt(mid) >= kkf
        return jnp.where(up, mid, lo), jnp.where(up, hi, mid)

    lo, hi = lax.fori_loop(0, N_BISECT, bisect, (rmin, rmax))
    tau = next_above(lo)
    ct = count_gt(tau)

    def pending(ct_):
        return jnp.logical_and(ct_ >= kkf, jnp.logical_not(at_min))

    def w_cond(st):
        return jnp.max(jnp.where(pending(st[2]), 1.0, 0.0)) > 0.5

    def w_body(st):
        lo_, tau_, ct_ = st
        need = pending(ct_)
        lo2 = jnp.where(need, tau_, lo_)
        tau2 = next_above(lo2)
        ct2 = count_gt(tau2)
        return lo2, jnp.where(need, tau2, tau_), jnp.where(need, ct2, ct_)

    lo, tau, ct = lax.while_loop(w_cond, w_body, (lo, tau, ct))
    tau = jnp.where(at_min, rmin, tau)
    m_need = kkf - jnp.where(at_min, c_min, ct)

    rk = lax.broadcasted_iota(jnp.int32, (tk, tk), 0)
    ck = lax.broadcasted_iota(jnp.int32, (tk, tk), 1)
    before = jnp.where(rk < ck, 1.0, 0.0).astype(BF16)

    def p2(c, cum):
        x = sc_ref[c]
        eqs = [chunk(x, j) == tau for j in range(nch)]
        eqf = [jnp.where(e, 1.0, 0.0) for e in eqs]
        pre = jnp.dot(jnp.concatenate(eqf, axis=-1).astype(BF16), before, preferred_element_type=F32)
        sel = [jnp.where(chunk(x, j) > tau, 1.0,
                         jnp.where(eqs[j], jnp.where(chunk(pre, j) + cum < m_need, 1.0, 0.0), 0.0)) > 0.5
               for j in range(nch)]
        tot = eqf[0]
        for j in range(1, nch):
            tot = tot + eqf[j]
        cum = cum + bcast(jnp.sum(tot, axis=1, keepdims=True))
        koff = pl.multiple_of(c * tk, tk)
        for p in range(N_HEADS // HEADS_PER_PAIR):
            kp = k_ref[0, pl.ds(koff, tk), p * LANES:(p + 1) * LANES]
            vp = v_ref[0, pl.ds(koff, tk), p * LANES:(p + 1) * LANES]
            s = lax.dot_general(qh_ref[p], kp, NT, preferred_element_type=F32).reshape(HEADS_PER_PAIR, tq, tk)
            sm = [jnp.where(sel[j][None], chunk(s, j), NEG) for j in range(nch)]
            mx_ = sm[0]
            for j in range(1, nch):
                mx_ = jnp.maximum(mx_, sm[j])
            m_old = m_ref[p]
            m_new = jnp.maximum(m_old, bcast(jnp.max(mx_, axis=-1, keepdims=True)))
            pe = [jnp.where(sel[j][None], jnp.exp(sm[j] - m_new), 0.0) for j in range(nch)]
            ps = pe[0]
            for j in range(1, nch):
                ps = ps + pe[j]
            alpha = jnp.exp(m_old - m_new)
            l_ref[p] = alpha * l_ref[p] + bcast(jnp.sum(ps, axis=-1, keepdims=True))
            pv = jnp.dot(jnp.concatenate(pe, axis=-1).reshape(HEADS_PER_PAIR * tq, tk).astype(BF16), vp,
                         preferred_element_type=F32)
            acc_ref[p] = alpha.reshape(HEADS_PER_PAIR * tq, LANES) * acc_ref[p] + pv
            m_ref[p] = m_new
        return cum

    lax.fori_loop(0, n_kv, p2, jnp.zeros((tq, LANES), F32))

    def head_out(h):
        p, hh = h // HEADS_PER_PAIR, h % HEADS_PER_PAIR
        r = acc_ref[p, hh * tq:(hh + 1) * tq] / l_ref[p, hh]
        if (h // (N_HEADS // N_KV_HEADS)) % 2 != h % 2:
            r = pltpu.roll(r, HEAD_DIM, 1)
        return r

    for j in range(N_HEADS // 2):
        o_ref[0, :, j * LANES:(j + 1) * LANES] = jnp.where(lo_half, head_out(2 * j), head_out(2 * j + 1)).astype(o_ref.dtype)


def _attention(q, qi, wi, k_all, v_all, ki2, *, tq, tk, pos0, s_valid, kk):
    b, tq_pad, _ = q.shape
    s_pad = k_all.shape[1]
    nblk = s_pad // tk
    npair = N_HEADS // HEADS_PER_PAIR
    qblk = lambda w: pl.BlockSpec((1, tq, w), lambda bi, qi_: (bi, qi_, 0))
    seq = lambda w: pl.BlockSpec((1, s_pad, w), lambda bi, qi_: (bi, 0, 0))
    return pl.pallas_call(
        functools.partial(_attn_kernel, tq=tq, tk=tk, nblk=nblk, pos0=pos0, s_valid=s_valid, kk=kk),
        out_shape=jax.ShapeDtypeStruct((b, tq_pad, D_ATTN), BF16),
        grid=(b, tq_pad // tq),
        in_specs=[qblk(D_ATTN), qblk(IDX_HEADS * IDX_DIM), qblk(LANES), seq(D_KV), seq(D_KV), seq(LANES)],
        out_specs=qblk(D_ATTN),
        scratch_shapes=[pltpu.VMEM((nblk, tq, tk), F32), pltpu.VMEM((npair, HEADS_PER_PAIR * tq, LANES), BF16),
                        pltpu.VMEM((IDX_HEADS * tq, LANES), BF16), pltpu.VMEM((IDX_HEADS, tq, LANES), F32),
                        pltpu.VMEM((npair, HEADS_PER_PAIR, tq, LANES), F32),
                        pltpu.VMEM((npair, HEADS_PER_PAIR, tq, LANES), F32),
                        pltpu.VMEM((npair, HEADS_PER_PAIR * tq, LANES), F32)],
        compiler_params=_cparams("parallel", "parallel"),
        name="sparse_attention",
    )(q, qi, wi, k_all, v_all, ki2)


POOL_HALO = 16


def _pool_kernel(u_ref, halo_ref, st_ref, w_ref, scale_ref, o_ref, ext_ref, *, tb, pos0):
    i = pl.program_id(1)
    ext_ref[0:POOL_HALO] = jnp.where(i == 0, st_ref[0], halo_ref[0])
    ext_ref[POOL_HALO:POOL_HALO + tb] = u_ref[0]
    pos = pos0 + i * tb + lax.broadcasted_iota(jnp.int32, (tb, 1), 0)
    for gi, w in enumerate(POOL_WINDOWS):
        sl = slice(gi * POOL_GROUP, (gi + 1) * POOL_GROUP)
        tot = ext_ref[POOL_HALO:POOL_HALO + tb, sl]
        for k in range(1, w):
            tot = tot + ext_ref[POOL_HALO - k:POOL_HALO - k + tb, sl]
        cnt = jnp.minimum(w, pos + 1).astype(F32)
        pooled = tot / cnt - u_ref[0, :, sl]
        mixed = jnp.dot(pooled.astype(BF16), w_ref[gi], preferred_element_type=F32) * scale_ref[:, sl]
        o_ref[0, :, sl] = mixed.astype(o_ref.dtype)


def _pool(u, state16, w, scale, *, pos0):
    b, t, c = u.shape
    tb = SEQ_BLOCK
    hb = tb // POOL_HALO
    return pl.pallas_call(
        functools.partial(_pool_kernel, tb=tb, pos0=pos0),
        name="pool_mix",
        out_shape=jax.ShapeDtypeStruct((b, t, c), BF16),
        grid=(b, t // tb),
        in_specs=[pl.BlockSpec((1, tb, c), lambda bi, i: (bi, i, 0)),
                  pl.BlockSpec((1, POOL_HALO, c), lambda bi, i: (bi, jnp.maximum(i * hb - 1, 0), 0)),
                  pl.BlockSpec((1, POOL_HALO, c), lambda bi, i: (bi, 0, 0)),
                  pl.BlockSpec((len(POOL_WINDOWS), POOL_GROUP, POOL_GROUP), lambda bi, i: (0, 0, 0)),
                  pl.BlockSpec((1, c), lambda bi, i: (0, 0))],
        out_specs=pl.BlockSpec((1, tb, c), lambda bi, i: (bi, i, 0)),
        scratch_shapes=[pltpu.VMEM((POOL_HALO + tb, c), F32)],
        compiler_params=_cparams("parallel", "parallel"),
    )(u, u, state16, w, scale)


CONV_HALO = SUBLANES
SSD_GH = SSD_HEADS // SSD_GROUPS
SSD_GW = SSD_GH * SSD_HEAD_DIM


def _ssd_kernel(z_ref, dt_ref, xbc_ref, halo_ref, st_ref, h0_ref, cw_ref, cb_ref, dtb_ref, alog_ref, dsk_ref,
                nw_ref, ex_ref, ext_t_ref, y_ref, hT_ref, ext_ref, h_ref, *, q, t_valid):
    c = pl.program_id(1)

    @pl.when(c == 0)
    def _():
        h_ref[...] = h0_ref[0]

    ext_ref[0:CONV_HALO] = jnp.where(c == 0, st_ref[0], halo_ref[0])
    ext_ref[CONV_HALO:CONV_HALO + q] = xbc_ref[0]
    xc = cb_ref[...]
    for j in range(SSD_CONV):
        off = CONV_HALO - (SSD_CONV - 1) + j
        xc = xc + ext_ref[off:off + q] * cw_ref[j:j + 1, :]
    xs = _silu(xc)
    x = xs[:, :D_SSD]

    lane = lax.broadcasted_iota(jnp.int32, (q, LANES), 1)
    row = lax.broadcasted_iota(jnp.int32, (q, LANES), 0)
    v = dt_ref[0] + dtb_ref[...]
    dt = jnp.maximum(v, 0.0) + jnp.log1p(jnp.exp(-jnp.abs(v)))
    dt = jnp.where(jnp.logical_and(lane < SSD_HEADS, c * q + row < t_valid), dt, 0.0)
    a = dt * (-jnp.exp(alog_ref[...]))
    tri = jnp.where(lane <= row, 1.0, 0.0).astype(BF16)
    causal = lane <= row
    cs = _dot3_left(tri, a)
    cs_t = cs.T
    ex = ex_ref[...]
    cs_last = cs[q - 1:q, :]
    dt_x = _dot3(dt, ex)
    ecs_x = jnp.exp(_dot3(cs, ex))
    dec_x = jnp.exp(_dot3(cs_last - cs, ex))
    xdt = x * dt_x
    xdec = (xdt * dec_x).astype(BF16)
    xdt16 = xdt.astype(BF16)
    rowdec = jnp.exp(_dot3_left(ext_t_ref[...], jnp.broadcast_to(cs_t[:, q - 1:q], (LANES, LANES))))
    lo_half = lane < SSD_HEAD_DIM

    for g in range(SSD_GROUPS):
        bg = xs[:, D_SSD + g * SSD_STATE:D_SSD + (g + 1) * SSD_STATE].astype(BF16)
        cg = xs[:, D_SSD + SSD_GROUPS * SSD_STATE + g * SSD_STATE:D_SSD + SSD_GROUPS * SSD_STATE + (g + 1) * SSD_STATE].astype(BF16)
        cb = lax.dot_general(cg, bg, NT, preferred_element_type=F32)
        hs = slice(g * SSD_GW, (g + 1) * SSD_GW)
        hprev = h_ref[hs, :]
        y_off = lax.dot_general(cg, hprev.astype(BF16), NT, preferred_element_type=F32) * ecs_x[:, hs]
        for jp in range(SSD_GH // 2):
            cols = slice(g * SSD_GW + jp * LANES, g * SSD_GW + (jp + 1) * LANES)
            xp = xdt16[:, cols]
            halves = []
            for e in range(2):
                hd = g * SSD_GH + 2 * jp + e
                seg = cs[:, hd:hd + 1] - cs_t[hd:hd + 1, :]
                lmat = jnp.where(causal, jnp.exp(jnp.where(causal, seg, 0.0)), 0.0)
                halves.append(jnp.dot((cb * lmat).astype(BF16), xp, preferred_element_type=F32))
            y_diag = jnp.where(lo_half, halves[0], halves[1])
            yv = y_diag + y_off[:, jp * LANES:(jp + 1) * LANES] + dsk_ref[:, cols] * x[:, cols]
            zv = z_ref[0, :, cols]
            ext_ref[0:q, cols] = yv * _silu(zv)
        st_new = lax.dot_general(xdec[:, hs], bg, TN, preferred_element_type=F32)
        h_ref[hs, :] = hprev * rowdec[hs, :] + st_new

    for g in range(SSD_GROUPS):
        hs = slice(g * SSD_GW, (g + 1) * SSD_GW)
        yg = ext_ref[0:q, hs]
        yn = yg * lax.rsqrt(jnp.mean(yg * yg, axis=-1, keepdims=True) + EPS) * nw_ref[:, hs]
        y_ref[0, :, hs] = yn.astype(y_ref.dtype)

    @pl.when(c == pl.num_programs(1) - 1)
    def _():
        hT_ref[0] = h_ref[...]


def _dot3_left(w, v):
    h1, h2, h3 = _split3(v)
    return (jnp.dot(w, h1, preferred_element_type=F32) + jnp.dot(w, h2, preferred_element_type=F32)
            + jnp.dot(w, h3, preferred_element_type=F32))


def _ssd(zdt, xbc, state8, h0, cw, cb, dtb, alog, dsk, nw, ex, ext_t, *, t_valid):
    b, t, _ = xbc.shape
    q = SEQ_BLOCK
    hb = q // CONV_HALO
    cst = lambda r, c_: pl.BlockSpec((r, c_), lambda bi, i: (0, 0))
    return pl.pallas_call(
        functools.partial(_ssd_kernel, q=q, t_valid=t_valid),
        name="ssd_scan",
        out_shape=[jax.ShapeDtypeStruct((b, t, D_SSD), BF16), jax.ShapeDtypeStruct((b, D_SSD, SSD_STATE), F32)],
        grid=(b, t // q),
        in_specs=[pl.BlockSpec((1, q, D_SSD), lambda bi, i: (bi, i, 0)),
                  pl.BlockSpec((1, q, LANES), lambda bi, i: (bi, i, D_SSD // LANES)),
                  pl.BlockSpec((1, q, D_SSD_CONV), lambda bi, i: (bi, i, 0)),
                  pl.BlockSpec((1, CONV_HALO, D_SSD_CONV), lambda bi, i: (bi, jnp.maximum(i * hb - 1, 0), 0)),
                  pl.BlockSpec((1, CONV_HALO, D_SSD_CONV), lambda bi, i: (bi, 0, 0)),
                  pl.BlockSpec((1, D_SSD, SSD_STATE), lambda bi, i: (bi, 0, 0)),
                  cst(SSD_CONV, D_SSD_CONV), cst(1, D_SSD_CONV), cst(1, LANES), cst(1, LANES), cst(1, D_SSD),
                  cst(1, D_SSD), cst(LANES, D_SSD), cst(D_SSD, LANES)],
        out_specs=[pl.BlockSpec((1, q, D_SSD), lambda bi, i: (bi, i, 0)),
                   pl.BlockSpec((1, D_SSD, SSD_STATE), lambda bi, i: (bi, 0, 0))],
        scratch_shapes=[pltpu.VMEM((CONV_HALO + q, D_SSD_CONV), F32), pltpu.VMEM((D_SSD, SSD_STATE), F32)],
        compiler_params=_cparams("parallel", "arbitrary"),
    )(zdt, zdt, xbc, xbc, state8, h0, cw, cb, dtb, alog, dsk, nw, ex, ext_t)


def _merge_kernel(x_ref, oa_ref, op_ref, os_ref, gt_ref, g1_ref, wa_ref, wp_ref, ws_ref, wo_ref, o_ref):
    def gate(bi):
        return jax.nn.sigmoid(gt_ref[:, bi * D_MODEL:(bi + 1) * D_MODEL])
    merged = (gate(0) * jnp.dot(oa_ref[...], wa_ref[...], preferred_element_type=F32)
              + gate(1) * jnp.dot(op_ref[...], wp_ref[...], preferred_element_type=F32)
              + gate(2) * jnp.dot(os_ref[...], ws_ref[...], preferred_element_type=F32))
    o_ref[...] = x_ref[...] + g1_ref[0] * jnp.dot(merged.astype(BF16), wo_ref[...], preferred_element_type=F32)


def _merge(x, oa, op, os_, gates, mv_g1, mv_idx, wa, wp, ws, wo, tm):
    n, d = x.shape
    rb = mv_g1.shape[1]
    rowb = lambda w: pl.BlockSpec((tm, w), lambda i: (i, 0))
    cst = lambda r, c: pl.BlockSpec((r, c), lambda i: (0, 0))
    return pl.pallas_call(
        _merge_kernel,
        name="merge_out",
        out_shape=jax.ShapeDtypeStruct((n, d), F32),
        grid=(n // tm,),
        in_specs=[rowb(d), rowb(D_ATTN), rowb(D_POOL), rowb(D_SSD), rowb(N_BRANCH * D_MODEL),
                  pl.BlockSpec((1, rb, d), lambda i: (mv_idx(i), 0, 0)),
                  cst(D_ATTN, d), cst(D_POOL, d), cst(D_SSD, d), cst(d, d)],
        out_specs=rowb(d),
        compiler_params=_cparams("parallel"),
    )(x, oa, op, os_, gates, mv_g1, wa, wp, ws, wo)


def _ffn_down_kernel(x_ref, ua_ref, ug_ref, ha_ref, hg_ref, sa_ref, sg_ref, g2_ref, cwa_ref, cwg_ref, cba_ref,
                     cbg_ref, wd_ref, o_ref, ea_ref, eg_ref, *, tb):
    i = pl.program_id(1)
    first = i == 0

    def conv(e_ref, u_ref, h_ref, s_ref, cw_ref, cb_ref):
        e_ref[0:CONV_HALO] = jnp.where(first, s_ref[0], h_ref[0])
        e_ref[CONV_HALO:CONV_HALO + tb] = u_ref[0]
        out = cb_ref[...]
        for j in range(FFN_CONV):
            off = CONV_HALO - (FFN_CONV - 1) + j
            out = out + e_ref[off:off + tb] * cw_ref[j:j + 1, :]
        return out

    a = conv(ea_ref, ua_ref, ha_ref, sa_ref, cwa_ref, cba_ref)
    g = conv(eg_ref, ug_ref, hg_ref, sg_ref, cwg_ref, cbg_ref)
    act = (_silu(g) * a).astype(BF16)
    o_ref[0] = x_ref[0] + g2_ref[0] * jnp.dot(act, wd_ref[...], preferred_element_type=F32)


def _ffn_down(x, ua, ug, sa, sg, g2, cwa, cwg, cba, cbg, wd):
    b, t, d = x.shape
    f = ua.shape[2]
    tb = SEQ_BLOCK
    hb = tb // CONV_HALO
    blk = lambda w: pl.BlockSpec((1, tb, w), lambda bi, i: (bi, i, 0))
    halo = pl.BlockSpec((1, CONV_HALO, f), lambda bi, i: (bi, jnp.maximum(i * hb - 1, 0), 0))
    st = pl.BlockSpec((1, CONV_HALO, f), lambda bi, i: (bi, 0, 0))
    cst = lambda r, c: pl.BlockSpec((r, c), lambda bi, i: (0, 0))
    return pl.pallas_call(
        functools.partial(_ffn_down_kernel, tb=tb),
        name="ffn_down",
        out_shape=jax.ShapeDtypeStruct((b, t, d), F32),
        grid=(b, t // tb),
        in_specs=[blk(d), blk(f), blk(f), halo, halo, st, st,
                  pl.BlockSpec((1, 1, d), lambda bi, i: (bi, 0, 0)),
                  cst(FFN_CONV, f), cst(FFN_CONV, f), cst(1, f), cst(1, f), cst(f, d)],
        out_specs=blk(d),
        scratch_shapes=[pltpu.VMEM((CONV_HALO + tb, f), F32), pltpu.VMEM((CONV_HALO + tb, f), F32)],
        compiler_params=_cparams("parallel", "parallel"),
    )(x, ua, ug, ua, ug, sa, sg, g2, cwa, cwg, cba, cbg, wd)


def _pad_rows(a, rows, front=False):
    extra = rows - a.shape[1]
    if extra == 0:
        return a
    return jnp.pad(a, ((0, 0), (extra, 0) if front else (0, extra), (0, 0)))


def _rope_tables(pos):
    half = HEAD_DIM // 2
    freqs = ROPE_THETA ** (-jnp.arange(half, dtype=jnp.float32) / half)
    ang = pos.astype(jnp.float32)[:, None] * freqs[None, :]
    cos, sin = jnp.cos(ang), jnp.sin(ang)
    return jnp.concatenate([cos, cos, cos, cos], axis=-1), jnp.concatenate([-sin, sin, -sin, sin], axis=-1)


def _layer_weights(l, w_ada, b_ada, norm1, norm2, w_in, q_norm, k_norm, pool_w, pool_scale, ssd_conv_w, ssd_conv_b,
                   ssd_dt_bias, ssd_a_log, ssd_d, ssd_norm, w_branch_attn, w_branch_pool, w_branch_ssd, w_out,
                   ffn_up, ffn_conv_w, ffn_conv_b, ffn_down):
    o = _OFFS
    win = w_in[l]
    col = lambda a, b: win[:, a:b]
    zpad = lambda n: jnp.zeros((D_MODEL, n), win.dtype)
    row = lambda v: v.reshape(1, -1).astype(F32)
    lanepad = lambda v: jnp.pad(v.astype(F32), (0, LANES - v.shape[0])).reshape(1, LANES)
    return dict(
        w_ada=w_ada[l].astype(BF16), b_ada=row(b_ada[l]), norm1=row(norm1[l]), norm2=row(norm2[l]),
        w_attn=jnp.concatenate([col(o[0], o[6]), zpad(LANES - IDX_DIM - IDX_HEADS)], axis=1).astype(BF16),
        w_pool=col(o[6], o[7]).astype(BF16),
        w_zdt=jnp.concatenate([col(o[7], o[8]), col(o[9], o[10]), zpad(LANES - SSD_HEADS)], axis=1).astype(BF16),
        w_xbc=col(o[8], o[9]).astype(BF16),
        w_gate=col(o[10], o[11]).astype(BF16),
        qn=row(jnp.tile(q_norm[l], LANES // HEAD_DIM)), kn=row(jnp.tile(k_norm[l], LANES // HEAD_DIM)),
        pool_w=pool_w[l].astype(BF16), pool_scale=row(pool_scale[l]),
        cw=ssd_conv_w[l].astype(F32), cb=row(ssd_conv_b[l]), dtb=lanepad(ssd_dt_bias[l]), alog=lanepad(ssd_a_log[l]),
        dsk=row(jnp.repeat(ssd_d[l], SSD_HEAD_DIM)), nw=row(ssd_norm[l]),
        wa=w_branch_attn[l].astype(BF16), wp=w_branch_pool[l].astype(BF16), ws=w_branch_ssd[l].astype(BF16),
        wo=w_out[l].astype(BF16),
        up_a=ffn_up[l][:, :D_FF].astype(BF16), up_g=ffn_up[l][:, D_FF:].astype(BF16),
        cwa=ffn_conv_w[l][:, :D_FF].astype(F32), cwg=ffn_conv_w[l][:, D_FF:].astype(F32),
        cba=row(ffn_conv_b[l][:D_FF]), cbg=row(ffn_conv_b[l][D_FF:]), wd=ffn_down[l].astype(BF16))


def _group_layer(x, mod, lw, consts, *, pos0, past, states):
    b, t, d = x.shape
    n = b * t
    tp = -(-t // SEQ_BLOCK) * SEQ_BLOCK
    per_seq = t % SEQ_BLOCK == 0
    tm = 256 if per_seq else n
    sh1, sc1, g1, sh2, sc2, g2 = [mod[:, i * d:(i + 1) * d] for i in range(6)]
    pos = pos0 + jnp.arange(t)
    cos_t, sin_t = _rope_tables(pos)
    if per_seq:
        view = lambda m: m.reshape(b, 1, d)
        mv_idx = lambda i: i // (t // tm)
        tab = lambda a: a.reshape(t // tm, tm, LANES)
        tab_idx = lambda i: i % (t // tm)
    else:
        view = lambda m: jnp.repeat(m, t, axis=0).reshape(1, n, d)
        mv_idx = lambda i: 0
        tab = lambda a: jnp.tile(a, (b, 1)).reshape(1, n, LANES)
        tab_idx = lambda i: 0
    xf = x.reshape(n, d)
    nm = lambda w: _norm_matmul(xf, view(sh1), view(sc1), mv_idx, lw['norm1'], w, tm)
    pa, pool_u, zdt, xbc, gates = nm(lw['w_attn']), nm(lw['w_pool']), nm(lw['w_zdt']), nm(lw['w_xbc']), nm(lw['w_gate'])

    q_r, k_r, qi_r, ki2, wi_s = _attn_prep(pa, tab(cos_t), tab(sin_t), tab_idx, lw['qn'], lw['kn'], consts['seg'], tm)
    v_new = pa[:, D_ATTN + D_KV:D_ATTN + 2 * D_KV]
    seq3 = lambda a: a.reshape(b, t, a.shape[-1])
    if past is None:
        tq, tk = ATTN_Q_ROWS, min(ATTN_KEYS, t)
        k_all, v_all, ki_all = seq3(k_r).astype(BF16), seq3(v_new).astype(BF16), seq3(ki2).astype(BF16)
        s_valid = t
        qpad = lambda a: seq3(a)
    else:
        tq = 2 * SUBLANES
        page_table, ck, cv, ci = past
        k_all, v_all, ki_all = _page_gather(page_table, ck, cv, ci, seq3(k_r), seq3(v_new), seq3(ki2[:, :IDX_DIM]))
        tk = min(ATTN_KEYS_DECODE, k_all.shape[1] // 2)
        s_valid = pos0 + t
        qpad = lambda a: _pad_rows(seq3(a), tq)
    assert k_all.shape[1] % tk == 0
    kk = min(INDEX_TOPK, s_valid // 4)
    o_attn = _attention(qpad(q_r), qpad(qi_r), qpad(wi_s), k_all, v_all, ki_all,
                        tq=tq, tk=tk, pos0=pos0, s_valid=s_valid, kk=kk)[:, :t].reshape(n, D_ATTN)

    st_pool, st_sconv, st_ssd, st_fconv = states
    seqp = lambda a: _pad_rows(seq3(a), tp)
    o_pool = _pool(seqp(pool_u), _pad_rows(st_pool, POOL_HALO, front=True), lw['pool_w'], lw['pool_scale'], pos0=pos0)
    o_ssd, ssd_state = _ssd(seqp(zdt), seqp(xbc), _pad_rows(st_sconv, CONV_HALO, front=True),
                            st_ssd.reshape(b, D_SSD, SSD_STATE), lw['cw'], lw['cb'], lw['dtb'], lw['alog'], lw['dsk'],
                            lw['nw'], consts['ex'], consts['ext_t'], t_valid=t)
    unpad = lambda a: a[:, :t].reshape(n, a.shape[-1])
    x1 = _merge(xf, o_attn, unpad(o_pool), unpad(o_ssd), gates, view(g1), mv_idx,
                lw['wa'], lw['wp'], lw['ws'], lw['wo'], tm)

    nm2 = lambda w: _norm_matmul(x1, view(sh2), view(sc2), mv_idx, lw['norm2'], w, tm)
    ua, ug = nm2(lw['up_a']), nm2(lw['up_g'])
    st_f = _pad_rows(st_fconv, CONV_HALO, front=True)
    x2 = _ffn_down(seqp(x1), seqp(ua), seqp(ug), st_f[:, :, :D_FF], st_f[:, :, D_FF:], g2.reshape(b, 1, d),
                   lw['cwa'], lw['cwg'], lw['cba'], lw['cbg'], lw['wd'])[:, :t]

    tail = lambda prev, new, keep: jnp.concatenate([prev, seq3(new)], axis=1)[:, -keep:]
    new_states = (seq3(k_r).reshape(b, t, N_KV_HEADS, HEAD_DIM), seq3(v_new).reshape(b, t, N_KV_HEADS, HEAD_DIM),
                  seq3(ki2)[:, :, :IDX_DIM], tail(st_pool, pool_u, POOL_CTX), tail(st_sconv, xbc, SSD_CONV - 1),
                  ssd_state.reshape(b, SSD_HEADS, SSD_HEAD_DIM, SSD_STATE),
                  tail(st_fconv, jnp.concatenate([ua, ug], axis=-1), FFN_CONV - 1))
    return x2, new_states


def kernel(x_prompt, x_sample, c_prompt, c_sample, cache_k, cache_v, cache_kidx, page_table, state_pool, state_ssd_conv, state_ssd, state_ffn_conv, w_ada, b_ada, norm1, norm2, w_in, q_norm, k_norm, pool_w, pool_scale, ssd_conv_w, ssd_conv_b, ssd_dt_bias, ssd_a_log, ssd_d, ssd_norm, w_branch_attn, w_branch_pool, w_branch_ssd, w_out, ffn_up, ffn_conv_w, ffn_conv_b, ffn_down):
    bp, tp, d = x_prompt.shape
    bs, ts, _ = x_sample.shape
    depth = w_ada.shape[0]
    n_pool, page = cache_k.shape[1], cache_k.shape[2]
    past_len = page_table.shape[1] * page
    dt = x_prompt.dtype

    r = lax.broadcasted_iota(jnp.int32, (LANES, LANES), 0)
    c = lax.broadcasted_iota(jnp.int32, (LANES, LANES), 1)
    hr = lax.broadcasted_iota(jnp.int32, (LANES, D_SSD), 0)
    hc = lax.broadcasted_iota(jnp.int32, (LANES, D_SSD), 1)
    ex = (hr == hc // SSD_HEAD_DIM).astype(BF16)
    consts = dict(seg=(r // HEAD_DIM == c // HEAD_DIM).astype(BF16), ex=ex, ext_t=ex.T)

    c_all = jnp.concatenate([c_prompt, c_sample], axis=0)
    rows = -(-c_all.shape[0] // SUBLANES) * SUBLANES
    c_pad = jnp.pad(c_all, ((0, rows - c_all.shape[0]), (0, 0)))

    zeros = lambda *s: jnp.zeros(s, dt)
    xp, xs = x_prompt, x_sample
    new_p, new_s = [[] for _ in range(7)], [[] for _ in range(7)]
    for l in range(depth):
        lw = _layer_weights(l, w_ada, b_ada, norm1, norm2, w_in, q_norm, k_norm, pool_w, pool_scale, ssd_conv_w,
                            ssd_conv_b, ssd_dt_bias, ssd_a_log, ssd_d, ssd_norm, w_branch_attn, w_branch_pool,
                            w_branch_ssd, w_out, ffn_up, ffn_conv_w, ffn_conv_b, ffn_down)
        mod = _mod(c_pad, lw['w_ada'], lw['b_ada'])
        xp, st_p = _group_layer(
            xp, mod[:bp], lw, consts, pos0=0, past=None,
            states=(zeros(bp, POOL_CTX, D_POOL), zeros(bp, SSD_CONV - 1, D_SSD_CONV),
                    zeros(bp, SSD_HEADS, SSD_HEAD_DIM, SSD_STATE), zeros(bp, FFN_CONV - 1, 2 * D_FF)))
        past = (page_table, cache_k[l].reshape(n_pool, page, D_KV), cache_v[l].reshape(n_pool, page, D_KV), cache_kidx[l])
        xs, st_s = _group_layer(xs, mod[bp:bp + bs], lw, consts, pos0=past_len, past=past,
                                states=(state_pool[l], state_ssd_conv[l], state_ssd[l], state_ffn_conv[l]))
        for j in range(7):
            new_p[j].append(st_p[j])
            new_s[j].append(st_s[j])
    outs_p = [jnp.stack(s) for s in new_p]
    outs_s = [jnp.stack(s) for s in new_s]
    return (xp, xs, *outs_p, *outs_s)
```

```python
import functools

import numpy as np
import jax
import jax.numpy as jnp
from jax import lax
from jax.experimental import pallas as pl
from jax.experimental.pallas import tpu as pltpu

D_MODEL = 1024
N_HEADS = 16
N_KV_HEADS = 4
HEAD_DIM = 64
D_ATTN = N_HEADS * HEAD_DIM
D_KV = N_KV_HEADS * HEAD_DIM
IDX_HEADS = 8
IDX_DIM = 64
INDEX_TOPK = 256
ROPE_THETA = 10000.0
POOL_WINDOWS = (2, 4, 8, 16)
D_POOL = D_MODEL
POOL_GROUP = D_POOL // len(POOL_WINDOWS)
POOL_CTX = max(POOL_WINDOWS) - 1
D_SSD = 2 * D_MODEL
SSD_HEAD_DIM = 64
SSD_HEADS = D_SSD // SSD_HEAD_DIM
SSD_GROUPS = 4
SSD_STATE = 128
SSD_CONV = 4
D_SSD_CONV = D_SSD + 2 * SSD_GROUPS * SSD_STATE
D_FF = 2816
FFN_CONV = 3
N_BRANCH = 3
EPS = 1e-6
_SPLITS = (D_ATTN, D_KV, D_KV, IDX_HEADS * IDX_DIM, IDX_DIM, IDX_HEADS, D_POOL, D_SSD, D_SSD_CONV,
           SSD_HEADS, N_BRANCH * D_MODEL)
_OFFS = tuple(int(v) for v in np.cumsum((0,) + _SPLITS))

F32 = jnp.float32
BF16 = jnp.bfloat16
LANES = 128
SUBLANES = 8
SEQ_BLOCK = 128
ATTN_Q_ROWS = 256
ATTN_KEYS = 256
ATTN_KEYS_DECODE = 1024
VMEM_LIMIT = 56 * 1024 * 1024
NEG = -1e30
LOG2E = 1.4426950408889634
N_BISECT = 20
NT = (((1,), (1,)), ((), ()))
TN = (((0,), (0,)), ((), ()))


def _cparams(*sem):
    return pltpu.CompilerParams(dimension_semantics=sem, vmem_limit_bytes=VMEM_LIMIT)


def _silu(v):
    return v * jax.nn.sigmoid(v)


def _split3(v):
    h1 = v.astype(BF16)
    r1 = v - h1.astype(F32)
    h2 = r1.astype(BF16)
    h3 = (r1 - h2.astype(F32)).astype(BF16)
    return h1, h2, h3


def _dot3(v, w):
    h1, h2, h3 = _split3(v)
    return (jnp.dot(h1, w, preferred_element_type=F32) + jnp.dot(h2, w, preferred_element_type=F32)
            + jnp.dot(h3, w, preferred_element_type=F32))


def _mod_kernel(c_ref, w_ref, b_ref, o_ref):
    s = _silu(c_ref[...])
    o_ref[...] = jnp.dot(s.astype(BF16), w_ref[...], preferred_element_type=F32) + b_ref[...]


def _mod(c_pad, w, b):
    rows, d = c_pad.shape
    n = w.shape[1]
    tn = n // 4
    return pl.pallas_call(
        _mod_kernel,
        name="adaln_mod",
        out_shape=jax.ShapeDtypeStruct((rows, n), F32),
        grid=(n // tn,),
        in_specs=[pl.BlockSpec((rows, d), lambda j: (0, 0)),
                  pl.BlockSpec((d, tn), lambda j: (0, j)),
                  pl.BlockSpec((1, tn), lambda j: (0, j))],
        out_specs=pl.BlockSpec((rows, tn), lambda j: (0, j)),
        compiler_params=_cparams("parallel"),
    )(c_pad, w, b)


def _norm_matmul_kernel(x_ref, sh_ref, sc_ref, g_ref, w_ref, o_ref):
    x = x_ref[...]
    y = x * lax.rsqrt(jnp.mean(x * x, axis=-1, keepdims=True) + EPS) * g_ref[...]
    h = y * (1.0 + sc_ref[0]) + sh_ref[0]
    o_ref[...] = jnp.dot(h.astype(BF16), w_ref[...], preferred_element_type=F32).astype(o_ref.dtype)


def _norm_matmul(x, mv_sh, mv_sc, mv_idx, g, w, tm):
    n, d = x.shape
    nout = w.shape[1]
    rb = mv_sh.shape[1]
    return pl.pallas_call(
        _norm_matmul_kernel,
        name="norm_matmul",
        out_shape=jax.ShapeDtypeStruct((n, nout), F32),
        grid=(n // tm,),
        in_specs=[pl.BlockSpec((tm, d), lambda i: (i, 0)),
                  pl.BlockSpec((1, rb, d), lambda i: (mv_idx(i), 0, 0)),
                  pl.BlockSpec((1, rb, d), lambda i: (mv_idx(i), 0, 0)),
                  pl.BlockSpec((1, d), lambda i: (0, 0)),
                  pl.BlockSpec((d, nout), lambda i: (0, 0))],
        out_specs=pl.BlockSpec((tm, nout), lambda i: (i, 0)),
        compiler_params=_cparams("parallel"),
    )(x, mv_sh, mv_sc, g, w)


def _attn_prep_kernel(q_ref, k_ref, qi_ref, kw_ref, cos_ref, sin_ref, qn_ref, kn_ref, seg_ref,
                      qo_ref, ko_ref, qio_ref, kio_ref, wio_ref):
    tm = q_ref.shape[0]
    lane = lax.broadcasted_iota(jnp.int32, (tm, LANES), 1)
    first_half = (lane % HEAD_DIM) < (HEAD_DIM // 2)
    cos = cos_ref[0]
    sin = sin_ref[0]
    seg = seg_ref[...]

    def rope(v):
        up = pltpu.roll(v, LANES - HEAD_DIM // 2, 1)
        dn = pltpu.roll(v, HEAD_DIM // 2, 1)
        return v * cos + jnp.where(first_half, up, dn) * sin

    def headnorm(v, gain):
        sq = v * v
        hi = sq.astype(BF16)
        lo = (sq - hi.astype(F32)).astype(BF16)
        ss = jnp.dot(hi, seg, preferred_element_type=F32) + jnp.dot(lo, seg, preferred_element_type=F32)
        return v * lax.rsqrt(ss * (1.0 / HEAD_DIM) + EPS) * gain

    qn = qn_ref[...]
    kn = kn_ref[...]
    for j in range(D_ATTN // LANES):
        sl = slice(j * LANES, (j + 1) * LANES)
        qo_ref[:, sl] = rope(headnorm(q_ref[:, sl], qn)) * (HEAD_DIM ** -0.5 * LOG2E)
    for j in range(D_KV // LANES):
        sl = slice(j * LANES, (j + 1) * LANES)
        ko_ref[:, sl] = rope(headnorm(k_ref[:, sl], kn))
    for j in range(IDX_HEADS * IDX_DIM // LANES):
        sl = slice(j * LANES, (j + 1) * LANES)
        qio_ref[:, sl] = rope(qi_ref[:, sl])
    kw = kw_ref[...]
    kr = rope(kw)
    kio_ref[...] = jnp.where(lane < IDX_DIM, kr, pltpu.roll(kr, IDX_DIM, 1))
    wio_ref[...] = pltpu.roll(kw, LANES - IDX_DIM, 1) * (IDX_HEADS ** -0.5 * IDX_DIM ** -0.5)


def _attn_prep(pa, cos_t, sin_t, tab_idx, qn, kn, seg, tm):
    n = pa.shape[0]
    blk = lambda w, c: pl.BlockSpec((tm, w), lambda i: (i, c))
    tab = pl.BlockSpec((1, tm, LANES), lambda i: (tab_idx(i), 0, 0))
    cst = lambda r, c: pl.BlockSpec((r, c), lambda i: (0, 0))
    return pl.pallas_call(
        _attn_prep_kernel,
        name="attn_prep",
        out_shape=[jax.ShapeDtypeStruct((n, D_ATTN), F32), jax.ShapeDtypeStruct((n, D_KV), F32),
                   jax.ShapeDtypeStruct((n, IDX_HEADS * IDX_DIM), F32),
                   jax.ShapeDtypeStruct((n, LANES), F32), jax.ShapeDtypeStruct((n, LANES), F32)],
        grid=(n // tm,),
        in_specs=[blk(D_ATTN, 0), blk(D_KV, D_ATTN // D_KV), blk(IDX_HEADS * IDX_DIM, 3), blk(LANES, 16),
                  tab, tab, cst(1, LANES), cst(1, LANES), cst(LANES, LANES)],
        out_specs=[blk(D_ATTN, 0), blk(D_KV, 0), blk(IDX_HEADS * IDX_DIM, 0), blk(LANES, 0), blk(LANES, 0)],
        compiler_params=_cparams("parallel"),
    )(pa, pa, pa, pa, cos_t, sin_t, qn, kn, seg)


GATHER_PAGES = 8


def _page_gather_kernel(pt_ref, *refs, n_steps, pg, page):
    ins, outs = refs[:3 * pg + 3], refs[3 * pg + 3:]
    past = pl.program_id(1) < n_steps
    for t in range(3):
        new_ref, out_ref = ins[3 * pg + t], outs[t]
        for i in range(pg):
            rows = slice(i * page, (i + 1) * page)
            tile = ins[t * pg + i][0]
            if t == 2:
                tile = jnp.concatenate([tile, tile], axis=0)
            out_ref[0, rows, :] = jnp.where(past, tile.T, new_ref[0, rows, :]).astype(out_ref.dtype)


def _page_gather(page_table, pool_off, ck, cv, ci, nk, nv, ni):
    bd, n_pages = page_table.shape
    page = ck.shape[2]
    pg = max(g for g in range(1, GATHER_PAGES + 1) if n_pages % g == 0)
    n_steps = n_pages // pg
    nk, nv, ni = [_pad_rows(a, pg * page) for a in (nk, nv, ni)]

    def cache(w, i):
        return pl.BlockSpec((1, w, page),
                            lambda b, p, pt: (pool_off + pt[b, jnp.minimum(p, n_steps - 1) * pg + i], 0, 0))

    new = lambda w: pl.BlockSpec((1, pg * page, w), lambda b, p, pt: (b, 0, 0))
    out = lambda w: pl.BlockSpec((1, pg * page, w), lambda b, p, pt: (b, p, 0))
    s_pad = (n_steps + 1) * pg * page
    in_specs = ([cache(D_KV, i) for i in range(pg)] + [cache(D_KV, i) for i in range(pg)]
                + [cache(IDX_DIM, i) for i in range(pg)] + [new(D_KV), new(D_KV), new(2 * IDX_DIM)])
    return pl.pallas_call(
        functools.partial(_page_gather_kernel, n_steps=n_steps, pg=pg, page=page),
        out_shape=[jax.ShapeDtypeStruct((bd, s_pad, D_KV), BF16), jax.ShapeDtypeStruct((bd, s_pad, D_KV), BF16),
                   jax.ShapeDtypeStruct((bd, s_pad, 2 * IDX_DIM), BF16)],
        grid_spec=pltpu.PrefetchScalarGridSpec(
            num_scalar_prefetch=1, grid=(bd, n_steps + 1), in_specs=in_specs,
            out_specs=[out(D_KV), out(D_KV), out(2 * IDX_DIM)]),
        compiler_params=_cparams("parallel", "parallel"),
        name="page_gather",
    )(page_table, *([ck] * pg), *([cv] * pg), *([ci] * pg), nk, nv, ni)


HEADS_PER_PAIR = 2 * (N_HEADS // N_KV_HEADS)


def _attn_kernel(q_ref, qi_ref, wi_ref, k_ref, v_ref, ki_ref, o_ref,
                 sc_ref, qh_ref, qim_ref, wb_ref, m_ref, l_ref, acc_ref, *, tq, tk, nblk, pos0, s_valid, kk):
    q0 = pl.program_id(1) * tq
    nch = tk // LANES
    lane = lax.broadcasted_iota(jnp.int32, (tq, LANES), 1)
    row = lax.broadcasted_iota(jnp.int32, (tq, LANES), 0)
    qpos = pos0 + q0 + row
    lo_half = lane < HEAD_DIM
    n_kv = jnp.minimum(nblk, (pos0 + q0 + tq - 1) // tk + 1)
    kkf = float(kk)
    chunk = lambda a, j: a[..., j * LANES:(j + 1) * LANES]

    def bcast(col):
        return jnp.broadcast_to(col, col.shape[:-1] + (LANES,))

    wi = wi_ref[0]
    for j in range(IDX_HEADS // 2):
        t = qi_ref[0, :, j * LANES:(j + 1) * LANES]
        qim_ref[2 * j * tq:(2 * j + 1) * tq] = jnp.where(lo_half, t, 0.0).astype(BF16)
        qim_ref[(2 * j + 1) * tq:(2 * j + 2) * tq] = jnp.where(lo_half, 0.0, t).astype(BF16)
    for h in range(IDX_HEADS):
        wb_ref[h] = bcast(wi[:, h:h + 1])
    for h in range(N_HEADS):
        t = q_ref[0, :, (h // 2) * LANES:(h // 2 + 1) * LANES]
        kv_half = (h // (N_HEADS // N_KV_HEADS)) % 2
        if h % 2 != kv_half:
            t = pltpu.roll(t, HEAD_DIM, 1)
        keep = lo_half if kv_half == 0 else jnp.logical_not(lo_half)
        hh = h % HEADS_PER_PAIR
        qh_ref[h // HEADS_PER_PAIR, hh * tq:(hh + 1) * tq] = jnp.where(keep, t, 0.0).astype(BF16)
    m_ref[...] = jnp.full(m_ref.shape, NEG, F32)
    l_ref[...] = jnp.zeros(l_ref.shape, F32)
    acc_ref[...] = jnp.zeros(acc_ref.shape, F32)

    def p1(c, carry):
        kid = ki_ref[0, pl.ds(pl.multiple_of(c * tk, tk), tk), :]
        y = lax.dot_general(qim_ref[...], kid, NT, preferred_element_type=F32)
        y = jnp.maximum(y, 0.0).reshape(IDX_HEADS, tq, tk)
        for j in range(nch):
            acc = wb_ref[0] * chunk(y[0], j)
            for h in range(1, IDX_HEADS):
                acc = acc + wb_ref[h] * chunk(y[h], j)
            kpos = c * tk + j * LANES + lane
            vis = jnp.logical_and(kpos <= qpos, kpos < s_valid)
            sc_ref[c, :, j * LANES:(j + 1) * LANES] = jnp.where(vis, acc, -jnp.inf)
        return carry

    lax.fori_loop(0, n_kv, p1, 0)

    def count_gt(t_b):
        def body(c, acc):
            x = sc_ref[c]
            for j in range(nch):
                acc = acc + jnp.where(chunk(x, j) > t_b, 1.0, 0.0)
            return acc
        acc = lax.fori_loop(0, n_kv, body, jnp.zeros((tq, LANES), F32))
        return bcast(jnp.sum(acc, axis=1, keepdims=True))

    def next_above(t_b):
        def body(c, acc):
            x = sc_ref[c]
            for j in range(nch):
                xj = chunk(x, j)
                acc = jnp.minimum(acc, jnp.where(xj > t_b, xj, jnp.inf))
            return acc
        acc = lax.fori_loop(0, n_kv, body, jnp.full((tq, LANES), jnp.inf, F32))
        return bcast(jnp.min(acc, axis=1, keepdims=True))

    def stat(c, carry):
        mn, mx = carry
        x = sc_ref[c]
        for j in range(nch):
            xj = chunk(x, j)
            mn = jnp.minimum(mn, jnp.where(xj == -jnp.inf, jnp.inf, xj))
            mx = jnp.maximum(mx, xj)
        return mn, mx

    mn, mx = lax.fori_loop(0, n_kv, stat, (jnp.full((tq, LANES), jnp.inf, F32), jnp.full((tq, LANES), -jnp.inf, F32)))
    rmin = bcast(jnp.min(mn, axis=1, keepdims=True))
    rmax = bcast(jnp.max(mx, axis=1, keepdims=True))
    c_min = count_gt(rmin)
    at_min = c_min < kkf

    def bisect(i, carry):
        lo, hi = carry
        mid = lo + 0.5 * (hi - lo)
        up = count_gt(mid) >= kkf
        return jnp.where(up, mid, lo), jnp.where(up, hi, mid)

    lo, hi = lax.fori_loop(0, N_BISECT, bisect, (rmin, rmax))
    tau = next_above(lo)
    ct = count_gt(tau)

    def pending(ct_):
        return jnp.logical_and(ct_ >= kkf, jnp.logical_not(at_min))

    def w_cond(st):
        return jnp.max(jnp.where(pending(st[2]), 1.0, 0.0)) > 0.5

    def w_body(st):
        lo_, tau_, ct_ = st
        need = pending(ct_)
        lo2 = jnp.where(need, tau_, lo_)
        tau2 = next_above(lo2)
        ct2 = count_gt(tau2)
        return lo2, jnp.where(need, tau2, tau_), jnp.where(need, ct2, ct_)

    lo, tau, ct = lax.while_loop(w_cond, w_body, (lo, tau, ct))
    tau = jnp.where(at_min, rmin, tau)
    m_need = kkf - jnp.where(at_min, c_min, ct)

    rk = lax.broadcasted_iota(jnp.int32, (tk, tk), 0)
    ck = lax.broadcasted_iota(jnp.int32, (tk, tk), 1)
    before = jnp.where(rk < ck, 1.0, 0.0).astype(BF16)

    def p2(c, cum):
        x = sc_ref[c]
        eqs = [chunk(x, j) == tau for j in range(nch)]
        eqf = [jnp.where(e, 1.0, 0.0) for e in eqs]
        pre = jnp.dot(jnp.concatenate(eqf, axis=-1).astype(BF16), before, preferred_element_type=F32)
        bias = [jnp.where(chunk(x, j) > tau, 0.0,
                          jnp.where(eqs[j], jnp.where(chunk(pre, j) + cum < m_need, 0.0, NEG), NEG))
                for j in range(nch)]
        tot = eqf[0]
        for j in range(1, nch):
            tot = tot + eqf[j]
        cum = cum + bcast(jnp.sum(tot, axis=1, keepdims=True))
        koff = pl.multiple_of(c * tk, tk)
        for p in range(N_HEADS // HEADS_PER_PAIR):
            kp = k_ref[0, pl.ds(koff, tk), p * LANES:(p + 1) * LANES]
            vp = v_ref[0, pl.ds(koff, tk), p * LANES:(p + 1) * LANES]
            s = lax.dot_general(qh_ref[p], kp, NT, preferred_element_type=F32).reshape(HEADS_PER_PAIR, tq, tk)
            sm = [chunk(s, j) + bias[j][None] for j in range(nch)]
            mx_ = sm[0]
            for j in range(1, nch):
                mx_ = jnp.maximum(mx_, sm[j])
            m_old = m_ref[p]
            m_new = jnp.maximum(m_old, bcast(jnp.max(mx_, axis=-1, keepdims=True)))
            pe = [jnp.exp2(sm[j] - m_new) for j in range(nch)]
            ps = pe[0]
            for j in range(1, nch):
                ps = ps + pe[j]
            alpha = jnp.exp2(m_old - m_new)
            l_ref[p] = alpha * l_ref[p] + bcast(jnp.sum(ps, axis=-1, keepdims=True))
            pv = jnp.dot(jnp.concatenate(pe, axis=-1).reshape(HEADS_PER_PAIR * tq, tk).astype(BF16), vp,
                         preferred_element_type=F32)
            acc_ref[p] = alpha.reshape(HEADS_PER_PAIR * tq, LANES) * acc_ref[p] + pv
            m_ref[p] = m_new
        return cum

    lax.fori_loop(0, n_kv, p2, jnp.zeros((tq, LANES), F32))

    def head_out(h):
        p, hh = h // HEADS_PER_PAIR, h % HEADS_PER_PAIR
        r = acc_ref[p, hh * tq:(hh + 1) * tq] / l_ref[p, hh]
        if (h // (N_HEADS // N_KV_HEADS)) % 2 != h % 2:
            r = pltpu.roll(r, HEAD_DIM, 1)
        return r

    for j in range(N_HEADS // 2):
        o_ref[0, :, j * LANES:(j + 1) * LANES] = jnp.where(lo_half, head_out(2 * j), head_out(2 * j + 1)).astype(o_ref.dtype)


def _attention(q, qi, wi, k_all, v_all, ki2, *, tq, tk, pos0, s_valid, kk):
    b, tq_pad, _ = q.shape
    s_pad = k_all.shape[1]
    nblk = s_pad // tk
    npair = N_HEADS // HEADS_PER_PAIR
    qblk = lambda w: pl.BlockSpec((1, tq, w), lambda bi, qi_: (bi, qi_, 0))
    seq = lambda w: pl.BlockSpec((1, s_pad, w), lambda bi, qi_: (bi, 0, 0))
    return pl.pallas_call(
        functools.partial(_attn_kernel, tq=tq, tk=tk, nblk=nblk, pos0=pos0, s_valid=s_valid, kk=kk),
        out_shape=jax.ShapeDtypeStruct((b, tq_pad, D_ATTN), BF16),
        grid=(b, tq_pad // tq),
        in_specs=[qblk(D_ATTN), qblk(IDX_HEADS * IDX_DIM), qblk(LANES), seq(D_KV), seq(D_KV), seq(LANES)],
        out_specs=qblk(D_ATTN),
        scratch_shapes=[pltpu.VMEM((nblk, tq, tk), F32), pltpu.VMEM((npair, HEADS_PER_PAIR * tq, LANES), BF16),
                        pltpu.VMEM((IDX_HEADS * tq, LANES), BF16), pltpu.VMEM((IDX_HEADS, tq, LANES), F32),
                        pltpu.VMEM((npair, HEADS_PER_PAIR, tq, LANES), F32),
                        pltpu.VMEM((npair, HEADS_PER_PAIR, tq, LANES), F32),
                        pltpu.VMEM((npair, HEADS_PER_PAIR * tq, LANES), F32)],
        compiler_params=_cparams("parallel", "parallel"),
        name="sparse_attention",
    )(q, qi, wi, k_all, v_all, ki2)


POOL_HALO = 16


def _pool_kernel(u_ref, halo_ref, st_ref, w_ref, scale_ref, o_ref, ext_ref, *, tb, pos0):
    i = pl.program_id(1)
    ext_ref[0:POOL_HALO] = jnp.where(i == 0, st_ref[0], halo_ref[0])
    ext_ref[POOL_HALO:POOL_HALO + tb] = u_ref[0]
    pos = pos0 + i * tb + lax.broadcasted_iota(jnp.int32, (tb, 1), 0)
    for gi, w in enumerate(POOL_WINDOWS):
        sl = slice(gi * POOL_GROUP, (gi + 1) * POOL_GROUP)
        tot = ext_ref[POOL_HALO:POOL_HALO + tb, sl]
        for k in range(1, w):
            tot = tot + ext_ref[POOL_HALO - k:POOL_HALO - k + tb, sl]
        cnt = jnp.minimum(w, pos + 1).astype(F32)
        pooled = tot / cnt - u_ref[0, :, sl]
        mixed = jnp.dot(pooled.astype(BF16), w_ref[gi], preferred_element_type=F32) * scale_ref[:, sl]
        o_ref[0, :, sl] = mixed.astype(o_ref.dtype)


def _pool(u, state16, w, scale, *, pos0):
    b, t, c = u.shape
    tb = SEQ_BLOCK
    hb = tb // POOL_HALO
    return pl.pallas_call(
        functools.partial(_pool_kernel, tb=tb, pos0=pos0),
        name="pool_mix",
        out_shape=jax.ShapeDtypeStruct((b, t, c), BF16),
        grid=(b, t // tb),
        in_specs=[pl.BlockSpec((1, tb, c), lambda bi, i: (bi, i, 0)),
                  pl.BlockSpec((1, POOL_HALO, c), lambda bi, i: (bi, jnp.maximum(i * hb - 1, 0), 0)),
                  pl.BlockSpec((1, POOL_HALO, c), lambda bi, i: (bi, 0, 0)),
                  pl.BlockSpec((len(POOL_WINDOWS), POOL_GROUP, POOL_GROUP), lambda bi, i: (0, 0, 0)),
                  pl.BlockSpec((1, c), lambda bi, i: (0, 0))],
        out_specs=pl.BlockSpec((1, tb, c), lambda bi, i: (bi, i, 0)),
        scratch_shapes=[pltpu.VMEM((POOL_HALO + tb, c), F32)],
        compiler_params=_cparams("parallel", "parallel"),
    )(u, u, state16, w, scale)


CONV_HALO = SUBLANES
SSD_GH = SSD_HEADS // SSD_GROUPS
SSD_GW = SSD_GH * SSD_HEAD_DIM


def _ssd_kernel(z_ref, dt_ref, xbc_ref, halo_ref, st_ref, h0_ref, cw_ref, cb_ref, dtb_ref, alog_ref, dsk_ref,
                nw_ref, ex_ref, ext_t_ref, y_ref, hT_ref, ext_ref, h_ref, *, q, t_valid):
    c = pl.program_id(1)

    @pl.when(c == 0)
    def _():
        h_ref[...] = h0_ref[0]

    ext_ref[0:CONV_HALO] = jnp.where(c == 0, st_ref[0], halo_ref[0])
    ext_ref[CONV_HALO:CONV_HALO + q] = xbc_ref[0]
    xc = cb_ref[...]
    for j in range(SSD_CONV):
        off = CONV_HALO - (SSD_CONV - 1) + j
        xc = xc + ext_ref[off:off + q] * cw_ref[j:j + 1, :]
    xs = _silu(xc)
    x = xs[:, :D_SSD]

    lane = lax.broadcasted_iota(jnp.int32, (q, LANES), 1)
    row = lax.broadcasted_iota(jnp.int32, (q, LANES), 0)
    v = dt_ref[0] + dtb_ref[...]
    dt = jnp.maximum(v, 0.0) + jnp.log1p(jnp.exp(-jnp.abs(v)))
    dt = jnp.where(jnp.logical_and(lane < SSD_HEADS, c * q + row < t_valid), dt, 0.0)
    a = dt * (-jnp.exp(alog_ref[...]))
    tri = jnp.where(lane <= row, 1.0, 0.0).astype(BF16)
    causal = lane <= row
    cs = _dot3_left(tri, a)
    cs_t = cs.T
    ex = ex_ref[...]
    cs_last = cs[q - 1:q, :]
    dt_x = _dot3(dt, ex)
    ecs_x = jnp.exp(_dot3(cs, ex))
    dec_x = jnp.exp(_dot3(cs_last - cs, ex))
    xdt = x * dt_x
    xdec = (xdt * dec_x).astype(BF16)
    xdt16 = xdt.astype(BF16)
    rowdec = jnp.exp(_dot3_left(ext_t_ref[...], jnp.broadcast_to(cs_t[:, q - 1:q], (LANES, LANES))))
    lo_half = lane < SSD_HEAD_DIM

    for g in range(SSD_GROUPS):
        bg = xs[:, D_SSD + g * SSD_STATE:D_SSD + (g + 1) * SSD_STATE].astype(BF16)
        cg = xs[:, D_SSD + SSD_GROUPS * SSD_STATE + g * SSD_STATE:D_SSD + SSD_GROUPS * SSD_STATE + (g + 1) * SSD_STATE].astype(BF16)
        cb = lax.dot_general(cg, bg, NT, preferred_element_type=F32)
        hs = slice(g * SSD_GW, (g + 1) * SSD_GW)
        hprev = h_ref[hs, :]
        y_off = lax.dot_general(cg, hprev.astype(BF16), NT, preferred_element_type=F32) * ecs_x[:, hs]
        for jp in range(SSD_GH // 2):
            cols = slice(g * SSD_GW + jp * LANES, g * SSD_GW + (jp + 1) * LANES)
            xp = xdt16[:, cols]
            halves = []
            for e in range(2):
                hd = g * SSD_GH + 2 * jp + e
                seg = cs[:, hd:hd + 1] - cs_t[hd:hd + 1, :]
                lmat = jnp.where(causal, jnp.exp(jnp.where(causal, seg, 0.0)), 0.0)
                halves.append(jnp.dot((cb * lmat).astype(BF16), xp, preferred_element_type=F32))
            y_diag = jnp.where(lo_half, halves[0], halves[1])
            yv = y_diag + y_off[:, jp * LANES:(jp + 1) * LANES] + dsk_ref[:, cols] * x[:, cols]
            zv = z_ref[0, :, cols]
            ext_ref[0:q, cols] = yv * _silu(zv)
        st_new = lax.dot_general(xdec[:, hs], bg, TN, preferred_element_type=F32)
        h_ref[hs, :] = hprev * rowdec[hs, :] + st_new

    for g in range(SSD_GROUPS):
        hs = slice(g * SSD_GW, (g + 1) * SSD_GW)
        yg = ext_ref[0:q, hs]
        yn = yg * lax.rsqrt(jnp.mean(yg * yg, axis=-1, keepdims=True) + EPS) * nw_ref[:, hs]
        y_ref[0, :, hs] = yn.astype(y_ref.dtype)

    @pl.when(c == pl.num_programs(1) - 1)
    def _():
        hT_ref[0] = h_ref[...]


def _dot3_left(w, v):
    h1, h2, h3 = _split3(v)
    return (jnp.dot(w, h1, preferred_element_type=F32) + jnp.dot(w, h2, preferred_element_type=F32)
            + jnp.dot(w, h3, preferred_element_type=F32))


def _ssd(zdt, xbc, state8, h0, cw, cb, dtb, alog, dsk, nw, ex, ext_t, *, t_valid):
    b, t, _ = xbc.shape
    q = SEQ_BLOCK
    hb = q // CONV_HALO
    cst = lambda r, c_: pl.BlockSpec((r, c_), lambda bi, i: (0, 0))
    return pl.pallas_call(
        functools.partial(_ssd_kernel, q=q, t_valid=t_valid),
        name="ssd_scan",
        out_shape=[jax.ShapeDtypeStruct((b, t, D_SSD), BF16), jax.ShapeDtypeStruct((b, D_SSD, SSD_STATE), F32)],
        grid=(b, t // q),
        in_specs=[pl.BlockSpec((1, q, D_SSD), lambda bi, i: (bi, i, 0)),
                  pl.BlockSpec((1, q, LANES), lambda bi, i: (bi, i, D_SSD // LANES)),
                  pl.BlockSpec((1, q, D_SSD_CONV), lambda bi, i: (bi, i, 0)),
                  pl.BlockSpec((1, CONV_HALO, D_SSD_CONV), lambda bi, i: (bi, jnp.maximum(i * hb - 1, 0), 0)),
                  pl.BlockSpec((1, CONV_HALO, D_SSD_CONV), lambda bi, i: (bi, 0, 0)),
                  pl.BlockSpec((1, D_SSD, SSD_STATE), lambda bi, i: (bi, 0, 0)),
                  cst(SSD_CONV, D_SSD_CONV), cst(1, D_SSD_CONV), cst(1, LANES), cst(1, LANES), cst(1, D_SSD),
                  cst(1, D_SSD), cst(LANES, D_SSD), cst(D_SSD, LANES)],
        out_specs=[pl.BlockSpec((1, q, D_SSD), lambda bi, i: (bi, i, 0)),
                   pl.BlockSpec((1, D_SSD, SSD_STATE), lambda bi, i: (bi, 0, 0))],
        scratch_shapes=[pltpu.VMEM((CONV_HALO + q, D_SSD_CONV), F32), pltpu.VMEM((D_SSD, SSD_STATE), F32)],
        compiler_params=_cparams("parallel", "arbitrary"),
    )(zdt, zdt, xbc, xbc, state8, h0, cw, cb, dtb, alog, dsk, nw, ex, ext_t)


def _merge_kernel(x_ref, oa_ref, op_ref, os_ref, gt_ref, g1_ref, wa_ref, wp_ref, ws_ref, wo_ref, o_ref):
    def gate(bi):
        return jax.nn.sigmoid(gt_ref[:, bi * D_MODEL:(bi + 1) * D_MODEL])
    merged = (gate(0) * jnp.dot(oa_ref[...], wa_ref[...], preferred_element_type=F32)
              + gate(1) * jnp.dot(op_ref[...], wp_ref[...], preferred_element_type=F32)
              + gate(2) * jnp.dot(os_ref[...], ws_ref[...], preferred_element_type=F32))
    o_ref[...] = x_ref[...] + g1_ref[0] * jnp.dot(merged.astype(BF16), wo_ref[...], preferred_element_type=F32)


def _merge(x, oa, op, os_, gates, mv_g1, mv_idx, wa, wp, ws, wo, tm):
    n, d = x.shape
    rb = mv_g1.shape[1]
    rowb = lambda w: pl.BlockSpec((tm, w), lambda i: (i, 0))
    cst = lambda r, c: pl.BlockSpec((r, c), lambda i: (0, 0))
    return pl.pallas_call(
        _merge_kernel,
        name="merge_out",
        out_shape=jax.ShapeDtypeStruct((n, d), F32),
        grid=(n // tm,),
        in_specs=[rowb(d), rowb(D_ATTN), rowb(D_POOL), rowb(D_SSD), rowb(N_BRANCH * D_MODEL),
                  pl.BlockSpec((1, rb, d), lambda i: (mv_idx(i), 0, 0)),
                  cst(D_ATTN, d), cst(D_POOL, d), cst(D_SSD, d), cst(d, d)],
        out_specs=rowb(d),
        compiler_params=_cparams("parallel"),
    )(x, oa, op, os_, gates, mv_g1, wa, wp, ws, wo)


def _ffn_down_kernel(x_ref, ua_ref, ug_ref, ha_ref, hg_ref, sa_ref, sg_ref, g2_ref, cwa_ref, cwg_ref, cba_ref,
                     cbg_ref, wd_ref, o_ref, ea_ref, eg_ref, *, tb):
    i = pl.program_id(1)
    first = i == 0

    def conv(e_ref, u_ref, h_ref, s_ref, cw_ref, cb_ref):
        e_ref[0:CONV_HALO] = jnp.where(first, s_ref[0], h_ref[0])
        e_ref[CONV_HALO:CONV_HALO + tb] = u_ref[0]
        out = cb_ref[...]
        for j in range(FFN_CONV):
            off = CONV_HALO - (FFN_CONV - 1) + j
            out = out + e_ref[off:off + tb] * cw_ref[j:j + 1, :]
        return out

    a = conv(ea_ref, ua_ref, ha_ref, sa_ref, cwa_ref, cba_ref)
    g = conv(eg_ref, ug_ref, hg_ref, sg_ref, cwg_ref, cbg_ref)
    act = (_silu(g) * a).astype(BF16)
    o_ref[0] = x_ref[0] + g2_ref[0] * jnp.dot(act, wd_ref[...], preferred_element_type=F32)


def _ffn_down(x, ua, ug, sa, sg, g2, cwa, cwg, cba, cbg, wd):
    b, t, d = x.shape
    f = ua.shape[2]
    tb = SEQ_BLOCK
    hb = tb // CONV_HALO
    blk = lambda w: pl.BlockSpec((1, tb, w), lambda bi, i: (bi, i, 0))
    halo = pl.BlockSpec((1, CONV_HALO, f), lambda bi, i: (bi, jnp.maximum(i * hb - 1, 0), 0))
    st = pl.BlockSpec((1, CONV_HALO, f), lambda bi, i: (bi, 0, 0))
    cst = lambda r, c: pl.BlockSpec((r, c), lambda bi, i: (0, 0))
    return pl.pallas_call(
        functools.partial(_ffn_down_kernel, tb=tb),
        name="ffn_down",
        out_shape=jax.ShapeDtypeStruct((b, t, d), F32),
        grid=(b, t // tb),
        in_specs=[blk(d), blk(f), blk(f), halo, halo, st, st,
                  pl.BlockSpec((1, 1, d), lambda bi, i: (bi, 0, 0)),
                  cst(FFN_CONV, f), cst(FFN_CONV, f), cst(1, f), cst(1, f), cst(f, d)],
        out_specs=blk(d),
        scratch_shapes=[pltpu.VMEM((CONV_HALO + tb, f), F32), pltpu.VMEM((CONV_HALO + tb, f), F32)],
        compiler_params=_cparams("parallel", "parallel"),
    )(x, ua, ug, ua, ug, sa, sg, g2, cwa, cwg, cba, cbg, wd)


def _pad_rows(a, rows, front=False):
    extra = rows - a.shape[1]
    if extra == 0:
        return a
    return jnp.pad(a, ((0, 0), (extra, 0) if front else (0, extra), (0, 0)))


def _rope_tables(pos):
    half = HEAD_DIM // 2
    freqs = ROPE_THETA ** (-jnp.arange(half, dtype=jnp.float32) / half)
    ang = pos.astype(jnp.float32)[:, None] * freqs[None, :]
    cos, sin = jnp.cos(ang), jnp.sin(ang)
    return jnp.concatenate([cos, cos, cos, cos], axis=-1), jnp.concatenate([-sin, sin, -sin, sin], axis=-1)


def _layer_weights(l, w_ada, b_ada, norm1, norm2, w_in, q_norm, k_norm, pool_w, pool_scale, ssd_conv_w, ssd_conv_b,
                   ssd_dt_bias, ssd_a_log, ssd_d, ssd_norm, w_branch_attn, w_branch_pool, w_branch_ssd, w_out,
                   ffn_up, ffn_conv_w, ffn_conv_b, ffn_down):
    o = _OFFS
    win = w_in[l]
    col = lambda a, b: win[:, a:b]
    zpad = lambda n: jnp.zeros((D_MODEL, n), win.dtype)
    row = lambda v: v.reshape(1, -1).astype(F32)
    lanepad = lambda v: jnp.pad(v.astype(F32), (0, LANES - v.shape[0])).reshape(1, LANES)
    return dict(
        w_ada=w_ada[l].astype(BF16), b_ada=row(b_ada[l]), norm1=row(norm1[l]), norm2=row(norm2[l]),
        w_attn=jnp.concatenate([col(o[0], o[6]), zpad(LANES - IDX_DIM - IDX_HEADS)], axis=1).astype(BF16),
        w_pool=col(o[6], o[7]).astype(BF16),
        w_zdt=jnp.concatenate([col(o[7], o[8]), col(o[9], o[10]), zpad(LANES - SSD_HEADS)], axis=1).astype(BF16),
        w_xbc=col(o[8], o[9]).astype(BF16),
        w_gate=col(o[10], o[11]).astype(BF16),
        qn=row(jnp.tile(q_norm[l], LANES // HEAD_DIM)), kn=row(jnp.tile(k_norm[l], LANES // HEAD_DIM)),
        pool_w=pool_w[l].astype(BF16), pool_scale=row(pool_scale[l]),
        cw=ssd_conv_w[l].astype(F32), cb=row(ssd_conv_b[l]), dtb=lanepad(ssd_dt_bias[l]), alog=lanepad(ssd_a_log[l]),
        dsk=row(jnp.repeat(ssd_d[l], SSD_HEAD_DIM)), nw=row(ssd_norm[l]),
        wa=w_branch_attn[l].astype(BF16), wp=w_branch_pool[l].astype(BF16), ws=w_branch_ssd[l].astype(BF16),
        wo=w_out[l].astype(BF16),
        up_a=ffn_up[l][:, :D_FF].astype(BF16), up_g=ffn_up[l][:, D_FF:].astype(BF16),
        cwa=ffn_conv_w[l][:, :D_FF].astype(F32), cwg=ffn_conv_w[l][:, D_FF:].astype(F32),
        cba=row(ffn_conv_b[l][:D_FF]), cbg=row(ffn_conv_b[l][D_FF:]), wd=ffn_down[l].astype(BF16))


def _group_layer(x, mod, lw, consts, *, pos0, past, states):
    b, t, d = x.shape
    n = b * t
    tp = -(-t // SEQ_BLOCK) * SEQ_BLOCK
    per_seq = t % SEQ_BLOCK == 0
    tm = 256 if per_seq else n
    sh1, sc1, g1, sh2, sc2, g2 = [mod[:, i * d:(i + 1) * d] for i in range(6)]
    pos = pos0 + jnp.arange(t)
    cos_t, sin_t = _rope_tables(pos)
    if per_seq:
        view = lambda m: m.reshape(b, 1, d)
        mv_idx = lambda i: i // (t // tm)
        tab = lambda a: a.reshape(t // tm, tm, LANES)
        tab_idx = lambda i: i % (t // tm)
    else:
        view = lambda m: jnp.repeat(m, t, axis=0).reshape(1, n, d)
        mv_idx = lambda i: 0
        tab = lambda a: jnp.tile(a, (b, 1)).reshape(1, n, LANES)
        tab_idx = lambda i: 0
    xf = x.reshape(n, d)
    nm = lambda w: _norm_matmul(xf, view(sh1), view(sc1), mv_idx, lw['norm1'], w, tm)
    pa, pool_u, zdt, xbc, gates = nm(lw['w_attn']), nm(lw['w_pool']), nm(lw['w_zdt']), nm(lw['w_xbc']), nm(lw['w_gate'])

    q_r, k_r, qi_r, ki2, wi_s = _attn_prep(pa, tab(cos_t), tab(sin_t), tab_idx, lw['qn'], lw['kn'], consts['seg'], tm)
    v_new = pa[:, D_ATTN + D_KV:D_ATTN + 2 * D_KV]
    seq3 = lambda a: a.reshape(b, t, a.shape[-1])
    if past is None:
        tq, tk = ATTN_Q_ROWS, min(ATTN_KEYS, t)
        k_all, v_all, ki_all = seq3(k_r).astype(BF16), seq3(v_new).astype(BF16), seq3(ki2).astype(BF16)
        s_valid = t
        qpad = lambda a: seq3(a)
    else:
        tq = 2 * SUBLANES
        page_table, pool_off, ck, cv, ci = past
        k_all, v_all, ki_all = _page_gather(page_table, pool_off, ck, cv, ci, seq3(k_r), seq3(v_new), seq3(ki2))
        tk = min(ATTN_KEYS_DECODE, k_all.shape[1] // 2)
        s_valid = pos0 + t
        qpad = lambda a: _pad_rows(seq3(a), tq)
    assert k_all.shape[1] % tk == 0
    kk = min(INDEX_TOPK, s_valid // 4)
    o_attn = _attention(qpad(q_r), qpad(qi_r), qpad(wi_s), k_all, v_all, ki_all,
                        tq=tq, tk=tk, pos0=pos0, s_valid=s_valid, kk=kk)[:, :t].reshape(n, D_ATTN)

    st_pool, st_sconv, st_ssd, st_fconv = states
    seqp = lambda a: _pad_rows(seq3(a), tp)
    o_pool = _pool(seqp(pool_u), _pad_rows(st_pool, POOL_HALO, front=True), lw['pool_w'], lw['pool_scale'], pos0=pos0)
    o_ssd, ssd_state = _ssd(seqp(zdt), seqp(xbc), _pad_rows(st_sconv, CONV_HALO, front=True),
                            st_ssd.reshape(b, D_SSD, SSD_STATE), lw['cw'], lw['cb'], lw['dtb'], lw['alog'], lw['dsk'],
                            lw['nw'], consts['ex'], consts['ext_t'], t_valid=t)
    unpad = lambda a: a[:, :t].reshape(n, a.shape[-1])
    x1 = _merge(xf, o_attn, unpad(o_pool), unpad(o_ssd), gates, view(g1), mv_idx,
                lw['wa'], lw['wp'], lw['ws'], lw['wo'], tm)

    nm2 = lambda w: _norm_matmul(x1, view(sh2), view(sc2), mv_idx, lw['norm2'], w, tm)
    ua, ug = nm2(lw['up_a']), nm2(lw['up_g'])
    st_f = _pad_rows(st_fconv, CONV_HALO, front=True)
    x2 = _ffn_down(seqp(x1), seqp(ua), seqp(ug), st_f[:, :, :D_FF], st_f[:, :, D_FF:], g2.reshape(b, 1, d),
                   lw['cwa'], lw['cwg'], lw['cba'], lw['cbg'], lw['wd'])[:, :t]

    def tail(prev, news, keep):
        last = jnp.concatenate([seq3(a)[:, -min(keep, t):] for a in news], axis=-1)
        return jnp.concatenate([prev, last], axis=1)[:, -keep:]

    new_states = (seq3(k_r).reshape(b, t, N_KV_HEADS, HEAD_DIM), seq3(v_new).reshape(b, t, N_KV_HEADS, HEAD_DIM),
                  seq3(ki2)[:, :, :IDX_DIM], tail(st_pool, [pool_u], POOL_CTX), tail(st_sconv, [xbc], SSD_CONV - 1),
                  ssd_state.reshape(b, SSD_HEADS, SSD_HEAD_DIM, SSD_STATE), tail(st_fconv, [ua, ug], FFN_CONV - 1))
    return x2, new_states


def kernel(x_prompt, x_sample, c_prompt, c_sample, cache_k, cache_v, cache_kidx, page_table, state_pool, state_ssd_conv, state_ssd, state_ffn_conv, w_ada, b_ada, norm1, norm2, w_in, q_norm, k_norm, pool_w, pool_scale, ssd_conv_w, ssd_conv_b, ssd_dt_bias, ssd_a_log, ssd_d, ssd_norm, w_branch_attn, w_branch_pool, w_branch_ssd, w_out, ffn_up, ffn_conv_w, ffn_conv_b, ffn_down):
    bp, tp, d = x_prompt.shape
    bs, ts, _ = x_sample.shape
    depth = w_ada.shape[0]
    n_pool, page = cache_k.shape[1], cache_k.shape[2]
    past_len = page_table.shape[1] * page
    dt = x_prompt.dtype

    r = lax.broadcasted_iota(jnp.int32, (LANES, LANES), 0)
    c = lax.broadcasted_iota(jnp.int32, (LANES, LANES), 1)
    hr = lax.broadcasted_iota(jnp.int32, (LANES, D_SSD), 0)
    hc = lax.broadcasted_iota(jnp.int32, (LANES, D_SSD), 1)
    ex = (hr == hc // SSD_HEAD_DIM).astype(BF16)
    consts = dict(seg=(r // HEAD_DIM == c // HEAD_DIM).astype(BF16), ex=ex, ext_t=ex.T)

    c_all = jnp.concatenate([c_prompt, c_sample], axis=0)
    rows = -(-c_all.shape[0] // SUBLANES) * SUBLANES
    c_pad = jnp.pad(c_all, ((0, rows - c_all.shape[0]), (0, 0)))

    ck_t = jnp.transpose(cache_k, (0, 1, 3, 4, 2)).reshape(depth * n_pool, D_KV, page)
    cv_t = jnp.transpose(cache_v, (0, 1, 3, 4, 2)).reshape(depth * n_pool, D_KV, page)
    ci_t = jnp.transpose(cache_kidx, (0, 1, 3, 2)).reshape(depth * n_pool, IDX_DIM, page)

    zeros = lambda *s: jnp.zeros(s, dt)
    xp, xs = x_prompt, x_sample
    new_p, new_s = [[] for _ in range(7)], [[] for _ in range(7)]
    for l in range(depth):
        lw = _layer_weights(l, w_ada, b_ada, norm1, norm2, w_in, q_norm, k_norm, pool_w, pool_scale, ssd_conv_w,
                            ssd_conv_b, ssd_dt_bias, ssd_a_log, ssd_d, ssd_norm, w_branch_attn, w_branch_pool,
                            w_branch_ssd, w_out, ffn_up, ffn_conv_w, ffn_conv_b, ffn_down)
        mod = _mod(c_pad, lw['w_ada'], lw['b_ada'])
        xp, st_p = _group_layer(
            xp, mod[:bp], lw, consts, pos0=0, past=None,
            states=(zeros(bp, POOL_CTX, D_POOL), zeros(bp, SSD_CONV - 1, D_SSD_CONV),
                    zeros(bp, SSD_HEADS, SSD_HEAD_DIM, SSD_STATE), zeros(bp, FFN_CONV - 1, 2 * D_FF)))
        past = (page_table, l * n_pool, ck_t, cv_t, ci_t)
        xs, st_s = _group_layer(xs, mod[bp:bp + bs], lw, consts, pos0=past_len, past=past,
                                states=(state_pool[l], state_ssd_conv[l], state_ssd[l], state_ffn_conv[l]))
        for j in range(7):
            new_p[j].append(st_p[j])
            new_s[j].append(st_s[j])
    outs_p = [jnp.stack(s) for s in new_p]
    outs_s = [jnp.stack(s) for s in new_s]
    return (xp, xs, *outs_p, *outs_s)
```

```python
import functools

import numpy as np
import jax
import jax.numpy as jnp
from jax import lax
from jax.experimental import pallas as pl
from jax.experimental.pallas import tpu as pltpu

D_MODEL = 1024
N_HEADS = 16
N_KV_HEADS = 4
HEAD_DIM = 64
D_ATTN = N_HEADS * HEAD_DIM
D_KV = N_KV_HEADS * HEAD_DIM
IDX_HEADS = 8
IDX_DIM = 64
INDEX_TOPK = 256
ROPE_THETA = 10000.0
POOL_WINDOWS = (2, 4, 8, 16)
D_POOL = D_MODEL
POOL_GROUP = D_POOL // len(POOL_WINDOWS)
POOL_CTX = max(POOL_WINDOWS) - 1
D_SSD = 2 * D_MODEL
SSD_HEAD_DIM = 64
SSD_HEADS = D_SSD // SSD_HEAD_DIM
SSD_GROUPS = 4
SSD_STATE = 128
SSD_CONV = 4
D_SSD_CONV = D_SSD + 2 * SSD_GROUPS * SSD_STATE
D_FF = 2816
FFN_CONV = 3
N_BRANCH = 3
EPS = 1e-6
_SPLITS = (D_ATTN, D_KV, D_KV, IDX_HEADS * IDX_DIM, IDX_DIM, IDX_HEADS, D_POOL, D_SSD, D_SSD_CONV,
           SSD_HEADS, N_BRANCH * D_MODEL)
_OFFS = tuple(int(v) for v in np.cumsum((0,) + _SPLITS))

F32 = jnp.float32
BF16 = jnp.bfloat16
LANES = 128
SUBLANES = 8
SEQ_BLOCK = 128
ROW_TILE = 512
ATTN_Q_ROWS = 256
ATTN_KEYS = 256
ATTN_KEYS_DECODE = 1024
VMEM_LIMIT = 56 * 1024 * 1024
NEG = -1e30
LOG2E = 1.4426950408889634
N_BISECT = 20
NT = (((1,), (1,)), ((), ()))
TN = (((0,), (0,)), ((), ()))


def _cparams(*sem):
    return pltpu.CompilerParams(dimension_semantics=sem, vmem_limit_bytes=VMEM_LIMIT)


def _silu(v):
    return (0.5 * v) * (1.0 + jnp.tanh(0.5 * v))


def _split3(v):
    h1 = v.astype(BF16)
    r1 = v - h1.astype(F32)
    h2 = r1.astype(BF16)
    h3 = (r1 - h2.astype(F32)).astype(BF16)
    return h1, h2, h3


def _dot3(v, w):
    h1, h2, h3 = _split3(v)
    return (jnp.dot(h1, w, preferred_element_type=F32) + jnp.dot(h2, w, preferred_element_type=F32)
            + jnp.dot(h3, w, preferred_element_type=F32))


def _mod_kernel(c_ref, w_ref, b_ref, o_ref):
    s = _silu(c_ref[...])
    o_ref[...] = jnp.dot(s.astype(BF16), w_ref[...], preferred_element_type=F32) + b_ref[...]


def _mod(c_pad, w, b):
    rows, d = c_pad.shape
    n = w.shape[1]
    tn = n // 4
    return pl.pallas_call(
        _mod_kernel,
        name="adaln_mod",
        out_shape=jax.ShapeDtypeStruct((rows, n), F32),
        grid=(n // tn,),
        in_specs=[pl.BlockSpec((rows, d), lambda j: (0, 0)),
                  pl.BlockSpec((d, tn), lambda j: (0, j)),
                  pl.BlockSpec((1, tn), lambda j: (0, j))],
        out_specs=pl.BlockSpec((rows, tn), lambda j: (0, j)),
        compiler_params=_cparams("parallel"),
    )(c_pad, w, b)


def _norm_matmul_kernel(x_ref, sh_ref, sc_ref, g_ref, w_ref, o_ref):
    x = x_ref[...]
    y = x * lax.rsqrt(jnp.mean(x * x, axis=-1, keepdims=True) + EPS) * g_ref[...]
    h = y * (1.0 + sc_ref[0]) + sh_ref[0]
    o_ref[...] = jnp.dot(h.astype(BF16), w_ref[...], preferred_element_type=F32).astype(o_ref.dtype)


def _norm_matmul(x, mv_sh, mv_sc, mv_idx, g, w, tm):
    n, d = x.shape
    nout = w.shape[1]
    rb = mv_sh.shape[1]
    return pl.pallas_call(
        _norm_matmul_kernel,
        name="norm_matmul",
        out_shape=jax.ShapeDtypeStruct((n, nout), F32),
        grid=(n // tm,),
        in_specs=[pl.BlockSpec((tm, d), lambda i: (i, 0)),
                  pl.BlockSpec((1, rb, d), lambda i: (mv_idx(i), 0, 0)),
                  pl.BlockSpec((1, rb, d), lambda i: (mv_idx(i), 0, 0)),
                  pl.BlockSpec((1, d), lambda i: (0, 0)),
                  pl.BlockSpec((d, nout), lambda i: (0, 0))],
        out_specs=pl.BlockSpec((tm, nout), lambda i: (i, 0)),
        compiler_params=_cparams("parallel"),
    )(x, mv_sh, mv_sc, g, w)


def _attn_prep_kernel(q_ref, k_ref, qi_ref, kw_ref, cos_ref, sin_ref, qn_ref, kn_ref, seg_ref,
                      qo_ref, ko_ref, qio_ref, kio_ref, wio_ref):
    tm = q_ref.shape[0]
    lane = lax.broadcasted_iota(jnp.int32, (tm, LANES), 1)
    first_half = (lane % HEAD_DIM) < (HEAD_DIM // 2)
    cos = cos_ref[0]
    sin = sin_ref[0]
    seg = seg_ref[...]

    def rope(v):
        up = pltpu.roll(v, LANES - HEAD_DIM // 2, 1)
        dn = pltpu.roll(v, HEAD_DIM // 2, 1)
        return v * cos + jnp.where(first_half, up, dn) * sin

    def headnorm(v, gain):
        sq = v * v
        hi = sq.astype(BF16)
        lo = (sq - hi.astype(F32)).astype(BF16)
        ss = jnp.dot(hi, seg, preferred_element_type=F32) + jnp.dot(lo, seg, preferred_element_type=F32)
        return v * lax.rsqrt(ss * (1.0 / HEAD_DIM) + EPS) * gain

    qn = qn_ref[...]
    kn = kn_ref[...]
    for j in range(D_ATTN // LANES):
        sl = slice(j * LANES, (j + 1) * LANES)
        qo_ref[:, sl] = rope(headnorm(q_ref[:, sl], qn)) * (HEAD_DIM ** -0.5 * LOG2E)
    for j in range(D_KV // LANES):
        sl = slice(j * LANES, (j + 1) * LANES)
        ko_ref[:, sl] = rope(headnorm(k_ref[:, sl], kn))
    for j in range(IDX_HEADS * IDX_DIM // LANES):
        sl = slice(j * LANES, (j + 1) * LANES)
        qio_ref[:, sl] = rope(qi_ref[:, sl])
    kw = kw_ref[...]
    kr = rope(kw)
    kio_ref[...] = jnp.where(lane < IDX_DIM, kr, pltpu.roll(kr, IDX_DIM, 1))
    wio_ref[...] = pltpu.roll(kw, LANES - IDX_DIM, 1) * (IDX_HEADS ** -0.5 * IDX_DIM ** -0.5)


def _attn_prep(pa, cos_t, sin_t, tab_idx, qn, kn, seg, tm):
    n = pa.shape[0]
    blk = lambda w, c: pl.BlockSpec((tm, w), lambda i: (i, c))
    tab = pl.BlockSpec((1, tm, LANES), lambda i: (tab_idx(i), 0, 0))
    cst = lambda r, c: pl.BlockSpec((r, c), lambda i: (0, 0))
    return pl.pallas_call(
        _attn_prep_kernel,
        name="attn_prep",
        out_shape=[jax.ShapeDtypeStruct((n, D_ATTN), F32), jax.ShapeDtypeStruct((n, D_KV), F32),
                   jax.ShapeDtypeStruct((n, IDX_HEADS * IDX_DIM), F32),
                   jax.ShapeDtypeStruct((n, LANES), F32), jax.ShapeDtypeStruct((n, LANES), F32)],
        grid=(n // tm,),
        in_specs=[blk(D_ATTN, 0), blk(D_KV, D_ATTN // D_KV), blk(IDX_HEADS * IDX_DIM, 3), blk(LANES, 16),
                  tab, tab, cst(1, LANES), cst(1, LANES), cst(LANES, LANES)],
        out_specs=[blk(D_ATTN, 0), blk(D_KV, 0), blk(IDX_HEADS * IDX_DIM, 0), blk(LANES, 0), blk(LANES, 0)],
        compiler_params=_cparams("parallel"),
    )(pa, pa, pa, pa, cos_t, sin_t, qn, kn, seg)


GATHER_PAGES = 8


def _page_gather_kernel(pt_ref, *refs, n_steps, pg, page):
    ins, outs = refs[:3 * pg + 3], refs[3 * pg + 3:]
    past = pl.program_id(1) < n_steps
    for t in range(3):
        new_ref, out_ref = ins[3 * pg + t], outs[t]
        for i in range(pg):
            rows = slice(i * page, (i + 1) * page)
            tile = ins[t * pg + i][0]
            if t == 2:
                tile = jnp.concatenate([tile, tile], axis=0)
            out_ref[0, rows, :] = jnp.where(past, tile.T, new_ref[0, rows, :]).astype(out_ref.dtype)


def _page_gather(page_table, pool_off, ck, cv, ci, nk, nv, ni):
    bd, n_pages = page_table.shape
    page = ck.shape[2]
    pg = max(g for g in range(1, GATHER_PAGES + 1) if n_pages % g == 0)
    n_steps = n_pages // pg
    nk, nv, ni = [_pad_rows(a, pg * page) for a in (nk, nv, ni)]

    def cache(w, i):
        return pl.BlockSpec((1, w, page),
                            lambda b, p, pt: (pool_off + pt[b, jnp.minimum(p, n_steps - 1) * pg + i], 0, 0))

    new = lambda w: pl.BlockSpec((1, pg * page, w), lambda b, p, pt: (b, 0, 0))
    out = lambda w: pl.BlockSpec((1, pg * page, w), lambda b, p, pt: (b, p, 0))
    s_pad = (n_steps + 1) * pg * page
    in_specs = ([cache(D_KV, i) for i in range(pg)] + [cache(D_KV, i) for i in range(pg)]
                + [cache(IDX_DIM, i) for i in range(pg)] + [new(D_KV), new(D_KV), new(2 * IDX_DIM)])
    return pl.pallas_call(
        functools.partial(_page_gather_kernel, n_steps=n_steps, pg=pg, page=page),
        out_shape=[jax.ShapeDtypeStruct((bd, s_pad, D_KV), BF16), jax.ShapeDtypeStruct((bd, s_pad, D_KV), BF16),
                   jax.ShapeDtypeStruct((bd, s_pad, 2 * IDX_DIM), BF16)],
        grid_spec=pltpu.PrefetchScalarGridSpec(
            num_scalar_prefetch=1, grid=(bd, n_steps + 1), in_specs=in_specs,
            out_specs=[out(D_KV), out(D_KV), out(2 * IDX_DIM)]),
        compiler_params=_cparams("parallel", "parallel"),
        name="page_gather",
    )(page_table, *([ck] * pg), *([cv] * pg), *([ci] * pg), nk, nv, ni)


GQA = N_HEADS // N_KV_HEADS


def _attn_kernel(q_ref, qi_ref, wi_ref, k_ref, v_ref, ki_ref, o_ref,
                 sc_ref, qh_ref, qim_ref, wb_ref, m_ref, l_ref, acc_ref, *, tq, tk, kw, nblk, pos0, s_valid, kk):
    q0 = pl.program_id(1) * tq
    nch = tk // LANES
    ngrp = D_KV // kw
    hpg = N_HEADS // ngrp
    lane = lax.broadcasted_iota(jnp.int32, (tq, LANES), 1)
    row = lax.broadcasted_iota(jnp.int32, (tq, LANES), 0)
    qpos = pos0 + q0 + row
    lo_half = lane < HEAD_DIM
    n_kv = jnp.minimum(nblk, (pos0 + q0 + tq - 1) // tk + 1)
    kkf = float(kk)
    chunk = lambda a, j: a[..., j * LANES:(j + 1) * LANES]

    def bcast(col):
        return jnp.broadcast_to(col, col.shape[:-1] + (LANES,))

    wi = wi_ref[0]
    for j in range(IDX_HEADS // 2):
        t = qi_ref[0, :, j * LANES:(j + 1) * LANES]
        qim_ref[2 * j * tq:(2 * j + 1) * tq] = jnp.where(lo_half, t, 0.0).astype(BF16)
        qim_ref[(2 * j + 1) * tq:(2 * j + 2) * tq] = jnp.where(lo_half, 0.0, t).astype(BF16)
    for h in range(IDX_HEADS):
        wb_ref[h] = bcast(wi[:, h:h + 1])
    def kv_tile(h):
        kvh = h // GQA
        return (kvh * HEAD_DIM % kw) // LANES, kvh % 2

    if kw > LANES:
        qh_ref[...] = jnp.zeros(qh_ref.shape, BF16)
    for h in range(N_HEADS):
        t = q_ref[0, :, (h // 2) * LANES:(h // 2 + 1) * LANES]
        tile, kv_half = kv_tile(h)
        if h % 2 != kv_half:
            t = pltpu.roll(t, HEAD_DIM, 1)
        keep = lo_half if kv_half == 0 else jnp.logical_not(lo_half)
        hh = h % hpg
        qh_ref[h // hpg, hh * tq:(hh + 1) * tq, tile * LANES:(tile + 1) * LANES] = jnp.where(keep, t, 0.0).astype(BF16)
    m_ref[...] = jnp.full(m_ref.shape, NEG, F32)
    l_ref[...] = jnp.zeros(l_ref.shape, F32)
    acc_ref[...] = jnp.zeros(acc_ref.shape, F32)

    def p1(c, carry):
        kid = ki_ref[0, pl.ds(pl.multiple_of(c * tk, tk), tk), :]
        y = lax.dot_general(qim_ref[...], kid, NT, preferred_element_type=F32)
        y = jnp.maximum(y, 0.0).reshape(IDX_HEADS, tq, tk)
        for j in range(nch):
            acc = wb_ref[0] * chunk(y[0], j)
            for h in range(1, IDX_HEADS):
                acc = acc + wb_ref[h] * chunk(y[h], j)
            kpos = c * tk + j * LANES + lane
            vis = jnp.logical_and(kpos <= qpos, kpos < s_valid)
            sc_ref[c, :, j * LANES:(j + 1) * LANES] = jnp.where(vis, acc, -jnp.inf)
        return carry

    lax.fori_loop(0, n_kv, p1, 0)

    def count_gt(t_b):
        def body(c, acc):
            x = sc_ref[c]
            for j in range(nch):
                acc = acc + jnp.where(chunk(x, j) > t_b, 1.0, 0.0)
            return acc
        acc = lax.fori_loop(0, n_kv, body, jnp.zeros((tq, LANES), F32))
        return bcast(jnp.sum(acc, axis=1, keepdims=True))

    def next_above(t_b):
        def body(c, acc):
            x = sc_ref[c]
            for j in range(nch):
                xj = chunk(x, j)
                acc = jnp.minimum(acc, jnp.where(xj > t_b, xj, jnp.inf))
            return acc
        acc = lax.fori_loop(0, n_kv, body, jnp.full((tq, LANES), jnp.inf, F32))
        return bcast(jnp.min(acc, axis=1, keepdims=True))

    def stat(c, carry):
        mn, mx = carry
        x = sc_ref[c]
        for j in range(nch):
            xj = chunk(x, j)
            mn = jnp.minimum(mn, jnp.where(xj == -jnp.inf, jnp.inf, xj))
            mx = jnp.maximum(mx, xj)
        return mn, mx

    mn, mx = lax.fori_loop(0, n_kv, stat, (jnp.full((tq, LANES), jnp.inf, F32), jnp.full((tq, LANES), -jnp.inf, F32)))
    rmin = bcast(jnp.min(mn, axis=1, keepdims=True))
    rmax = bcast(jnp.max(mx, axis=1, keepdims=True))
    c_min = count_gt(rmin)
    at_min = c_min < kkf

    def bisect(i, carry):
        lo, hi = carry
        mid = lo + 0.5 * (hi - lo)
        up = count_gt(mid) >= kkf
        return jnp.where(up, mid, lo), jnp.where(up, hi, mid)

    lo, hi = lax.fori_loop(0, N_BISECT, bisect, (rmin, rmax))
    tau = next_above(lo)
    ct = count_gt(tau)

    def pending(ct_):
        return jnp.logical_and(ct_ >= kkf, jnp.logical_not(at_min))

    def w_cond(st):
        return jnp.max(jnp.where(pending(st[2]), 1.0, 0.0)) > 0.5

    def w_body(st):
        lo_, tau_, ct_ = st
        need = pending(ct_)
        lo2 = jnp.where(need, tau_, lo_)
        tau2 = next_above(lo2)
        ct2 = count_gt(tau2)
        return lo2, jnp.where(need, tau2, tau_), jnp.where(need, ct2, ct_)

    lo, tau, ct = lax.while_loop(w_cond, w_body, (lo, tau, ct))
    tau = jnp.where(at_min, rmin, tau)
    m_need = kkf - jnp.where(at_min, c_min, ct)

    rk = lax.broadcasted_iota(jnp.int32, (tk, tk), 0)
    ck = lax.broadcasted_iota(jnp.int32, (tk, tk), 1)
    before = jnp.where(rk < ck, 1.0, 0.0).astype(BF16)

    def p2(c, cum):
        x = sc_ref[c]
        eqs = [chunk(x, j) == tau for j in range(nch)]
        eqf = [jnp.where(e, 1.0, 0.0) for e in eqs]
        pre = jnp.dot(jnp.concatenate(eqf, axis=-1).astype(BF16), before, preferred_element_type=F32)
        bias = [jnp.where(chunk(x, j) > tau, 0.0,
                          jnp.where(eqs[j], jnp.where(chunk(pre, j) + cum < m_need, 0.0, NEG), NEG))
                for j in range(nch)]
        tot = eqf[0]
        for j in range(1, nch):
            tot = tot + eqf[j]
        cum = cum + bcast(jnp.sum(tot, axis=1, keepdims=True))
        koff = pl.multiple_of(c * tk, tk)
        for p in range(ngrp):
            kp = k_ref[0, pl.ds(koff, tk), p * kw:(p + 1) * kw]
            vp = v_ref[0, pl.ds(koff, tk), p * kw:(p + 1) * kw]
            s = lax.dot_general(qh_ref[p], kp, NT, preferred_element_type=F32).reshape(hpg, tq, tk)
            sm = [chunk(s, j) + bias[j][None] for j in range(nch)]
            mx_ = sm[0]
            for j in range(1, nch):
                mx_ = jnp.maximum(mx_, sm[j])
            m_old = m_ref[p]
            m_new = jnp.maximum(m_old, bcast(jnp.max(mx_, axis=-1, keepdims=True)))
            pe = [jnp.exp2(sm[j] - m_new) for j in range(nch)]
            ps = pe[0]
            for j in range(1, nch):
                ps = ps + pe[j]
            alpha = jnp.exp2(m_old - m_new)
            l_ref[p] = alpha * l_ref[p] + bcast(jnp.sum(ps, axis=-1, keepdims=True))
            pv = jnp.dot(jnp.concatenate(pe, axis=-1).reshape(hpg * tq, tk).astype(BF16), vp,
                         preferred_element_type=F32)
            alpha = alpha.reshape(hpg * tq, LANES)
            acc_ref[p] = jnp.concatenate([alpha] * (kw // LANES), axis=-1) * acc_ref[p] + pv
            m_ref[p] = m_new
        return cum

    lax.fori_loop(0, n_kv, p2, jnp.zeros((tq, LANES), F32))

    def head_out(h):
        p, hh = h // hpg, h % hpg
        tile, kv_half = kv_tile(h)
        r = acc_ref[p, hh * tq:(hh + 1) * tq, tile * LANES:(tile + 1) * LANES] / l_ref[p, hh]
        if kv_half != h % 2:
            r = pltpu.roll(r, HEAD_DIM, 1)
        return r

    for j in range(N_HEADS // 2):
        o_ref[0, :, j * LANES:(j + 1) * LANES] = jnp.where(lo_half, head_out(2 * j), head_out(2 * j + 1)).astype(o_ref.dtype)


def _attention(q, qi, wi, k_all, v_all, ki2, *, tq, tk, kw, pos0, s_valid, kk):
    b, tq_pad, _ = q.shape
    s_pad = k_all.shape[1]
    nblk = s_pad // tk
    ngrp = D_KV // kw
    hpg = N_HEADS // ngrp
    qblk = lambda w: pl.BlockSpec((1, tq, w), lambda bi, qi_: (bi, qi_, 0))
    seq = lambda w: pl.BlockSpec((1, s_pad, w), lambda bi, qi_: (bi, 0, 0))
    return pl.pallas_call(
        functools.partial(_attn_kernel, tq=tq, tk=tk, kw=kw, nblk=nblk, pos0=pos0, s_valid=s_valid, kk=kk),
        out_shape=jax.ShapeDtypeStruct((b, tq_pad, D_ATTN), BF16),
        grid=(b, tq_pad // tq),
        in_specs=[qblk(D_ATTN), qblk(IDX_HEADS * IDX_DIM), qblk(LANES), seq(D_KV), seq(D_KV), seq(LANES)],
        out_specs=qblk(D_ATTN),
        scratch_shapes=[pltpu.VMEM((nblk, tq, tk), F32), pltpu.VMEM((ngrp, hpg * tq, kw), BF16),
                        pltpu.VMEM((IDX_HEADS * tq, LANES), BF16), pltpu.VMEM((IDX_HEADS, tq, LANES), F32),
                        pltpu.VMEM((ngrp, hpg, tq, LANES), F32), pltpu.VMEM((ngrp, hpg, tq, LANES), F32),
                        pltpu.VMEM((ngrp, hpg * tq, kw), F32)],
        compiler_params=_cparams("parallel", "parallel"),
        name="sparse_attention",
    )(q, qi, wi, k_all, v_all, ki2)


POOL_HALO = 16


def _pool_kernel(u_ref, halo_ref, st_ref, w_ref, scale_ref, o_ref, ext_ref, *, tb, pos0):
    i = pl.program_id(1)
    ext_ref[0:POOL_HALO] = jnp.where(i == 0, st_ref[0], halo_ref[0])
    ext_ref[POOL_HALO:POOL_HALO + tb] = u_ref[0]
    pos = pos0 + i * tb + lax.broadcasted_iota(jnp.int32, (tb, 1), 0)
    for gi, w in enumerate(POOL_WINDOWS):
        sl = slice(gi * POOL_GROUP, (gi + 1) * POOL_GROUP)
        tot = ext_ref[POOL_HALO:POOL_HALO + tb, sl]
        for k in range(1, w):
            tot = tot + ext_ref[POOL_HALO - k:POOL_HALO - k + tb, sl]
        cnt = jnp.minimum(w, pos + 1).astype(F32)
        pooled = tot / cnt - u_ref[0, :, sl]
        mixed = jnp.dot(pooled.astype(BF16), w_ref[gi], preferred_element_type=F32) * scale_ref[:, sl]
        o_ref[0, :, sl] = mixed.astype(o_ref.dtype)


def _pool(u, state16, w, scale, *, pos0):
    b, t, c = u.shape
    tb = SEQ_BLOCK
    hb = tb // POOL_HALO
    return pl.pallas_call(
        functools.partial(_pool_kernel, tb=tb, pos0=pos0),
        name="pool_mix",
        out_shape=jax.ShapeDtypeStruct((b, t, c), BF16),
        grid=(b, t // tb),
        in_specs=[pl.BlockSpec((1, tb, c), lambda bi, i: (bi, i, 0)),
                  pl.BlockSpec((1, POOL_HALO, c), lambda bi, i: (bi, jnp.maximum(i * hb - 1, 0), 0)),
                  pl.BlockSpec((1, POOL_HALO, c), lambda bi, i: (bi, 0, 0)),
                  pl.BlockSpec((len(POOL_WINDOWS), POOL_GROUP, POOL_GROUP), lambda bi, i: (0, 0, 0)),
                  pl.BlockSpec((1, c), lambda bi, i: (0, 0))],
        out_specs=pl.BlockSpec((1, tb, c), lambda bi, i: (bi, i, 0)),
        scratch_shapes=[pltpu.VMEM((POOL_HALO + tb, c), F32)],
        compiler_params=_cparams("parallel", "parallel"),
    )(u, u, state16, w, scale)


CONV_HALO = SUBLANES
SSD_GH = SSD_HEADS // SSD_GROUPS
SSD_GW = SSD_GH * SSD_HEAD_DIM
CONV_CHUNK = 512


def _ssd_kernel(z_ref, dt_ref, xbc_ref, halo_ref, st_ref, h0_ref, cw_ref, cb_ref, dtb_ref, alog_ref, dsk_ref,
                nw_ref, ex_ref, ext_t_ref, y_ref, hT_ref, ext_ref, xs_ref, h_ref, *, q, t_valid):
    c = pl.program_id(1)

    @pl.when(c == 0)
    def _():
        h_ref[...] = h0_ref[0]

    ext_ref[0:CONV_HALO] = jnp.where(c == 0, st_ref[0], halo_ref[0])
    ext_ref[CONV_HALO:CONV_HALO + q] = xbc_ref[0]
    for ch in range(D_SSD_CONV // CONV_CHUNK):
        cc = slice(ch * CONV_CHUNK, (ch + 1) * CONV_CHUNK)
        xc = cb_ref[:, cc]
        for j in range(SSD_CONV):
            off = CONV_HALO - (SSD_CONV - 1) + j
            xc = xc + ext_ref[off:off + q, cc] * cw_ref[j:j + 1, cc]
        xs_ref[:, cc] = _silu(xc)
    x = xs_ref[:, :D_SSD]

    lane = lax.broadcasted_iota(jnp.int32, (q, LANES), 1)
    row = lax.broadcasted_iota(jnp.int32, (q, LANES), 0)
    v = dt_ref[0] + dtb_ref[...]
    dt = jnp.maximum(v, 0.0) + jnp.log1p(jnp.exp(-jnp.abs(v)))
    dt = jnp.where(jnp.logical_and(lane < SSD_HEADS, c * q + row < t_valid), dt, 0.0)
    a = dt * (-jnp.exp(alog_ref[...]))
    tri = jnp.where(lane <= row, 1.0, 0.0).astype(BF16)
    causal = lane <= row
    cs = _dot3_left(tri, a)
    cs_t = cs.T
    ex = ex_ref[...]
    cs_last = cs[q - 1:q, :]
    dt_x = _dot3(dt, ex)
    ecs_x = jnp.exp(_dot3(cs, ex))
    dec_x = jnp.exp(_dot3(cs_last - cs, ex))
    xdt = x * dt_x
    xdec = (xdt * dec_x).astype(BF16)
    xdt16 = xdt.astype(BF16)
    rowdec = jnp.exp(_dot3_left(ext_t_ref[...], jnp.broadcast_to(cs_t[:, q - 1:q], (LANES, LANES))))
    lo_half = lane < SSD_HEAD_DIM

    for g in range(SSD_GROUPS):
        bg = xs_ref[:, D_SSD + g * SSD_STATE:D_SSD + (g + 1) * SSD_STATE].astype(BF16)
        cg = xs_ref[:, D_SSD + (SSD_GROUPS + g) * SSD_STATE:D_SSD + (SSD_GROUPS + g + 1) * SSD_STATE].astype(BF16)
        cb = lax.dot_general(cg, bg, NT, preferred_element_type=F32)
        hs = slice(g * SSD_GW, (g + 1) * SSD_GW)
        hprev = h_ref[hs, :]
        y_off = lax.dot_general(cg, hprev.astype(BF16), NT, preferred_element_type=F32) * ecs_x[:, hs]
        for jp in range(SSD_GH // 2):
            cols = slice(g * SSD_GW + jp * LANES, g * SSD_GW + (jp + 1) * LANES)
            xp = xdt16[:, cols]
            halves = []
            for e in range(2):
                hd = g * SSD_GH + 2 * jp + e
                seg = cs[:, hd:hd + 1] - cs_t[hd:hd + 1, :]
                lmat = jnp.where(causal, jnp.exp(jnp.where(causal, seg, 0.0)), 0.0)
                halves.append(jnp.dot((cb * lmat).astype(BF16), xp, preferred_element_type=F32))
            y_diag = jnp.where(lo_half, halves[0], halves[1])
            yv = y_diag + y_off[:, jp * LANES:(jp + 1) * LANES] + dsk_ref[:, cols] * xs_ref[:, cols]
            zv = z_ref[0, :, cols]
            ext_ref[0:q, cols] = yv * _silu(zv)
        st_new = lax.dot_general(xdec[:, hs], bg, TN, preferred_element_type=F32)
        h_ref[hs, :] = hprev * rowdec[hs, :] + st_new

    for g in range(SSD_GROUPS):
        hs = slice(g * SSD_GW, (g + 1) * SSD_GW)
        yg = ext_ref[0:q, hs]
        yn = yg * lax.rsqrt(jnp.mean(yg * yg, axis=-1, keepdims=True) + EPS) * nw_ref[:, hs]
        y_ref[0, :, hs] = yn.astype(y_ref.dtype)

    @pl.when(c == pl.num_programs(1) - 1)
    def _():
        hT_ref[0] = h_ref[...]


def _dot3_left(w, v):
    h1, h2, h3 = _split3(v)
    return (jnp.dot(w, h1, preferred_element_type=F32) + jnp.dot(w, h2, preferred_element_type=F32)
            + jnp.dot(w, h3, preferred_element_type=F32))


def _ssd(zdt, xbc, state8, h0, cw, cb, dtb, alog, dsk, nw, ex, ext_t, *, t_valid):
    b, t, _ = xbc.shape
    q = SEQ_BLOCK
    hb = q // CONV_HALO
    cst = lambda r, c_: pl.BlockSpec((r, c_), lambda bi, i: (0, 0))
    return pl.pallas_call(
        functools.partial(_ssd_kernel, q=q, t_valid=t_valid),
        name="ssd_scan",
        out_shape=[jax.ShapeDtypeStruct((b, t, D_SSD), BF16), jax.ShapeDtypeStruct((b, D_SSD, SSD_STATE), F32)],
        grid=(b, t // q),
        in_specs=[pl.BlockSpec((1, q, D_SSD), lambda bi, i: (bi, i, 0)),
                  pl.BlockSpec((1, q, LANES), lambda bi, i: (bi, i, D_SSD // LANES)),
                  pl.BlockSpec((1, q, D_SSD_CONV), lambda bi, i: (bi, i, 0)),
                  pl.BlockSpec((1, CONV_HALO, D_SSD_CONV), lambda bi, i: (bi, jnp.maximum(i * hb - 1, 0), 0)),
                  pl.BlockSpec((1, CONV_HALO, D_SSD_CONV), lambda bi, i: (bi, 0, 0)),
                  pl.BlockSpec((1, D_SSD, SSD_STATE), lambda bi, i: (bi, 0, 0)),
                  cst(SSD_CONV, D_SSD_CONV), cst(1, D_SSD_CONV), cst(1, LANES), cst(1, LANES), cst(1, D_SSD),
                  cst(1, D_SSD), cst(LANES, D_SSD), cst(D_SSD, LANES)],
        out_specs=[pl.BlockSpec((1, q, D_SSD), lambda bi, i: (bi, i, 0)),
                   pl.BlockSpec((1, D_SSD, SSD_STATE), lambda bi, i: (bi, 0, 0))],
        scratch_shapes=[pltpu.VMEM((CONV_HALO + q, D_SSD_CONV), F32), pltpu.VMEM((q, D_SSD_CONV), F32),
                        pltpu.VMEM((D_SSD, SSD_STATE), F32)],
        compiler_params=_cparams("parallel", "arbitrary"),
    )(zdt, zdt, xbc, xbc, state8, h0, cw, cb, dtb, alog, dsk, nw, ex, ext_t)


def _merge_kernel(x_ref, oa_ref, op_ref, os_ref, gt_ref, g1_ref, wa_ref, wp_ref, ws_ref, wo_ref, o_ref):
    def gate(bi):
        return jax.nn.sigmoid(gt_ref[:, bi * D_MODEL:(bi + 1) * D_MODEL])
    merged = (gate(0) * jnp.dot(oa_ref[...], wa_ref[...], preferred_element_type=F32)
              + gate(1) * jnp.dot(op_ref[...], wp_ref[...], preferred_element_type=F32)
              + gate(2) * jnp.dot(os_ref[...], ws_ref[...], preferred_element_type=F32))
    o_ref[...] = x_ref[...] + g1_ref[0] * jnp.dot(merged.astype(BF16), wo_ref[...], preferred_element_type=F32)


def _merge(x, oa, op, os_, gates, mv_g1, mv_idx, wa, wp, ws, wo, tm):
    n, d = x.shape
    rb = mv_g1.shape[1]
    rowb = lambda w: pl.BlockSpec((tm, w), lambda i: (i, 0))
    cst = lambda r, c: pl.BlockSpec((r, c), lambda i: (0, 0))
    return pl.pallas_call(
        _merge_kernel,
        name="merge_out",
        out_shape=jax.ShapeDtypeStruct((n, d), F32),
        grid=(n // tm,),
        in_specs=[rowb(d), rowb(D_ATTN), rowb(D_POOL), rowb(D_SSD), rowb(N_BRANCH * D_MODEL),
                  pl.BlockSpec((1, rb, d), lambda i: (mv_idx(i), 0, 0)),
                  cst(D_ATTN, d), cst(D_POOL, d), cst(D_SSD, d), cst(d, d)],
        out_specs=rowb(d),
        compiler_params=_cparams("parallel"),
    )(x, oa, op, os_, gates, mv_g1, wa, wp, ws, wo)


FFN_TILE = D_FF // 2
FFN_ROWS = 512


def _ffn_kernel(x_ref, hx_ref, sa_ref, sg_ref, sh_ref, sc_ref, g2_ref, gn_ref, wu_ref, cwa_ref, cwg_ref, cba_ref,
                cbg_ref, wd_ref, o_ref, ta_ref, tg_ref, xin_ref, ea_ref, eg_ref, *, tb, tv_last):
    i = pl.program_id(1)
    first = i == 0
    last = i == pl.num_programs(1) - 1
    xin_ref[0:CONV_HALO] = hx_ref[0]
    xin_ref[CONV_HALO:CONV_HALO + tb] = x_ref[0]
    xin = xin_ref[...]
    y = xin * lax.rsqrt(jnp.mean(xin * xin, axis=-1, keepdims=True) + EPS) * gn_ref[...]
    h = (y * (1.0 + sc_ref[0]) + sh_ref[0]).astype(BF16)
    acc = jnp.zeros((tb, x_ref.shape[2]), F32)
    for f in range(D_FF // FFN_TILE):
        cols = slice(f * FFN_TILE, (f + 1) * FFN_TILE)

        def conv(e_ref, s_ref, t_ref, cw_ref, cb_ref, col0):
            u = jnp.dot(h, wu_ref[:, col0 + f * FFN_TILE:col0 + (f + 1) * FFN_TILE], preferred_element_type=F32)
            e_ref[...] = u
            @pl.when(first)
            def _():
                e_ref[0:CONV_HALO] = s_ref[0, :, cols]

            @pl.when(last)
            def _():
                t_ref[0, :, cols] = e_ref[tv_last:tv_last + CONV_HALO]
            out = cb_ref[:, cols]
            for j in range(FFN_CONV):
                off = CONV_HALO - (FFN_CONV - 1) + j
                out = out + e_ref[off:off + tb] * cw_ref[j:j + 1, cols]
            return out

        a = conv(ea_ref, sa_ref, ta_ref, cwa_ref, cba_ref, 0)
        g = conv(eg_ref, sg_ref, tg_ref, cwg_ref, cbg_ref, D_FF)
        acc = acc + jnp.dot((_silu(g) * a).astype(BF16), wd_ref[cols, :], preferred_element_type=F32)
    o_ref[0] = x_ref[0] + g2_ref[0] * acc


def _ffn(x, sa, sg, sh2, sc2, g2, gn, wu, cwa, cwg, cba, cbg, wd, *, tb, t_valid):
    b, t, d = x.shape
    hb = tb // CONV_HALO
    blk = pl.BlockSpec((1, tb, d), lambda bi, i: (bi, i, 0))
    halo = pl.BlockSpec((1, CONV_HALO, d), lambda bi, i: (bi, jnp.maximum(i * hb - 1, 0), 0))
    st = pl.BlockSpec((1, CONV_HALO, D_FF), lambda bi, i: (bi, 0, 0))
    seqv = pl.BlockSpec((1, 1, d), lambda bi, i: (bi, 0, 0))
    cst = lambda r, c: pl.BlockSpec((r, c), lambda bi, i: (0, 0))
    tv_last = t_valid - (t // tb - 1) * tb
    return pl.pallas_call(
        functools.partial(_ffn_kernel, tb=tb, tv_last=tv_last),
        name="conv_ffn",
        out_shape=[jax.ShapeDtypeStruct((b, t, d), F32), jax.ShapeDtypeStruct((b, CONV_HALO, D_FF), F32),
                   jax.ShapeDtypeStruct((b, CONV_HALO, D_FF), F32)],
        grid=(b, t // tb),
        in_specs=[blk, halo, st, st, seqv, seqv, seqv, cst(1, d), cst(d, 2 * D_FF), cst(FFN_CONV, D_FF),
                  cst(FFN_CONV, D_FF), cst(1, D_FF), cst(1, D_FF), cst(D_FF, d)],
        out_specs=[blk, st, st],
        scratch_shapes=[pltpu.VMEM((CONV_HALO + tb, d), F32), pltpu.VMEM((CONV_HALO + tb, FFN_TILE), F32),
                        pltpu.VMEM((CONV_HALO + tb, FFN_TILE), F32)],
        compiler_params=_cparams("parallel", "arbitrary"),
    )(x, x, sa, sg, sh2, sc2, g2, gn, wu, cwa, cwg, cba, cbg, wd)


def _pad_rows(a, rows, front=False):
    extra = rows - a.shape[1]
    if extra == 0:
        return a
    return jnp.pad(a, ((0, 0), (extra, 0) if front else (0, extra), (0, 0)))


def _rope_tables(pos):
    half = HEAD_DIM // 2
    freqs = ROPE_THETA ** (-jnp.arange(half, dtype=jnp.float32) / half)
    ang = pos.astype(jnp.float32)[:, None] * freqs[None, :]
    cos, sin = jnp.cos(ang), jnp.sin(ang)
    return jnp.concatenate([cos, cos, cos, cos], axis=-1), jnp.concatenate([-sin, sin, -sin, sin], axis=-1)


def _layer_weights(l, w_ada, b_ada, norm1, norm2, w_in, q_norm, k_norm, pool_w, pool_scale, ssd_conv_w, ssd_conv_b,
                   ssd_dt_bias, ssd_a_log, ssd_d, ssd_norm, w_branch_attn, w_branch_pool, w_branch_ssd, w_out,
                   ffn_up, ffn_conv_w, ffn_conv_b, ffn_down):
    o = _OFFS
    win = w_in[l]
    col = lambda a, b: win[:, a:b]
    zpad = lambda n: jnp.zeros((D_MODEL, n), win.dtype)
    row = lambda v: v.reshape(1, -1).astype(F32)
    lanepad = lambda v: jnp.pad(v.astype(F32), (0, LANES - v.shape[0])).reshape(1, LANES)
    return dict(
        w_ada=w_ada[l].astype(BF16), b_ada=row(b_ada[l]), norm1=row(norm1[l]), norm2=row(norm2[l]),
        w_attn=jnp.concatenate([col(o[0], o[6]), zpad(LANES - IDX_DIM - IDX_HEADS)], axis=1).astype(BF16),
        w_pool=col(o[6], o[7]).astype(BF16),
        w_zdt=jnp.concatenate([col(o[7], o[8]), col(o[9], o[10]), zpad(LANES - SSD_HEADS)], axis=1).astype(BF16),
        w_xbc=col(o[8], o[9]).astype(BF16),
        w_gate=col(o[10], o[11]).astype(BF16),
        qn=row(jnp.tile(q_norm[l], LANES // HEAD_DIM)), kn=row(jnp.tile(k_norm[l], LANES // HEAD_DIM)),
        pool_w=pool_w[l].astype(BF16), pool_scale=row(pool_scale[l]),
        cw=ssd_conv_w[l].astype(F32), cb=row(ssd_conv_b[l]), dtb=lanepad(ssd_dt_bias[l]), alog=lanepad(ssd_a_log[l]),
        dsk=row(jnp.repeat(ssd_d[l], SSD_HEAD_DIM)), nw=row(ssd_norm[l]),
        wa=w_branch_attn[l].astype(BF16), wp=w_branch_pool[l].astype(BF16), ws=w_branch_ssd[l].astype(BF16),
        wo=w_out[l].astype(BF16),
        w_up=ffn_up[l].astype(BF16),
        cwa=ffn_conv_w[l][:, :D_FF].astype(F32), cwg=ffn_conv_w[l][:, D_FF:].astype(F32),
        cba=row(ffn_conv_b[l][:D_FF]), cbg=row(ffn_conv_b[l][D_FF:]), wd=ffn_down[l].astype(BF16))


def _group_layer(x, mod, lw, consts, *, pos0, past, states):
    b, t, d = x.shape
    n = b * t
    tp = -(-t // SEQ_BLOCK) * SEQ_BLOCK
    per_seq = t % SEQ_BLOCK == 0
    tm = min(ROW_TILE, t) if per_seq else n
    sh1, sc1, g1, sh2, sc2, g2 = [mod[:, i * d:(i + 1) * d] for i in range(6)]
    pos = pos0 + jnp.arange(t)
    cos_t, sin_t = _rope_tables(pos)
    if per_seq:
        view = lambda m: m.reshape(b, 1, d)
        mv_idx = lambda i: i // (t // tm)
        tab = lambda a: a.reshape(t // tm, tm, LANES)
        tab_idx = lambda i: i % (t // tm)
    else:
        view = lambda m: jnp.repeat(m, t, axis=0).reshape(1, n, d)
        mv_idx = lambda i: 0
        tab = lambda a: jnp.tile(a, (b, 1)).reshape(1, n, LANES)
        tab_idx = lambda i: 0
    xf = x.reshape(n, d)
    nm = lambda w: _norm_matmul(xf, view(sh1), view(sc1), mv_idx, lw['norm1'], w, tm)
    pa, pool_u, zdt, xbc, gates = nm(lw['w_attn']), nm(lw['w_pool']), nm(lw['w_zdt']), nm(lw['w_xbc']), nm(lw['w_gate'])

    q_r, k_r, qi_r, ki2, wi_s = _attn_prep(pa, tab(cos_t), tab(sin_t), tab_idx, lw['qn'], lw['kn'], consts['seg'], tm)
    v_new = pa[:, D_ATTN + D_KV:D_ATTN + 2 * D_KV]
    seq3 = lambda a: a.reshape(b, t, a.shape[-1])
    if past is None:
        tq, tk, kw = ATTN_Q_ROWS, min(ATTN_KEYS, t), LANES
        k_all, v_all, ki_all = seq3(k_r).astype(BF16), seq3(v_new).astype(BF16), seq3(ki2).astype(BF16)
        s_valid = t
        qpad = lambda a: seq3(a)
    else:
        tq, kw = 2 * SUBLANES, LANES
        page_table, pool_off, ck, cv, ci = past
        k_all, v_all, ki_all = _page_gather(page_table, pool_off, ck, cv, ci, seq3(k_r), seq3(v_new), seq3(ki2))
        tk = min(ATTN_KEYS_DECODE, k_all.shape[1] // 2)
        s_valid = pos0 + t
        qpad = lambda a: _pad_rows(seq3(a), tq)
    assert k_all.shape[1] % tk == 0
    kk = min(INDEX_TOPK, s_valid // 4)
    o_attn = _attention(qpad(q_r), qpad(qi_r), qpad(wi_s), k_all, v_all, ki_all,
                        tq=tq, tk=tk, kw=kw, pos0=pos0, s_valid=s_valid, kk=kk)[:, :t].reshape(n, D_ATTN)

    st_pool, st_sconv, st_ssd, st_fconv = states
    seqp = lambda a: _pad_rows(seq3(a), tp)
    o_pool = _pool(seqp(pool_u), _pad_rows(st_pool, POOL_HALO, front=True), lw['pool_w'], lw['pool_scale'], pos0=pos0)
    o_ssd, ssd_state = _ssd(seqp(zdt), seqp(xbc), _pad_rows(st_sconv, CONV_HALO, front=True),
                            st_ssd.reshape(b, D_SSD, SSD_STATE), lw['cw'], lw['cb'], lw['dtb'], lw['alog'], lw['dsk'],
                            lw['nw'], consts['ex'], consts['ext_t'], t_valid=t)
    unpad = lambda a: a[:, :t].reshape(n, a.shape[-1])
    x1 = _merge(xf, o_attn, unpad(o_pool), unpad(o_ssd), gates, view(g1), mv_idx,
                lw['wa'], lw['wp'], lw['ws'], lw['wo'], tm)

    st_f = _pad_rows(st_fconv, CONV_HALO, front=True)
    per_b = lambda m: m.reshape(b, 1, d)
    x2, tail_a, tail_g = _ffn(seqp(x1), st_f[:, :, :D_FF], st_f[:, :, D_FF:], per_b(sh2), per_b(sc2), per_b(g2),
                              lw['norm2'], lw['w_up'], lw['cwa'], lw['cwg'], lw['cba'], lw['cbg'], lw['wd'],
                              tb=min(FFN_ROWS, tp), t_valid=t)
    x2 = x2[:, :t]
    fconv_state = jnp.concatenate([tail_a, tail_g], axis=-1)[:, -(FFN_CONV - 1):]

    def tail(prev, news, keep):
        last = jnp.concatenate([seq3(a)[:, -min(keep, t):] for a in news], axis=-1)
        return jnp.concatenate([prev, last], axis=1)[:, -keep:]

    new_states = (seq3(k_r).reshape(b, t, N_KV_HEADS, HEAD_DIM), seq3(v_new).reshape(b, t, N_KV_HEADS, HEAD_DIM),
                  seq3(ki2)[:, :, :IDX_DIM], tail(st_pool, [pool_u], POOL_CTX), tail(st_sconv, [xbc], SSD_CONV - 1),
                  ssd_state.reshape(b, SSD_HEADS, SSD_HEAD_DIM, SSD_STATE), fconv_state)
    return x2, new_states


def kernel(x_prompt, x_sample, c_prompt, c_sample, cache_k, cache_v, cache_kidx, page_table, state_pool, state_ssd_conv, state_ssd, state_ffn_conv, w_ada, b_ada, norm1, norm2, w_in, q_norm, k_norm, pool_w, pool_scale, ssd_conv_w, ssd_conv_b, ssd_dt_bias, ssd_a_log, ssd_d, ssd_norm, w_branch_attn, w_branch_pool, w_branch_ssd, w_out, ffn_up, ffn_conv_w, ffn_conv_b, ffn_down):
    bp, tp, d = x_prompt.shape
    bs, ts, _ = x_sample.shape
    depth = w_ada.shape[0]
    n_pool, page = cache_k.shape[1], cache_k.shape[2]
    past_len = page_table.shape[1] * page
    dt = x_prompt.dtype

    r = lax.broadcasted_iota(jnp.int32, (LANES, LANES), 0)
    c = lax.broadcasted_iota(jnp.int32, (LANES, LANES), 1)
    hr = lax.broadcasted_iota(jnp.int32, (LANES, D_SSD), 0)
    hc = lax.broadcasted_iota(jnp.int32, (LANES, D_SSD), 1)
    ex = (hr == hc // SSD_HEAD_DIM).astype(BF16)
    consts = dict(seg=(r // HEAD_DIM == c // HEAD_DIM).astype(BF16), ex=ex, ext_t=ex.T)

    c_all = jnp.concatenate([c_prompt, c_sample], axis=0)
    rows = -(-c_all.shape[0] // SUBLANES) * SUBLANES
    c_pad = jnp.pad(c_all, ((0, rows - c_all.shape[0]), (0, 0)))

    ck_t = jnp.transpose(cache_k, (0, 1, 3, 4, 2)).reshape(depth * n_pool, D_KV, page)
    cv_t = jnp.transpose(cache_v, (0, 1, 3, 4, 2)).reshape(depth * n_pool, D_KV, page)
    ci_t = jnp.transpose(cache_kidx, (0, 1, 3, 2)).reshape(depth * n_pool, IDX_DIM, page)

    zeros = lambda *s: jnp.zeros(s, dt)
    xp, xs = x_prompt, x_sample
    new_p, new_s = [[] for _ in range(7)], [[] for _ in range(7)]
    for l in range(depth):
        lw = _layer_weights(l, w_ada, b_ada, norm1, norm2, w_in, q_norm, k_norm, pool_w, pool_scale, ssd_conv_w,
                            ssd_conv_b, ssd_dt_bias, ssd_a_log, ssd_d, ssd_norm, w_branch_attn, w_branch_pool,
                            w_branch_ssd, w_out, ffn_up, ffn_conv_w, ffn_conv_b, ffn_down)
        mod = _mod(c_pad, lw['w_ada'], lw['b_ada'])
        xp, st_p = _group_layer(
            xp, mod[:bp], lw, consts, pos0=0, past=None,
            states=(zeros(bp, POOL_CTX, D_POOL), zeros(bp, SSD_CONV - 1, D_SSD_CONV),
                    zeros(bp, SSD_HEADS, SSD_HEAD_DIM, SSD_STATE), zeros(bp, FFN_CONV - 1, 2 * D_FF)))
        past = (page_table, l * n_pool, ck_t, cv_t, ci_t)
        xs, st_s = _group_layer(xs, mod[bp:bp + bs], lw, consts, pos0=past_len, past=past,
                                states=(state_pool[l], state_ssd_conv[l], state_ssd[l], state_ffn_conv[l]))
        for j in range(7):
            new_p[j].append(st_p[j])
            new_s[j].append(st_s[j])
    outs_p = [jnp.stack(s) for s in new_p]
    outs_s = [jnp.stack(s) for s in new_s]
    return (xp, xs, *outs_p, *outs_s)
```

```python
import functools

import numpy as np
import jax
import jax.numpy as jnp
from jax import lax
from jax.experimental import pallas as pl
from jax.experimental.pallas import tpu as pltpu

D_MODEL = 1024
N_HEADS = 16
N_KV_HEADS = 4
HEAD_DIM = 64
D_ATTN = N_HEADS * HEAD_DIM
D_KV = N_KV_HEADS * HEAD_DIM
IDX_HEADS = 8
IDX_DIM = 64
INDEX_TOPK = 256
ROPE_THETA = 10000.0
POOL_WINDOWS = (2, 4, 8, 16)
D_POOL = D_MODEL
POOL_GROUP = D_POOL // len(POOL_WINDOWS)
POOL_CTX = max(POOL_WINDOWS) - 1
D_SSD = 2 * D_MODEL
SSD_HEAD_DIM = 64
SSD_HEADS = D_SSD // SSD_HEAD_DIM
SSD_GROUPS = 4
SSD_STATE = 128
SSD_CONV = 4
D_SSD_CONV = D_SSD + 2 * SSD_GROUPS * SSD_STATE
D_FF = 2816
FFN_CONV = 3
N_BRANCH = 3
EPS = 1e-6
_SPLITS = (D_ATTN, D_KV, D_KV, IDX_HEADS * IDX_DIM, IDX_DIM, IDX_HEADS, D_POOL, D_SSD, D_SSD_CONV,
           SSD_HEADS, N_BRANCH * D_MODEL)
_OFFS = tuple(int(v) for v in np.cumsum((0,) + _SPLITS))

F32 = jnp.float32
BF16 = jnp.bfloat16
LANES = 128
SUBLANES = 8
SEQ_BLOCK = 128
ROW_TILE = 512
ATTN_Q_ROWS = 256
ATTN_KEYS = 256
ATTN_KEYS_DECODE = 1024
VMEM_LIMIT = 56 * 1024 * 1024
NEG = -1e30
LOG2E = 1.4426950408889634
N_BISECT = 20
NT = (((1,), (1,)), ((), ()))
TN = (((0,), (0,)), ((), ()))


def _cparams(*sem):
    return pltpu.CompilerParams(dimension_semantics=sem, vmem_limit_bytes=VMEM_LIMIT)


def _silu(v):
    return (0.5 * v) * (1.0 + jnp.tanh(0.5 * v))


def _split3(v):
    h1 = v.astype(BF16)
    r1 = v - h1.astype(F32)
    h2 = r1.astype(BF16)
    h3 = (r1 - h2.astype(F32)).astype(BF16)
    return h1, h2, h3


def _dot3(v, w):
    h1, h2, h3 = _split3(v)
    return (jnp.dot(h1, w, preferred_element_type=F32) + jnp.dot(h2, w, preferred_element_type=F32)
            + jnp.dot(h3, w, preferred_element_type=F32))


def _mod_kernel(c_ref, w_ref, b_ref, o_ref):
    s = _silu(c_ref[...])
    o_ref[...] = jnp.dot(s.astype(BF16), w_ref[...], preferred_element_type=F32) + b_ref[...]


def _mod(c_pad, w, b):
    rows, d = c_pad.shape
    n = w.shape[1]
    tn = n // 4
    return pl.pallas_call(
        _mod_kernel,
        name="adaln_mod",
        out_shape=jax.ShapeDtypeStruct((rows, n), F32),
        grid=(n // tn,),
        in_specs=[pl.BlockSpec((rows, d), lambda j: (0, 0)),
                  pl.BlockSpec((d, tn), lambda j: (0, j)),
                  pl.BlockSpec((1, tn), lambda j: (0, j))],
        out_specs=pl.BlockSpec((rows, tn), lambda j: (0, j)),
        compiler_params=_cparams("parallel"),
    )(c_pad, w, b)


def _norm_matmul_kernel(x_ref, sh_ref, sc_ref, g_ref, w_ref, o_ref):
    x = x_ref[...]
    y = x * lax.rsqrt(jnp.mean(x * x, axis=-1, keepdims=True) + EPS) * g_ref[...]
    h = y * (1.0 + sc_ref[0]) + sh_ref[0]
    o_ref[...] = jnp.dot(h.astype(BF16), w_ref[...], preferred_element_type=F32).astype(o_ref.dtype)


def _norm_matmul(x, mv_sh, mv_sc, mv_idx, g, w, tm):
    n, d = x.shape
    nout = w.shape[1]
    rb = mv_sh.shape[1]
    return pl.pallas_call(
        _norm_matmul_kernel,
        name="norm_matmul",
        out_shape=jax.ShapeDtypeStruct((n, nout), F32),
        grid=(n // tm,),
        in_specs=[pl.BlockSpec((tm, d), lambda i: (i, 0)),
                  pl.BlockSpec((1, rb, d), lambda i: (mv_idx(i), 0, 0)),
                  pl.BlockSpec((1, rb, d), lambda i: (mv_idx(i), 0, 0)),
                  pl.BlockSpec((1, d), lambda i: (0, 0)),
                  pl.BlockSpec((d, nout), lambda i: (0, 0))],
        out_specs=pl.BlockSpec((tm, nout), lambda i: (i, 0)),
        compiler_params=_cparams("parallel"),
    )(x, mv_sh, mv_sc, g, w)


def _attn_prep_kernel(q_ref, k_ref, qi_ref, kw_ref, cos_ref, sin_ref, qn_ref, kn_ref, seg_ref,
                      qo_ref, ko_ref, qio_ref, kio_ref, wio_ref):
    tm = q_ref.shape[0]
    lane = lax.broadcasted_iota(jnp.int32, (tm, LANES), 1)
    first_half = (lane % HEAD_DIM) < (HEAD_DIM // 2)
    cos = cos_ref[0]
    sin = sin_ref[0]
    seg = seg_ref[...]

    def rope(v):
        up = pltpu.roll(v, LANES - HEAD_DIM // 2, 1)
        dn = pltpu.roll(v, HEAD_DIM // 2, 1)
        return v * cos + jnp.where(first_half, up, dn) * sin

    def headnorm(v, gain):
        sq = v * v
        hi = sq.astype(BF16)
        lo = (sq - hi.astype(F32)).astype(BF16)
        ss = jnp.dot(hi, seg, preferred_element_type=F32) + jnp.dot(lo, seg, preferred_element_type=F32)
        return v * lax.rsqrt(ss * (1.0 / HEAD_DIM) + EPS) * gain

    qn = qn_ref[...]
    kn = kn_ref[...]
    for j in range(D_ATTN // LANES):
        sl = slice(j * LANES, (j + 1) * LANES)
        qo_ref[:, sl] = rope(headnorm(q_ref[:, sl], qn)) * (HEAD_DIM ** -0.5 * LOG2E)
    for j in range(D_KV // LANES):
        sl = slice(j * LANES, (j + 1) * LANES)
        ko_ref[:, sl] = rope(headnorm(k_ref[:, sl], kn))
    for j in range(IDX_HEADS * IDX_DIM // LANES):
        sl = slice(j * LANES, (j + 1) * LANES)
        qio_ref[:, sl] = rope(qi_ref[:, sl])
    kw = kw_ref[...]
    kr = rope(kw)
    kio_ref[...] = jnp.where(lane < IDX_DIM, kr, pltpu.roll(kr, IDX_DIM, 1))
    wio_ref[...] = pltpu.roll(kw, LANES - IDX_DIM, 1) * (IDX_HEADS ** -0.5 * IDX_DIM ** -0.5)


def _attn_prep(pa, cos_t, sin_t, tab_idx, qn, kn, seg, tm):
    n = pa.shape[0]
    blk = lambda w, c: pl.BlockSpec((tm, w), lambda i: (i, c))
    tab = pl.BlockSpec((1, tm, LANES), lambda i: (tab_idx(i), 0, 0))
    cst = lambda r, c: pl.BlockSpec((r, c), lambda i: (0, 0))
    return pl.pallas_call(
        _attn_prep_kernel,
        name="attn_prep",
        out_shape=[jax.ShapeDtypeStruct((n, D_ATTN), F32), jax.ShapeDtypeStruct((n, D_KV), F32),
                   jax.ShapeDtypeStruct((n, IDX_HEADS * IDX_DIM), F32),
                   jax.ShapeDtypeStruct((n, LANES), F32), jax.ShapeDtypeStruct((n, LANES), F32)],
        grid=(n // tm,),
        in_specs=[blk(D_ATTN, 0), blk(D_KV, D_ATTN // D_KV), blk(IDX_HEADS * IDX_DIM, 3), blk(LANES, 16),
                  tab, tab, cst(1, LANES), cst(1, LANES), cst(LANES, LANES)],
        out_specs=[blk(D_ATTN, 0), blk(D_KV, 0), blk(IDX_HEADS * IDX_DIM, 0), blk(LANES, 0), blk(LANES, 0)],
        compiler_params=_cparams("parallel"),
    )(pa, pa, pa, pa, cos_t, sin_t, qn, kn, seg)


GATHER_PAGES = 16


def _page_gather_kernel(pt_ref, *refs, n_steps, pg, page):
    ins, outs = refs[:3 * pg + 3], refs[3 * pg + 3:]
    past = pl.program_id(1) < n_steps
    for t in range(3):
        new_ref, out_ref = ins[3 * pg + t], outs[t]
        for i in range(pg):
            rows = slice(i * page, (i + 1) * page)
            tile = ins[t * pg + i][0]
            if t == 2:
                tile = jnp.concatenate([tile, tile], axis=0)
            out_ref[0, rows, :] = jnp.where(past, tile.T, new_ref[0, rows, :]).astype(out_ref.dtype)


def _page_gather(page_table, pool_off, ck, cv, ci, nk, nv, ni):
    bd, n_pages = page_table.shape
    page = ck.shape[2]
    pg = max(g for g in range(1, GATHER_PAGES + 1) if n_pages % g == 0)
    n_steps = n_pages // pg
    nk, nv, ni = [_pad_rows(a, pg * page) for a in (nk, nv, ni)]

    def cache(w, i):
        return pl.BlockSpec((1, w, page),
                            lambda b, p, pt: (pool_off + pt[b, jnp.minimum(p, n_steps - 1) * pg + i], 0, 0))

    new = lambda w: pl.BlockSpec((1, pg * page, w), lambda b, p, pt: (b, 0, 0))
    out = lambda w: pl.BlockSpec((1, pg * page, w), lambda b, p, pt: (b, p, 0))
    s_pad = (n_steps + 1) * pg * page
    in_specs = ([cache(D_KV, i) for i in range(pg)] + [cache(D_KV, i) for i in range(pg)]
                + [cache(IDX_DIM, i) for i in range(pg)] + [new(D_KV), new(D_KV), new(2 * IDX_DIM)])
    return pl.pallas_call(
        functools.partial(_page_gather_kernel, n_steps=n_steps, pg=pg, page=page),
        out_shape=[jax.ShapeDtypeStruct((bd, s_pad, D_KV), BF16), jax.ShapeDtypeStruct((bd, s_pad, D_KV), BF16),
                   jax.ShapeDtypeStruct((bd, s_pad, 2 * IDX_DIM), BF16)],
        grid_spec=pltpu.PrefetchScalarGridSpec(
            num_scalar_prefetch=1, grid=(bd, n_steps + 1), in_specs=in_specs,
            out_specs=[out(D_KV), out(D_KV), out(2 * IDX_DIM)]),
        compiler_params=_cparams("parallel", "parallel"),
        name="page_gather",
    )(page_table, *([ck] * pg), *([cv] * pg), *([ci] * pg), nk, nv, ni)


GQA = N_HEADS // N_KV_HEADS


def _attn_kernel(q_ref, qi_ref, wi_ref, k_ref, v_ref, ki_ref, o_ref,
                 sc_ref, qh_ref, qim_ref, wb_ref, m_ref, l_ref, acc_ref, *, tq, tk, kw, nblk, n_static, pos0, s_valid, kk):
    q0 = pl.program_id(1) * tq
    nch = tk // LANES
    ngrp = D_KV // kw
    hpg = N_HEADS // ngrp
    lane = lax.broadcasted_iota(jnp.int32, (tq, LANES), 1)
    row = lax.broadcasted_iota(jnp.int32, (tq, LANES), 0)
    qpos = pos0 + q0 + row
    lo_half = lane < HEAD_DIM
    n_kv = jnp.minimum(nblk, (pos0 + q0 + tq - 1) // tk + 1)
    kkf = float(kk)
    chunk = lambda a, j: a[..., j * LANES:(j + 1) * LANES]
    rg = min(tq, LANES)

    def kv_loop(body, init):
        if n_static is None:
            return lax.fori_loop(0, n_kv, body, init)
        for c in range(n_static):
            init = body(c, init)
        return init

    def key_off(c):
        return c * tk if n_static is not None else pl.multiple_of(c * tk, tk)

    def bcast(col):
        return jnp.broadcast_to(col, col.shape[:-1] + (LANES,))

    wi = wi_ref[0]
    for j in range(IDX_HEADS // 2):
        t = qi_ref[0, :, j * LANES:(j + 1) * LANES]
        qim_ref[2 * j * tq:(2 * j + 1) * tq] = jnp.where(lo_half, t, 0.0).astype(BF16)
        qim_ref[(2 * j + 1) * tq:(2 * j + 2) * tq] = jnp.where(lo_half, 0.0, t).astype(BF16)
    for h in range(IDX_HEADS):
        wb_ref[h] = bcast(wi[:, h:h + 1])
    def kv_tile(h):
        kvh = h // GQA
        return (kvh * HEAD_DIM % kw) // LANES, kvh % 2

    if kw > LANES:
        qh_ref[...] = jnp.zeros(qh_ref.shape, BF16)
    for h in range(N_HEADS):
        t = q_ref[0, :, (h // 2) * LANES:(h // 2 + 1) * LANES]
        tile, kv_half = kv_tile(h)
        if h % 2 != kv_half:
            t = pltpu.roll(t, HEAD_DIM, 1)
        keep = lo_half if kv_half == 0 else jnp.logical_not(lo_half)
        hh = h % hpg
        qh_ref[h // hpg, hh * tq:(hh + 1) * tq, tile * LANES:(tile + 1) * LANES] = jnp.where(keep, t, 0.0).astype(BF16)
    m_ref[...] = jnp.full(m_ref.shape, NEG, F32)
    l_ref[...] = jnp.zeros(l_ref.shape, F32)
    acc_ref[...] = jnp.zeros(acc_ref.shape, F32)

    def p1(c, carry):
        kid = ki_ref[0, pl.ds(key_off(c), tk), :]
        y = lax.dot_general(qim_ref[...], kid, NT, preferred_element_type=F32)
        y = jnp.maximum(y, 0.0).reshape(IDX_HEADS, tq, tk)
        for j in range(nch):
            acc = wb_ref[0] * chunk(y[0], j)
            for h in range(1, IDX_HEADS):
                acc = acc + wb_ref[h] * chunk(y[h], j)
            kpos = c * tk + j * LANES + lane
            vis = jnp.logical_and(kpos <= qpos, kpos < s_valid)
            sc_ref[c, :, j * LANES:(j + 1) * LANES] = jnp.where(vis, acc, -jnp.inf)
        return carry

    kv_loop(p1, 0)

    def row_pass(step, init, reduce):
        outs = []
        for r in range(tq // rg):
            rows = slice(r * rg, (r + 1) * rg)

            def body(c, acc, rows=rows):
                x = sc_ref[c, rows, :]
                for j in range(nch):
                    acc = step(acc, chunk(x, j), rows)
                return acc
            outs.append(bcast(reduce(kv_loop(body, jnp.full((rg, LANES), init, F32)), axis=1, keepdims=True)))
        return outs[0] if len(outs) == 1 else jnp.concatenate(outs, axis=0)

    def count_gt(t_b):
        return row_pass(lambda acc, xj, rows: acc + jnp.where(xj > t_b[rows], 1.0, 0.0), 0.0, jnp.sum)

    def next_above(t_b):
        return row_pass(lambda acc, xj, rows: jnp.minimum(acc, jnp.where(xj > t_b[rows], xj, jnp.inf)), jnp.inf, jnp.min)

    def stat(c, carry):
        mn, mx = carry
        x = sc_ref[c]
        for j in range(nch):
            xj = chunk(x, j)
            mn = jnp.minimum(mn, jnp.where(xj == -jnp.inf, jnp.inf, xj))
            mx = jnp.maximum(mx, xj)
        return mn, mx

    mn, mx = kv_loop(stat, (jnp.full((tq, LANES), jnp.inf, F32), jnp.full((tq, LANES), -jnp.inf, F32)))
    rmin = bcast(jnp.min(mn, axis=1, keepdims=True))
    rmax = bcast(jnp.max(mx, axis=1, keepdims=True))
    c_min = count_gt(rmin)
    at_min = c_min < kkf

    def bisect(i, carry):
        lo, hi = carry
        mid = lo + 0.5 * (hi - lo)
        up = count_gt(mid) >= kkf
        return jnp.where(up, mid, lo), jnp.where(up, hi, mid)

    lo, hi = lax.fori_loop(0, N_BISECT, bisect, (rmin, rmax))
    tau = next_above(lo)
    ct = count_gt(tau)

    def pending(ct_):
        return jnp.logical_and(ct_ >= kkf, jnp.logical_not(at_min))

    def w_cond(st):
        return jnp.max(jnp.where(pending(st[2]), 1.0, 0.0)) > 0.5

    def w_body(st):
        lo_, tau_, ct_ = st
        need = pending(ct_)
        lo2 = jnp.where(need, tau_, lo_)
        tau2 = next_above(lo2)
        ct2 = count_gt(tau2)
        return lo2, jnp.where(need, tau2, tau_), jnp.where(need, ct2, ct_)

    lo, tau, ct = lax.while_loop(w_cond, w_body, (lo, tau, ct))
    tau = jnp.where(at_min, rmin, tau)
    m_need = kkf - jnp.where(at_min, c_min, ct)

    rk = lax.broadcasted_iota(jnp.int32, (tk, tk), 0)
    ck = lax.broadcasted_iota(jnp.int32, (tk, tk), 1)
    before = jnp.where(rk < ck, 1.0, 0.0).astype(BF16)

    def p2(c, cum):
        x = sc_ref[c]
        eqs = [chunk(x, j) == tau for j in range(nch)]
        eqf = [jnp.where(e, 1.0, 0.0) for e in eqs]
        pre = jnp.dot(jnp.concatenate(eqf, axis=-1).astype(BF16), before, preferred_element_type=F32)
        bias = [jnp.where(chunk(x, j) > tau, 0.0,
                          jnp.where(eqs[j], jnp.where(chunk(pre, j) + cum < m_need, 0.0, NEG), NEG))
                for j in range(nch)]
        tot = eqf[0]
        for j in range(1, nch):
            tot = tot + eqf[j]
        cum = cum + bcast(jnp.sum(tot, axis=1, keepdims=True))
        koff = key_off(c)
        for p in range(ngrp):
            kp = k_ref[0, pl.ds(koff, tk), p * kw:(p + 1) * kw]
            vp = v_ref[0, pl.ds(koff, tk), p * kw:(p + 1) * kw]
            s = lax.dot_general(qh_ref[p], kp, NT, preferred_element_type=F32).reshape(hpg, tq, tk)
            sm = [chunk(s, j) + bias[j][None] for j in range(nch)]
            mx_ = sm[0]
            for j in range(1, nch):
                mx_ = jnp.maximum(mx_, sm[j])
            m_old = m_ref[p]
            m_new = jnp.maximum(m_old, bcast(jnp.max(mx_, axis=-1, keepdims=True)))
            pe = [jnp.exp2(sm[j] - m_new) for j in range(nch)]
            ps = pe[0]
            for j in range(1, nch):
                ps = ps + pe[j]
            alpha = jnp.exp2(m_old - m_new)
            l_ref[p] = alpha * l_ref[p] + bcast(jnp.sum(ps, axis=-1, keepdims=True))
            pv = jnp.dot(jnp.concatenate(pe, axis=-1).reshape(hpg * tq, tk).astype(BF16), vp,
                         preferred_element_type=F32)
            alpha = alpha.reshape(hpg * tq, LANES)
            acc_ref[p] = jnp.concatenate([alpha] * (kw // LANES), axis=-1) * acc_ref[p] + pv
            m_ref[p] = m_new
        return cum

    kv_loop(p2, jnp.zeros((tq, LANES), F32))

    def head_out(h):
        p, hh = h // hpg, h % hpg
        tile, kv_half = kv_tile(h)
        r = acc_ref[p, hh * tq:(hh + 1) * tq, tile * LANES:(tile + 1) * LANES] / l_ref[p, hh]
        if kv_half != h % 2:
            r = pltpu.roll(r, HEAD_DIM, 1)
        return r

    for j in range(N_HEADS // 2):
        o_ref[0, :, j * LANES:(j + 1) * LANES] = jnp.where(lo_half, head_out(2 * j), head_out(2 * j + 1)).astype(o_ref.dtype)


def _attention(q, qi, wi, k_all, v_all, ki2, *, tq, tk, kw, pos0, s_valid, kk):
    b, tq_pad, _ = q.shape
    s_pad = k_all.shape[1]
    nblk = s_pad // tk
    ngrp = D_KV // kw
    hpg = N_HEADS // ngrp
    n_static = min(nblk, (pos0 + tq - 1) // tk + 1) if tq_pad == tq else None
    qblk = lambda w: pl.BlockSpec((1, tq, w), lambda bi, qi_: (bi, qi_, 0))
    seq = lambda w: pl.BlockSpec((1, s_pad, w), lambda bi, qi_: (bi, 0, 0))
    return pl.pallas_call(
        functools.partial(_attn_kernel, tq=tq, tk=tk, kw=kw, nblk=nblk, n_static=n_static, pos0=pos0, s_valid=s_valid,
                          kk=kk),
        out_shape=jax.ShapeDtypeStruct((b, tq_pad, D_ATTN), BF16),
        grid=(b, tq_pad // tq),
        in_specs=[qblk(D_ATTN), qblk(IDX_HEADS * IDX_DIM), qblk(LANES), seq(D_KV), seq(D_KV), seq(LANES)],
        out_specs=qblk(D_ATTN),
        scratch_shapes=[pltpu.VMEM((nblk, tq, tk), F32), pltpu.VMEM((ngrp, hpg * tq, kw), BF16),
                        pltpu.VMEM((IDX_HEADS * tq, LANES), BF16), pltpu.VMEM((IDX_HEADS, tq, LANES), F32),
                        pltpu.VMEM((ngrp, hpg, tq, LANES), F32), pltpu.VMEM((ngrp, hpg, tq, LANES), F32),
                        pltpu.VMEM((ngrp, hpg * tq, kw), F32)],
        compiler_params=_cparams("parallel", "parallel"),
        name="sparse_attention",
    )(q, qi, wi, k_all, v_all, ki2)


POOL_HALO = 16


def _pool_kernel(u_ref, halo_ref, st_ref, w_ref, scale_ref, o_ref, ext_ref, *, tb, pos0):
    i = pl.program_id(1)
    ext_ref[0:POOL_HALO] = jnp.where(i == 0, st_ref[0], halo_ref[0])
    ext_ref[POOL_HALO:POOL_HALO + tb] = u_ref[0]
    pos = pos0 + i * tb + lax.broadcasted_iota(jnp.int32, (tb, 1), 0)
    for gi, w in enumerate(POOL_WINDOWS):
        sl = slice(gi * POOL_GROUP, (gi + 1) * POOL_GROUP)
        tot = ext_ref[POOL_HALO:POOL_HALO + tb, sl]
        for k in range(1, w):
            tot = tot + ext_ref[POOL_HALO - k:POOL_HALO - k + tb, sl]
        cnt = jnp.minimum(w, pos + 1).astype(F32)
        pooled = tot / cnt - u_ref[0, :, sl]
        mixed = jnp.dot(pooled.astype(BF16), w_ref[gi], preferred_element_type=F32) * scale_ref[:, sl]
        o_ref[0, :, sl] = mixed.astype(o_ref.dtype)


def _pool(u, state16, w, scale, *, pos0):
    b, t, c = u.shape
    tb = SEQ_BLOCK
    hb = tb // POOL_HALO
    return pl.pallas_call(
        functools.partial(_pool_kernel, tb=tb, pos0=pos0),
        name="pool_mix",
        out_shape=jax.ShapeDtypeStruct((b, t, c), BF16),
        grid=(b, t // tb),
        in_specs=[pl.BlockSpec((1, tb, c), lambda bi, i: (bi, i, 0)),
                  pl.BlockSpec((1, POOL_HALO, c), lambda bi, i: (bi, jnp.maximum(i * hb - 1, 0), 0)),
                  pl.BlockSpec((1, POOL_HALO, c), lambda bi, i: (bi, 0, 0)),
                  pl.BlockSpec((len(POOL_WINDOWS), POOL_GROUP, POOL_GROUP), lambda bi, i: (0, 0, 0)),
                  pl.BlockSpec((1, c), lambda bi, i: (0, 0))],
        out_specs=pl.BlockSpec((1, tb, c), lambda bi, i: (bi, i, 0)),
        scratch_shapes=[pltpu.VMEM((POOL_HALO + tb, c), F32)],
        compiler_params=_cparams("parallel", "parallel"),
    )(u, u, state16, w, scale)


CONV_HALO = SUBLANES
SSD_GH = SSD_HEADS // SSD_GROUPS
SSD_GW = SSD_GH * SSD_HEAD_DIM
CONV_CHUNK = 512


def _ssd_kernel(z_ref, dt_ref, xbc_ref, halo_ref, st_ref, h0_ref, cw_ref, cb_ref, dtb_ref, alog_ref, dsk_ref,
                nw_ref, ex_ref, ext_t_ref, y_ref, hT_ref, ext_ref, xs_ref, h_ref, *, q, t_valid):
    c = pl.program_id(1)

    @pl.when(c == 0)
    def _():
        h_ref[...] = h0_ref[0]

    ext_ref[0:CONV_HALO] = jnp.where(c == 0, st_ref[0], halo_ref[0])
    ext_ref[CONV_HALO:CONV_HALO + q] = xbc_ref[0]
    for ch in range(D_SSD_CONV // CONV_CHUNK):
        cc = slice(ch * CONV_CHUNK, (ch + 1) * CONV_CHUNK)
        xc = cb_ref[:, cc]
        for j in range(SSD_CONV):
            off = CONV_HALO - (SSD_CONV - 1) + j
            xc = xc + ext_ref[off:off + q, cc] * cw_ref[j:j + 1, cc]
        xs_ref[:, cc] = _silu(xc)
    x = xs_ref[:, :D_SSD]

    lane = lax.broadcasted_iota(jnp.int32, (q, LANES), 1)
    row = lax.broadcasted_iota(jnp.int32, (q, LANES), 0)
    v = dt_ref[0] + dtb_ref[...]
    dt = jnp.maximum(v, 0.0) + jnp.log1p(jnp.exp(-jnp.abs(v)))
    dt = jnp.where(jnp.logical_and(lane < SSD_HEADS, c * q + row < t_valid), dt, 0.0)
    a = dt * (-jnp.exp(alog_ref[...]))
    tri = jnp.where(lane <= row, 1.0, 0.0).astype(BF16)
    causal = lane <= row
    cs = _dot3_left(tri, a)
    cs_t = cs.T
    ex = ex_ref[...]
    cs_last = cs[q - 1:q, :]
    dt_x = _dot3(dt, ex)
    ecs_x = jnp.exp(_dot3(cs, ex))
    dec_x = jnp.exp(_dot3(cs_last - cs, ex))
    xdt = x * dt_x
    xdec = (xdt * dec_x).astype(BF16)
    xdt16 = xdt.astype(BF16)
    rowdec = jnp.exp(_dot3_left(ext_t_ref[...], jnp.broadcast_to(cs_t[:, q - 1:q], (LANES, LANES))))
    lo_half = lane < SSD_HEAD_DIM

    for g in range(SSD_GROUPS):
        bg = xs_ref[:, D_SSD + g * SSD_STATE:D_SSD + (g + 1) * SSD_STATE].astype(BF16)
        cg = xs_ref[:, D_SSD + (SSD_GROUPS + g) * SSD_STATE:D_SSD + (SSD_GROUPS + g + 1) * SSD_STATE].astype(BF16)
        cb = lax.dot_general(cg, bg, NT, preferred_element_type=F32)
        hs = slice(g * SSD_GW, (g + 1) * SSD_GW)
        hprev = h_ref[hs, :]
        y_off = lax.dot_general(cg, hprev.astype(BF16), NT, preferred_element_type=F32) * ecs_x[:, hs]
        for jp in range(SSD_GH // 2):
            cols = slice(g * SSD_GW + jp * LANES, g * SSD_GW + (jp + 1) * LANES)
            xp = xdt16[:, cols]
            halves = []
            for e in range(2):
                hd = g * SSD_GH + 2 * jp + e
                seg = cs[:, hd:hd + 1] - cs_t[hd:hd + 1, :]
                lmat = jnp.where(causal, jnp.exp(jnp.where(causal, seg, 0.0)), 0.0)
                halves.append(jnp.dot((cb * lmat).astype(BF16), xp, preferred_element_type=F32))
            y_diag = jnp.where(lo_half, halves[0], halves[1])
            yv = y_diag + y_off[:, jp * LANES:(jp + 1) * LANES] + dsk_ref[:, cols] * xs_ref[:, cols]
            zv = z_ref[0, :, cols]
            ext_ref[0:q, cols] = yv * _silu(zv)
        st_new = lax.dot_general(xdec[:, hs], bg, TN, preferred_element_type=F32)
        h_ref[hs, :] = hprev * rowdec[hs, :] + st_new

    for g in range(SSD_GROUPS):
        hs = slice(g * SSD_GW, (g + 1) * SSD_GW)
        yg = ext_ref[0:q, hs]
        yn = yg * lax.rsqrt(jnp.mean(yg * yg, axis=-1, keepdims=True) + EPS) * nw_ref[:, hs]
        y_ref[0, :, hs] = yn.astype(y_ref.dtype)

    @pl.when(c == pl.num_programs(1) - 1)
    def _():
        hT_ref[0] = h_ref[...]


def _dot3_left(w, v):
    h1, h2, h3 = _split3(v)
    return (jnp.dot(w, h1, preferred_element_type=F32) + jnp.dot(w, h2, preferred_element_type=F32)
            + jnp.dot(w, h3, preferred_element_type=F32))


def _ssd(zdt, xbc, state8, h0, h_off, cw, cb, dtb, alog, dsk, nw, ex, ext_t, *, t_valid):
    b, t, _ = xbc.shape
    q = SEQ_BLOCK
    hb = q // CONV_HALO
    cst = lambda r, c_: pl.BlockSpec((r, c_), lambda bi, i: (0, 0))
    return pl.pallas_call(
        functools.partial(_ssd_kernel, q=q, t_valid=t_valid),
        name="ssd_scan",
        out_shape=[jax.ShapeDtypeStruct((b, t, D_SSD), BF16), jax.ShapeDtypeStruct((b, D_SSD, SSD_STATE), F32)],
        grid=(b, t // q),
        in_specs=[pl.BlockSpec((1, q, D_SSD), lambda bi, i: (bi, i, 0)),
                  pl.BlockSpec((1, q, LANES), lambda bi, i: (bi, i, D_SSD // LANES)),
                  pl.BlockSpec((1, q, D_SSD_CONV), lambda bi, i: (bi, i, 0)),
                  pl.BlockSpec((1, CONV_HALO, D_SSD_CONV), lambda bi, i: (bi, jnp.maximum(i * hb - 1, 0), 0)),
                  pl.BlockSpec((1, CONV_HALO, D_SSD_CONV), lambda bi, i: (bi, 0, 0)),
                  pl.BlockSpec((1, D_SSD, SSD_STATE), lambda bi, i: (h_off + bi, 0, 0)),
                  cst(SSD_CONV, D_SSD_CONV), cst(1, D_SSD_CONV), cst(1, LANES), cst(1, LANES), cst(1, D_SSD),
                  cst(1, D_SSD), cst(LANES, D_SSD), cst(D_SSD, LANES)],
        out_specs=[pl.BlockSpec((1, q, D_SSD), lambda bi, i: (bi, i, 0)),
                   pl.BlockSpec((1, D_SSD, SSD_STATE), lambda bi, i: (bi, 0, 0))],
        scratch_shapes=[pltpu.VMEM((CONV_HALO + q, D_SSD_CONV), F32), pltpu.VMEM((q, D_SSD_CONV), F32),
                        pltpu.VMEM((D_SSD, SSD_STATE), F32)],
        compiler_params=_cparams("parallel", "arbitrary"),
    )(zdt, zdt, xbc, xbc, state8, h0, cw, cb, dtb, alog, dsk, nw, ex, ext_t)


def _merge_kernel(x_ref, oa_ref, op_ref, os_ref, gt_ref, g1_ref, wa_ref, wp_ref, ws_ref, wo_ref, o_ref):
    def gate(bi):
        return jax.nn.sigmoid(gt_ref[:, bi * D_MODEL:(bi + 1) * D_MODEL])
    merged = (gate(0) * jnp.dot(oa_ref[...], wa_ref[...], preferred_element_type=F32)
              + gate(1) * jnp.dot(op_ref[...], wp_ref[...], preferred_element_type=F32)
              + gate(2) * jnp.dot(os_ref[...], ws_ref[...], preferred_element_type=F32))
    o_ref[...] = x_ref[...] + g1_ref[0] * jnp.dot(merged.astype(BF16), wo_ref[...], preferred_element_type=F32)


def _merge(x, oa, op, os_, gates, mv_g1, mv_idx, wa, wp, ws, wo, tm):
    n, d = x.shape
    rb = mv_g1.shape[1]
    rowb = lambda w: pl.BlockSpec((tm, w), lambda i: (i, 0))
    cst = lambda r, c: pl.BlockSpec((r, c), lambda i: (0, 0))
    return pl.pallas_call(
        _merge_kernel,
        name="merge_out",
        out_shape=jax.ShapeDtypeStruct((n, d), F32),
        grid=(n // tm,),
        in_specs=[rowb(d), rowb(D_ATTN), rowb(D_POOL), rowb(D_SSD), rowb(N_BRANCH * D_MODEL),
                  pl.BlockSpec((1, rb, d), lambda i: (mv_idx(i), 0, 0)),
                  cst(D_ATTN, d), cst(D_POOL, d), cst(D_SSD, d), cst(d, d)],
        out_specs=rowb(d),
        compiler_params=_cparams("parallel"),
    )(x, oa, op, os_, gates, mv_g1, wa, wp, ws, wo)


FFN_TILE = D_FF // 2
FFN_ROWS = 512


def _ffn_kernel(x_ref, hx_ref, sa_ref, sg_ref, sh_ref, sc_ref, g2_ref, gn_ref, wu_ref, cwa_ref, cwg_ref, cba_ref,
                cbg_ref, wd_ref, o_ref, ta_ref, tg_ref, xin_ref, ea_ref, eg_ref, *, tb, tv_last):
    i = pl.program_id(1)
    first = i == 0
    last = i == pl.num_programs(1) - 1
    xin_ref[0:CONV_HALO] = hx_ref[0]
    xin_ref[CONV_HALO:CONV_HALO + tb] = x_ref[0]
    xin = xin_ref[...]
    y = xin * lax.rsqrt(jnp.mean(xin * xin, axis=-1, keepdims=True) + EPS) * gn_ref[...]
    h = (y * (1.0 + sc_ref[0]) + sh_ref[0]).astype(BF16)
    acc = jnp.zeros((tb, x_ref.shape[2]), F32)
    for f in range(D_FF // FFN_TILE):
        cols = slice(f * FFN_TILE, (f + 1) * FFN_TILE)

        def conv(e_ref, s_ref, t_ref, cw_ref, cb_ref, col0):
            u = jnp.dot(h, wu_ref[:, col0 + f * FFN_TILE:col0 + (f + 1) * FFN_TILE], preferred_element_type=F32)
            e_ref[...] = u
            @pl.when(first)
            def _():
                e_ref[0:CONV_HALO] = s_ref[0, :, cols]

            @pl.when(last)
            def _():
                t_ref[0, :, cols] = e_ref[tv_last:tv_last + CONV_HALO]
            out = cb_ref[:, cols]
            for j in range(FFN_CONV):
                off = CONV_HALO - (FFN_CONV - 1) + j
                out = out + e_ref[off:off + tb] * cw_ref[j:j + 1, cols]
            return out

        a = conv(ea_ref, sa_ref, ta_ref, cwa_ref, cba_ref, 0)
        g = conv(eg_ref, sg_ref, tg_ref, cwg_ref, cbg_ref, D_FF)
        acc = acc + jnp.dot((_silu(g) * a).astype(BF16), wd_ref[cols, :], preferred_element_type=F32)
    o_ref[0] = x_ref[0] + g2_ref[0] * acc


def _ffn(x, sa, sg, sh2, sc2, g2, gn, wu, cwa, cwg, cba, cbg, wd, *, tb, t_valid):
    b, t, d = x.shape
    hb = tb // CONV_HALO
    blk = pl.BlockSpec((1, tb, d), lambda bi, i: (bi, i, 0))
    halo = pl.BlockSpec((1, CONV_HALO, d), lambda bi, i: (bi, jnp.maximum(i * hb - 1, 0), 0))
    st = pl.BlockSpec((1, CONV_HALO, D_FF), lambda bi, i: (bi, 0, 0))
    seqv = pl.BlockSpec((1, 1, d), lambda bi, i: (bi, 0, 0))
    cst = lambda r, c: pl.BlockSpec((r, c), lambda bi, i: (0, 0))
    tv_last = t_valid - (t // tb - 1) * tb
    return pl.pallas_call(
        functools.partial(_ffn_kernel, tb=tb, tv_last=tv_last),
        name="conv_ffn",
        out_shape=[jax.ShapeDtypeStruct((b, t, d), F32), jax.ShapeDtypeStruct((b, CONV_HALO, D_FF), F32),
                   jax.ShapeDtypeStruct((b, CONV_HALO, D_FF), F32)],
        grid=(b, t // tb),
        in_specs=[blk, halo, st, st, seqv, seqv, seqv, cst(1, d), cst(d, 2 * D_FF), cst(FFN_CONV, D_FF),
                  cst(FFN_CONV, D_FF), cst(1, D_FF), cst(1, D_FF), cst(D_FF, d)],
        out_specs=[blk, st, st],
        scratch_shapes=[pltpu.VMEM((CONV_HALO + tb, d), F32), pltpu.VMEM((CONV_HALO + tb, FFN_TILE), F32),
                        pltpu.VMEM((CONV_HALO + tb, FFN_TILE), F32)],
        compiler_params=_cparams("parallel", "arbitrary"),
    )(x, x, sa, sg, sh2, sc2, g2, gn, wu, cwa, cwg, cba, cbg, wd)


def _pad_rows(a, rows, front=False):
    extra = rows - a.shape[1]
    if extra == 0:
        return a
    return jnp.pad(a, ((0, 0), (extra, 0) if front else (0, extra), (0, 0)))


def _rope_tables(pos):
    half = HEAD_DIM // 2
    freqs = ROPE_THETA ** (-jnp.arange(half, dtype=jnp.float32) / half)
    ang = pos.astype(jnp.float32)[:, None] * freqs[None, :]
    cos, sin = jnp.cos(ang), jnp.sin(ang)
    return jnp.concatenate([cos, cos, cos, cos], axis=-1), jnp.concatenate([-sin, sin, -sin, sin], axis=-1)


def _layer_weights(l, w_ada, b_ada, norm1, norm2, w_in, q_norm, k_norm, pool_w, pool_scale, ssd_conv_w, ssd_conv_b,
                   ssd_dt_bias, ssd_a_log, ssd_d, ssd_norm, w_branch_attn, w_branch_pool, w_branch_ssd, w_out,
                   ffn_up, ffn_conv_w, ffn_conv_b, ffn_down):
    o = _OFFS
    win = w_in[l]
    col = lambda a, b: win[:, a:b]
    zpad = lambda n: jnp.zeros((D_MODEL, n), win.dtype)
    row = lambda v: v.reshape(1, -1).astype(F32)
    lanepad = lambda v: jnp.pad(v.astype(F32), (0, LANES - v.shape[0])).reshape(1, LANES)
    return dict(
        w_ada=w_ada[l].astype(BF16), b_ada=row(b_ada[l]), norm1=row(norm1[l]), norm2=row(norm2[l]),
        w_attn=jnp.concatenate([col(o[0], o[6]), zpad(LANES - IDX_DIM - IDX_HEADS)], axis=1).astype(BF16),
        w_pool=col(o[6], o[7]).astype(BF16),
        w_zdt=jnp.concatenate([col(o[7], o[8]), col(o[9], o[10]), zpad(LANES - SSD_HEADS)], axis=1).astype(BF16),
        w_xbc=col(o[8], o[9]).astype(BF16),
        w_gate=col(o[10], o[11]).astype(BF16),
        qn=row(jnp.tile(q_norm[l], LANES // HEAD_DIM)), kn=row(jnp.tile(k_norm[l], LANES // HEAD_DIM)),
        pool_w=pool_w[l].astype(BF16), pool_scale=row(pool_scale[l]),
        cw=ssd_conv_w[l].astype(F32), cb=row(ssd_conv_b[l]), dtb=lanepad(ssd_dt_bias[l]), alog=lanepad(ssd_a_log[l]),
        dsk=row(jnp.repeat(ssd_d[l], SSD_HEAD_DIM)), nw=row(ssd_norm[l]),
        wa=w_branch_attn[l].astype(BF16), wp=w_branch_pool[l].astype(BF16), ws=w_branch_ssd[l].astype(BF16),
        wo=w_out[l].astype(BF16),
        w_up=ffn_up[l].astype(BF16),
        cwa=ffn_conv_w[l][:, :D_FF].astype(F32), cwg=ffn_conv_w[l][:, D_FF:].astype(F32),
        cba=row(ffn_conv_b[l][:D_FF]), cbg=row(ffn_conv_b[l][D_FF:]), wd=ffn_down[l].astype(BF16))


def _group_layer(x, mod, lw, consts, *, pos0, past, states):
    b, t, d = x.shape
    n = b * t
    tp = -(-t // SEQ_BLOCK) * SEQ_BLOCK
    per_seq = t % SEQ_BLOCK == 0
    tm = min(ROW_TILE, t) if per_seq else n
    sh1, sc1, g1, sh2, sc2, g2 = [mod[:, i * d:(i + 1) * d] for i in range(6)]
    pos = pos0 + jnp.arange(t)
    cos_t, sin_t = _rope_tables(pos)
    if per_seq:
        view = lambda m: m.reshape(b, 1, d)
        mv_idx = lambda i: i // (t // tm)
        tab = lambda a: a.reshape(t // tm, tm, LANES)
        tab_idx = lambda i: i % (t // tm)
    else:
        view = lambda m: jnp.repeat(m, t, axis=0).reshape(1, n, d)
        mv_idx = lambda i: 0
        tab = lambda a: jnp.tile(a, (b, 1)).reshape(1, n, LANES)
        tab_idx = lambda i: 0
    xf = x.reshape(n, d)
    nm = lambda w: _norm_matmul(xf, view(sh1), view(sc1), mv_idx, lw['norm1'], w, tm)
    pa, pool_u, zdt, xbc, gates = nm(lw['w_attn']), nm(lw['w_pool']), nm(lw['w_zdt']), nm(lw['w_xbc']), nm(lw['w_gate'])

    q_r, k_r, qi_r, ki2, wi_s = _attn_prep(pa, tab(cos_t), tab(sin_t), tab_idx, lw['qn'], lw['kn'], consts['seg'], tm)
    v_new = pa[:, D_ATTN + D_KV:D_ATTN + 2 * D_KV]
    seq3 = lambda a: a.reshape(b, t, a.shape[-1])
    if past is None:
        tq, tk, kw = ATTN_Q_ROWS, min(ATTN_KEYS, t), LANES
        k_all, v_all, ki_all = seq3(k_r).astype(BF16), seq3(v_new).astype(BF16), seq3(ki2).astype(BF16)
        s_valid = t
        qpad = lambda a: seq3(a)
    else:
        tq, kw = 2 * SUBLANES, LANES
        page_table, pool_off, ck, cv, ci = past
        k_all, v_all, ki_all = _page_gather(page_table, pool_off, ck, cv, ci, seq3(k_r), seq3(v_new), seq3(ki2))
        tk = min(ATTN_KEYS_DECODE, k_all.shape[1] // 2)
        s_valid = pos0 + t
        qpad = lambda a: _pad_rows(seq3(a), tq)
    assert k_all.shape[1] % tk == 0
    kk = min(INDEX_TOPK, s_valid // 4)
    o_attn = _attention(qpad(q_r), qpad(qi_r), qpad(wi_s), k_all, v_all, ki_all,
                        tq=tq, tk=tk, kw=kw, pos0=pos0, s_valid=s_valid, kk=kk)[:, :t].reshape(n, D_ATTN)

    st_pool, st_sconv, (st_ssd, ssd_off), st_fconv = states
    seqp = lambda a: _pad_rows(seq3(a), tp)
    o_pool = _pool(seqp(pool_u), _pad_rows(st_pool, POOL_HALO, front=True), lw['pool_w'], lw['pool_scale'], pos0=pos0)
    o_ssd, ssd_state = _ssd(seqp(zdt), seqp(xbc), _pad_rows(st_sconv, CONV_HALO, front=True),
                            st_ssd, ssd_off, lw['cw'], lw['cb'], lw['dtb'], lw['alog'], lw['dsk'],
                            lw['nw'], consts['ex'], consts['ext_t'], t_valid=t)
    unpad = lambda a: a[:, :t].reshape(n, a.shape[-1])
    x1 = _merge(xf, o_attn, unpad(o_pool), unpad(o_ssd), gates, view(g1), mv_idx,
                lw['wa'], lw['wp'], lw['ws'], lw['wo'], tm)

    st_f = _pad_rows(st_fconv, CONV_HALO, front=True)
    per_b = lambda m: m.reshape(b, 1, d)
    x2, tail_a, tail_g = _ffn(seqp(x1), st_f[:, :, :D_FF], st_f[:, :, D_FF:], per_b(sh2), per_b(sc2), per_b(g2),
                              lw['norm2'], lw['w_up'], lw['cwa'], lw['cwg'], lw['cba'], lw['cbg'], lw['wd'],
                              tb=min(FFN_ROWS, tp), t_valid=t)
    x2 = x2[:, :t]
    fconv_state = jnp.concatenate([tail_a, tail_g], axis=-1)[:, -(FFN_CONV - 1):]

    def tail(prev, news, keep):
        last = jnp.concatenate([seq3(a)[:, -min(keep, t):] for a in news], axis=-1)
        return jnp.concatenate([prev, last], axis=1)[:, -keep:]

    new_states = (seq3(k_r).reshape(b, t, N_KV_HEADS, HEAD_DIM), seq3(v_new).reshape(b, t, N_KV_HEADS, HEAD_DIM),
                  seq3(ki2)[:, :, :IDX_DIM], tail(st_pool, [pool_u], POOL_CTX), tail(st_sconv, [xbc], SSD_CONV - 1),
                  ssd_state.reshape(b, SSD_HEADS, SSD_HEAD_DIM, SSD_STATE), fconv_state)
    return x2, new_states


def kernel(x_prompt, x_sample, c_prompt, c_sample, cache_k, cache_v, cache_kidx, page_table, state_pool, state_ssd_conv, state_ssd, state_ffn_conv, w_ada, b_ada, norm1, norm2, w_in, q_norm, k_norm, pool_w, pool_scale, ssd_conv_w, ssd_conv_b, ssd_dt_bias, ssd_a_log, ssd_d, ssd_norm, w_branch_attn, w_branch_pool, w_branch_ssd, w_out, ffn_up, ffn_conv_w, ffn_conv_b, ffn_down):
    bp, tp, d = x_prompt.shape
    bs, ts, _ = x_sample.shape
    depth = w_ada.shape[0]
    n_pool, page = cache_k.shape[1], cache_k.shape[2]
    past_len = page_table.shape[1] * page
    dt = x_prompt.dtype

    r = lax.broadcasted_iota(jnp.int32, (LANES, LANES), 0)
    c = lax.broadcasted_iota(jnp.int32, (LANES, LANES), 1)
    hr = lax.broadcasted_iota(jnp.int32, (LANES, D_SSD), 0)
    hc = lax.broadcasted_iota(jnp.int32, (LANES, D_SSD), 1)
    ex = (hr == hc // SSD_HEAD_DIM).astype(BF16)
    consts = dict(seg=(r // HEAD_DIM == c // HEAD_DIM).astype(BF16), ex=ex, ext_t=ex.T)

    c_all = jnp.concatenate([c_prompt, c_sample], axis=0)
    rows = -(-c_all.shape[0] // SUBLANES) * SUBLANES
    c_pad = jnp.pad(c_all, ((0, rows - c_all.shape[0]), (0, 0)))

    ck_t = jnp.transpose(cache_k, (0, 1, 3, 4, 2)).reshape(depth * n_pool, D_KV, page)
    cv_t = jnp.transpose(cache_v, (0, 1, 3, 4, 2)).reshape(depth * n_pool, D_KV, page)
    ci_t = jnp.transpose(cache_kidx, (0, 1, 3, 2)).reshape(depth * n_pool, IDX_DIM, page)

    ssd_all = state_ssd.reshape(depth * bs, D_SSD, SSD_STATE)
    zeros = lambda *s: jnp.zeros(s, dt)
    xp, xs = x_prompt, x_sample
    new_p, new_s = [[] for _ in range(7)], [[] for _ in range(7)]
    for l in range(depth):
        lw = _layer_weights(l, w_ada, b_ada, norm1, norm2, w_in, q_norm, k_norm, pool_w, pool_scale, ssd_conv_w,
                            ssd_conv_b, ssd_dt_bias, ssd_a_log, ssd_d, ssd_norm, w_branch_attn, w_branch_pool,
                            w_branch_ssd, w_out, ffn_up, ffn_conv_w, ffn_conv_b, ffn_down)
        mod = _mod(c_pad, lw['w_ada'], lw['b_ada'])
        xp, st_p = _group_layer(
            xp, mod[:bp], lw, consts, pos0=0, past=None,
            states=(zeros(bp, POOL_CTX, D_POOL), zeros(bp, SSD_CONV - 1, D_SSD_CONV),
                    (zeros(bp, D_SSD, SSD_STATE), 0), zeros(bp, FFN_CONV - 1, 2 * D_FF)))
        past = (page_table, l * n_pool, ck_t, cv_t, ci_t)
        xs, st_s = _group_layer(xs, mod[bp:bp + bs], lw, consts, pos0=past_len, past=past,
                                states=(state_pool[l], state_ssd_conv[l], (ssd_all, l * bs), state_ffn_conv[l]))
        for j in range(7):
            new_p[j].append(st_p[j])
            new_s[j].append(st_s[j])
    outs_p = [jnp.stack(s) for s in new_p]
    outs_s = [jnp.stack(s) for s in new_s]
    return (xp, xs, *outs_p, *outs_s)
```

```python
import functools

import numpy as np
import jax
import jax.numpy as jnp
from jax import lax
from jax.experimental import pallas as pl
from jax.experimental.pallas import tpu as pltpu

D_MODEL = 1024
N_HEADS = 16
N_KV_HEADS = 4
HEAD_DIM = 64
D_ATTN = N_HEADS * HEAD_DIM
D_KV = N_KV_HEADS * HEAD_DIM
IDX_HEADS = 8
IDX_DIM = 64
INDEX_TOPK = 256
ROPE_THETA = 10000.0
POOL_WINDOWS = (2, 4, 8, 16)
D_POOL = D_MODEL
POOL_GROUP = D_POOL // len(POOL_WINDOWS)
POOL_CTX = max(POOL_WINDOWS) - 1
D_SSD = 2 * D_MODEL
SSD_HEAD_DIM = 64
SSD_HEADS = D_SSD // SSD_HEAD_DIM
SSD_GROUPS = 4
SSD_STATE = 128
SSD_CONV = 4
D_SSD_CONV = D_SSD + 2 * SSD_GROUPS * SSD_STATE
D_FF = 2816
FFN_CONV = 3
N_BRANCH = 3
EPS = 1e-6
_SPLITS = (D_ATTN, D_KV, D_KV, IDX_HEADS * IDX_DIM, IDX_DIM, IDX_HEADS, D_POOL, D_SSD, D_SSD_CONV,
           SSD_HEADS, N_BRANCH * D_MODEL)
_OFFS = tuple(int(v) for v in np.cumsum((0,) + _SPLITS))

F32 = jnp.float32
BF16 = jnp.bfloat16
LANES = 128
SUBLANES = 8
SEQ_BLOCK = 128
ROW_TILE = 512
ATTN_Q_ROWS = 256
ATTN_KEYS = 256
ATTN_KEYS_DECODE = 1024
VMEM_LIMIT = 56 * 1024 * 1024
NEG = -1e30
LOG2E = 1.4426950408889634
N_BISECT = 20
NT = (((1,), (1,)), ((), ()))
TN = (((0,), (0,)), ((), ()))


def _cparams(*sem):
    return pltpu.CompilerParams(dimension_semantics=sem, vmem_limit_bytes=VMEM_LIMIT)


def _silu(v):
    return (0.5 * v) * (1.0 + jnp.tanh(0.5 * v))


def _split3(v):
    h1 = v.astype(BF16)
    r1 = v - h1.astype(F32)
    h2 = r1.astype(BF16)
    h3 = (r1 - h2.astype(F32)).astype(BF16)
    return h1, h2, h3


def _dot3(v, w):
    h1, h2, h3 = _split3(v)
    return (jnp.dot(h1, w, preferred_element_type=F32) + jnp.dot(h2, w, preferred_element_type=F32)
            + jnp.dot(h3, w, preferred_element_type=F32))


def _mod_kernel(c_ref, w_ref, b_ref, o_ref):
    s = _silu(c_ref[...])
    o_ref[...] = jnp.dot(s.astype(BF16), w_ref[...], preferred_element_type=F32) + b_ref[...]


def _mod(c_pad, w, b):
    rows, d = c_pad.shape
    n = w.shape[1]
    tn = n // 4
    return pl.pallas_call(
        _mod_kernel,
        name="adaln_mod",
        out_shape=jax.ShapeDtypeStruct((rows, n), F32),
        grid=(n // tn,),
        in_specs=[pl.BlockSpec((rows, d), lambda j: (0, 0)),
                  pl.BlockSpec((d, tn), lambda j: (0, j)),
                  pl.BlockSpec((1, tn), lambda j: (0, j))],
        out_specs=pl.BlockSpec((rows, tn), lambda j: (0, j)),
        compiler_params=_cparams("parallel"),
    )(c_pad, w, b)


def _norm_matmul_kernel(x_ref, sh_ref, sc_ref, g_ref, w_ref, o_ref):
    x = x_ref[...]
    y = x * lax.rsqrt(jnp.mean(x * x, axis=-1, keepdims=True) + EPS) * g_ref[...]
    h = y * (1.0 + sc_ref[0]) + sh_ref[0]
    o_ref[...] = jnp.dot(h.astype(BF16), w_ref[...], preferred_element_type=F32).astype(o_ref.dtype)


def _norm_matmul(x, mv_sh, mv_sc, mv_idx, g, w, tm):
    n, d = x.shape
    nout = w.shape[1]
    rb = mv_sh.shape[1]
    return pl.pallas_call(
        _norm_matmul_kernel,
        name="norm_matmul",
        out_shape=jax.ShapeDtypeStruct((n, nout), F32),
        grid=(n // tm,),
        in_specs=[pl.BlockSpec((tm, d), lambda i: (i, 0)),
                  pl.BlockSpec((1, rb, d), lambda i: (mv_idx(i), 0, 0)),
                  pl.BlockSpec((1, rb, d), lambda i: (mv_idx(i), 0, 0)),
                  pl.BlockSpec((1, d), lambda i: (0, 0)),
                  pl.BlockSpec((d, nout), lambda i: (0, 0))],
        out_specs=pl.BlockSpec((tm, nout), lambda i: (i, 0)),
        compiler_params=_cparams("parallel"),
    )(x, mv_sh, mv_sc, g, w)


def _attn_prep_kernel(q_ref, k_ref, qi_ref, kw_ref, cos_ref, sin_ref, qn_ref, kn_ref, seg_ref,
                      qo_ref, ko_ref, qio_ref, kio_ref, wio_ref):
    tm = q_ref.shape[0]
    lane = lax.broadcasted_iota(jnp.int32, (tm, LANES), 1)
    first_half = (lane % HEAD_DIM) < (HEAD_DIM // 2)
    cos = cos_ref[0]
    sin = sin_ref[0]
    seg = seg_ref[...]

    def rope(v):
        up = pltpu.roll(v, LANES - HEAD_DIM // 2, 1)
        dn = pltpu.roll(v, HEAD_DIM // 2, 1)
        return v * cos + jnp.where(first_half, up, dn) * sin

    def headnorm(v, gain):
        sq = v * v
        hi = sq.astype(BF16)
        lo = (sq - hi.astype(F32)).astype(BF16)
        ss = jnp.dot(hi, seg, preferred_element_type=F32) + jnp.dot(lo, seg, preferred_element_type=F32)
        return v * lax.rsqrt(ss * (1.0 / HEAD_DIM) + EPS) * gain

    qn = qn_ref[...]
    kn = kn_ref[...]
    for j in range(D_ATTN // LANES):
        sl = slice(j * LANES, (j + 1) * LANES)
        qo_ref[:, sl] = rope(headnorm(q_ref[:, sl], qn)) * (HEAD_DIM ** -0.5 * LOG2E)
    for j in range(D_KV // LANES):
        sl = slice(j * LANES, (j + 1) * LANES)
        ko_ref[:, sl] = rope(headnorm(k_ref[:, sl], kn))
    for j in range(IDX_HEADS * IDX_DIM // LANES):
        sl = slice(j * LANES, (j + 1) * LANES)
        qio_ref[:, sl] = rope(qi_ref[:, sl])
    kw = kw_ref[...]
    kr = rope(kw)
    kio_ref[...] = jnp.where(lane < IDX_DIM, kr, pltpu.roll(kr, IDX_DIM, 1))
    wio_ref[...] = pltpu.roll(kw, LANES - IDX_DIM, 1) * (IDX_HEADS ** -0.5 * IDX_DIM ** -0.5)


def _attn_prep(pa, cos_t, sin_t, tab_idx, qn, kn, seg, tm):
    n = pa.shape[0]
    blk = lambda w, c: pl.BlockSpec((tm, w), lambda i: (i, c))
    tab = pl.BlockSpec((1, tm, LANES), lambda i: (tab_idx(i), 0, 0))
    cst = lambda r, c: pl.BlockSpec((r, c), lambda i: (0, 0))
    return pl.pallas_call(
        _attn_prep_kernel,
        name="attn_prep",
        out_shape=[jax.ShapeDtypeStruct((n, D_ATTN), F32), jax.ShapeDtypeStruct((n, D_KV), F32),
                   jax.ShapeDtypeStruct((n, IDX_HEADS * IDX_DIM), F32),
                   jax.ShapeDtypeStruct((n, LANES), F32), jax.ShapeDtypeStruct((n, LANES), F32)],
        grid=(n // tm,),
        in_specs=[blk(D_ATTN, 0), blk(D_KV, D_ATTN // D_KV), blk(IDX_HEADS * IDX_DIM, 3), blk(LANES, 16),
                  tab, tab, cst(1, LANES), cst(1, LANES), cst(LANES, LANES)],
        out_specs=[blk(D_ATTN, 0), blk(D_KV, 0), blk(IDX_HEADS * IDX_DIM, 0), blk(LANES, 0), blk(LANES, 0)],
        compiler_params=_cparams("parallel"),
    )(pa, pa, pa, pa, cos_t, sin_t, qn, kn, seg)


GATHER_PAGES = 16


def _page_gather_kernel(pt_ref, *refs, n_steps, pg, page):
    ins, outs = refs[:3 * pg + 3], refs[3 * pg + 3:]
    past = pl.program_id(1) < n_steps
    for t in range(3):
        new_ref, out_ref = ins[3 * pg + t], outs[t]
        for i in range(pg):
            rows = slice(i * page, (i + 1) * page)
            tile = ins[t * pg + i][0]
            if t == 2:
                tile = jnp.concatenate([tile, tile], axis=0)
            out_ref[0, rows, :] = jnp.where(past, tile.T, new_ref[0, rows, :]).astype(out_ref.dtype)


def _page_gather(page_table, pool_off, ck, cv, ci, nk, nv, ni):
    bd, n_pages = page_table.shape
    page = ck.shape[2]
    pg = max(g for g in range(1, GATHER_PAGES + 1) if n_pages % g == 0)
    n_steps = n_pages // pg
    nk, nv, ni = [_pad_rows(a, pg * page) for a in (nk, nv, ni)]

    def cache(w, i):
        return pl.BlockSpec((1, w, page),
                            lambda b, p, pt: (pool_off + pt[b, jnp.minimum(p, n_steps - 1) * pg + i], 0, 0))

    new = lambda w: pl.BlockSpec((1, pg * page, w), lambda b, p, pt: (b, 0, 0))
    out = lambda w: pl.BlockSpec((1, pg * page, w), lambda b, p, pt: (b, p, 0))
    s_pad = (n_steps + 1) * pg * page
    in_specs = ([cache(D_KV, i) for i in range(pg)] + [cache(D_KV, i) for i in range(pg)]
                + [cache(IDX_DIM, i) for i in range(pg)] + [new(D_KV), new(D_KV), new(2 * IDX_DIM)])
    return pl.pallas_call(
        functools.partial(_page_gather_kernel, n_steps=n_steps, pg=pg, page=page),
        out_shape=[jax.ShapeDtypeStruct((bd, s_pad, D_KV), BF16), jax.ShapeDtypeStruct((bd, s_pad, D_KV), BF16),
                   jax.ShapeDtypeStruct((bd, s_pad, 2 * IDX_DIM), BF16)],
        grid_spec=pltpu.PrefetchScalarGridSpec(
            num_scalar_prefetch=1, grid=(bd, n_steps + 1), in_specs=in_specs,
            out_specs=[out(D_KV), out(D_KV), out(2 * IDX_DIM)]),
        compiler_params=_cparams("parallel", "parallel"),
        name="page_gather",
    )(page_table, *([ck] * pg), *([cv] * pg), *([ci] * pg), nk, nv, ni)


GQA = N_HEADS // N_KV_HEADS


def _attn_kernel(q_ref, qi_ref, wi_ref, k_ref, v_ref, ki_ref, o_ref,
                 sc_ref, qh_ref, qim_ref, wb_ref, m_ref, l_ref, acc_ref, *, tq, tk, kw, nblk, n_static, pos0, s_valid, kk):
    q0 = pl.program_id(1) * tq
    nch = tk // LANES
    ngrp = D_KV // kw
    hpg = N_HEADS // ngrp
    lane = lax.broadcasted_iota(jnp.int32, (tq, LANES), 1)
    row = lax.broadcasted_iota(jnp.int32, (tq, LANES), 0)
    qpos = pos0 + q0 + row
    lo_half = lane < HEAD_DIM
    n_kv = jnp.minimum(nblk, (pos0 + q0 + tq - 1) // tk + 1)
    kkf = float(kk)
    chunk = lambda a, j: a[..., j * LANES:(j + 1) * LANES]
    rg = min(tq, LANES)

    def kv_loop(body, init):
        if n_static is None:
            return lax.fori_loop(0, n_kv, body, init)
        for c in range(n_static):
            init = body(c, init)
        return init

    def key_off(c):
        return c * tk if n_static is not None else pl.multiple_of(c * tk, tk)

    def bcast(col):
        return jnp.broadcast_to(col, col.shape[:-1] + (LANES,))

    wi = wi_ref[0]
    for j in range(IDX_HEADS // 2):
        t = qi_ref[0, :, j * LANES:(j + 1) * LANES]
        qim_ref[2 * j * tq:(2 * j + 1) * tq] = jnp.where(lo_half, t, 0.0).astype(BF16)
        qim_ref[(2 * j + 1) * tq:(2 * j + 2) * tq] = jnp.where(lo_half, 0.0, t).astype(BF16)
    for h in range(IDX_HEADS):
        wb_ref[h] = bcast(wi[:, h:h + 1])
    def kv_tile(h):
        kvh = h // GQA
        return (kvh * HEAD_DIM % kw) // LANES, kvh % 2

    if kw > LANES:
        qh_ref[...] = jnp.zeros(qh_ref.shape, BF16)
    for h in range(N_HEADS):
        t = q_ref[0, :, (h // 2) * LANES:(h // 2 + 1) * LANES]
        tile, kv_half = kv_tile(h)
        if h % 2 != kv_half:
            t = pltpu.roll(t, HEAD_DIM, 1)
        keep = lo_half if kv_half == 0 else jnp.logical_not(lo_half)
        hh = h % hpg
        qh_ref[h // hpg, hh * tq:(hh + 1) * tq, tile * LANES:(tile + 1) * LANES] = jnp.where(keep, t, 0.0).astype(BF16)
    m_ref[...] = jnp.full(m_ref.shape, NEG, F32)
    l_ref[...] = jnp.zeros(l_ref.shape, F32)
    acc_ref[...] = jnp.zeros(acc_ref.shape, F32)

    def p1(c, carry):
        mn, mx = carry
        kid = ki_ref[0, pl.ds(key_off(c), tk), :]
        y = lax.dot_general(qim_ref[...], kid, NT, preferred_element_type=F32)
        y = jnp.maximum(y, 0.0).reshape(IDX_HEADS, tq, tk)
        for j in range(nch):
            acc = wb_ref[0] * chunk(y[0], j)
            for h in range(1, IDX_HEADS):
                acc = acc + wb_ref[h] * chunk(y[h], j)
            kpos = c * tk + j * LANES + lane
            vis = jnp.logical_and(kpos <= qpos, kpos < s_valid)
            sc_ref[c, :, j * LANES:(j + 1) * LANES] = jnp.where(vis, acc, -jnp.inf)
            mn = jnp.minimum(mn, jnp.where(vis, acc, jnp.inf))
            mx = jnp.maximum(mx, jnp.where(vis, acc, -jnp.inf))
        return mn, mx

    mn, mx = kv_loop(p1, (jnp.full((tq, LANES), jnp.inf, F32), jnp.full((tq, LANES), -jnp.inf, F32)))
    rmin = bcast(jnp.min(mn, axis=1, keepdims=True))
    rmax = bcast(jnp.max(mx, axis=1, keepdims=True))

    def row_pass(step, init, reduce):
        outs = []
        for r in range(tq // rg):
            rows = slice(r * rg, (r + 1) * rg)

            def body(c, acc, rows=rows):
                x = sc_ref[c, rows, :]
                for j in range(nch):
                    acc = step(acc, chunk(x, j), rows)
                return acc
            outs.append(bcast(reduce(kv_loop(body, jnp.full((rg, LANES), init, F32)), axis=1, keepdims=True)))
        return outs[0] if len(outs) == 1 else jnp.concatenate(outs, axis=0)

    def count_gt(t_b):
        return row_pass(lambda acc, xj, rows: acc + jnp.where(xj > t_b[rows], 1.0, 0.0), 0.0, jnp.sum)

    def next_above(t_b):
        return row_pass(lambda acc, xj, rows: jnp.minimum(acc, jnp.where(xj > t_b[rows], xj, jnp.inf)), jnp.inf, jnp.min)

    at_min = jnp.minimum(qpos + 1, s_valid).astype(F32) < kkf
    below = rmin - jnp.maximum((rmax - rmin) * 2.0 ** -8, jnp.abs(rmin) * 2.0 ** -20 + 1e-30)

    def bisect(i, carry):
        lo, hi = carry
        mid = lo + 0.5 * (hi - lo)
        up = count_gt(mid) >= kkf
        return jnp.where(up, mid, lo), jnp.where(up, hi, mid)

    lo, hi = lax.fori_loop(0, N_BISECT, bisect, (below, rmax))
    tau = next_above(lo)
    ct = count_gt(tau)

    def pending(ct_):
        return jnp.logical_and(ct_ >= kkf, jnp.logical_not(at_min))

    def w_cond(st):
        return jnp.max(jnp.where(pending(st[2]), 1.0, 0.0)) > 0.5

    def w_body(st):
        lo_, tau_, ct_ = st
        need = pending(ct_)
        lo2 = jnp.where(need, tau_, lo_)
        tau2 = next_above(lo2)
        ct2 = count_gt(tau2)
        return lo2, jnp.where(need, tau2, tau_), jnp.where(need, ct2, ct_)

    lo, tau, ct = lax.while_loop(w_cond, w_body, (lo, tau, ct))
    tau = jnp.where(at_min, rmin, tau)
    m_need = jnp.where(at_min, kkf, kkf - ct)

    rk = lax.broadcasted_iota(jnp.int32, (tk, tk), 0)
    ck = lax.broadcasted_iota(jnp.int32, (tk, tk), 1)
    before = jnp.where(rk < ck, 1.0, 0.0).astype(BF16)

    def p2(c, cum):
        x = sc_ref[c]
        eqs = [chunk(x, j) == tau for j in range(nch)]
        eqf = [jnp.where(e, 1.0, 0.0) for e in eqs]
        pre = jnp.dot(jnp.concatenate(eqf, axis=-1).astype(BF16), before, preferred_element_type=F32)
        bias = [jnp.where(chunk(x, j) > tau, 0.0,
                          jnp.where(eqs[j], jnp.where(chunk(pre, j) + cum < m_need, 0.0, NEG), NEG))
                for j in range(nch)]
        tot = eqf[0]
        for j in range(1, nch):
            tot = tot + eqf[j]
        cum = cum + bcast(jnp.sum(tot, axis=1, keepdims=True))
        koff = key_off(c)
        for p in range(ngrp):
            kp = k_ref[0, pl.ds(koff, tk), p * kw:(p + 1) * kw]
            vp = v_ref[0, pl.ds(koff, tk), p * kw:(p + 1) * kw]
            s = lax.dot_general(qh_ref[p], kp, NT, preferred_element_type=F32).reshape(hpg, tq, tk)
            sm = [chunk(s, j) + bias[j][None] for j in range(nch)]
            mx_ = sm[0]
            for j in range(1, nch):
                mx_ = jnp.maximum(mx_, sm[j])
            m_old = m_ref[p]
            m_new = jnp.maximum(m_old, bcast(jnp.max(mx_, axis=-1, keepdims=True)))
            pe = [jnp.exp2(sm[j] - m_new) for j in range(nch)]
            ps = pe[0]
            for j in range(1, nch):
                ps = ps + pe[j]
            alpha = jnp.exp2(m_old - m_new)
            l_ref[p] = alpha * l_ref[p] + bcast(jnp.sum(ps, axis=-1, keepdims=True))
            pv = jnp.dot(jnp.concatenate(pe, axis=-1).reshape(hpg * tq, tk).astype(BF16), vp,
                         preferred_element_type=F32)
            alpha = alpha.reshape(hpg * tq, LANES)
            acc_ref[p] = jnp.concatenate([alpha] * (kw // LANES), axis=-1) * acc_ref[p] + pv
            m_ref[p] = m_new
        return cum

    kv_loop(p2, jnp.zeros((tq, LANES), F32))

    def head_out(h):
        p, hh = h // hpg, h % hpg
        tile, kv_half = kv_tile(h)
        r = acc_ref[p, hh * tq:(hh + 1) * tq, tile * LANES:(tile + 1) * LANES] / l_ref[p, hh]
        if kv_half != h % 2:
            r = pltpu.roll(r, HEAD_DIM, 1)
        return r

    for j in range(N_HEADS // 2):
        o_ref[0, :, j * LANES:(j + 1) * LANES] = jnp.where(lo_half, head_out(2 * j), head_out(2 * j + 1)).astype(o_ref.dtype)


def _attention(q, qi, wi, k_all, v_all, ki2, *, tq, tk, kw, pos0, s_valid, kk):
    b, tq_pad, _ = q.shape
    s_pad = k_all.shape[1]
    nblk = s_pad // tk
    ngrp = D_KV // kw
    hpg = N_HEADS // ngrp
    n_static = min(nblk, (pos0 + tq - 1) // tk + 1) if tq_pad == tq else None
    qblk = lambda w: pl.BlockSpec((1, tq, w), lambda bi, qi_: (bi, qi_, 0))
    seq = lambda w: pl.BlockSpec((1, s_pad, w), lambda bi, qi_: (bi, 0, 0))
    return pl.pallas_call(
        functools.partial(_attn_kernel, tq=tq, tk=tk, kw=kw, nblk=nblk, n_static=n_static, pos0=pos0, s_valid=s_valid,
                          kk=kk),
        out_shape=jax.ShapeDtypeStruct((b, tq_pad, D_ATTN), BF16),
        grid=(b, tq_pad // tq),
        in_specs=[qblk(D_ATTN), qblk(IDX_HEADS * IDX_DIM), qblk(LANES), seq(D_KV), seq(D_KV), seq(LANES)],
        out_specs=qblk(D_ATTN),
        scratch_shapes=[pltpu.VMEM((nblk, tq, tk), F32), pltpu.VMEM((ngrp, hpg * tq, kw), BF16),
                        pltpu.VMEM((IDX_HEADS * tq, LANES), BF16), pltpu.VMEM((IDX_HEADS, tq, LANES), F32),
                        pltpu.VMEM((ngrp, hpg, tq, LANES), F32), pltpu.VMEM((ngrp, hpg, tq, LANES), F32),
                        pltpu.VMEM((ngrp, hpg * tq, kw), F32)],
        compiler_params=_cparams("parallel", "parallel"),
        name="sparse_attention",
    )(q, qi, wi, k_all, v_all, ki2)


POOL_HALO = 16


def _pool_kernel(u_ref, halo_ref, st_ref, w_ref, scale_ref, o_ref, ext_ref, *, tb, pos0):
    i = pl.program_id(1)
    ext_ref[0:POOL_HALO] = jnp.where(i == 0, st_ref[0], halo_ref[0])
    ext_ref[POOL_HALO:POOL_HALO + tb] = u_ref[0]
    pos = pos0 + i * tb + lax.broadcasted_iota(jnp.int32, (tb, 1), 0)
    for gi, w in enumerate(POOL_WINDOWS):
        sl = slice(gi * POOL_GROUP, (gi + 1) * POOL_GROUP)
        tot = ext_ref[POOL_HALO:POOL_HALO + tb, sl]
        for k in range(1, w):
            tot = tot + ext_ref[POOL_HALO - k:POOL_HALO - k + tb, sl]
        cnt = jnp.minimum(w, pos + 1).astype(F32)
        pooled = tot / cnt - u_ref[0, :, sl]
        mixed = jnp.dot(pooled.astype(BF16), w_ref[gi], preferred_element_type=F32) * scale_ref[:, sl]
        o_ref[0, :, sl] = mixed.astype(o_ref.dtype)


def _pool(u, state16, w, scale, *, pos0):
    b, t, c = u.shape
    tb = SEQ_BLOCK
    hb = tb // POOL_HALO
    return pl.pallas_call(
        functools.partial(_pool_kernel, tb=tb, pos0=pos0),
        name="pool_mix",
        out_shape=jax.ShapeDtypeStruct((b, t, c), BF16),
        grid=(b, t // tb),
        in_specs=[pl.BlockSpec((1, tb, c), lambda bi, i: (bi, i, 0)),
                  pl.BlockSpec((1, POOL_HALO, c), lambda bi, i: (bi, jnp.maximum(i * hb - 1, 0), 0)),
                  pl.BlockSpec((1, POOL_HALO, c), lambda bi, i: (bi, 0, 0)),
                  pl.BlockSpec((len(POOL_WINDOWS), POOL_GROUP, POOL_GROUP), lambda bi, i: (0, 0, 0)),
                  pl.BlockSpec((1, c), lambda bi, i: (0, 0))],
        out_specs=pl.BlockSpec((1, tb, c), lambda bi, i: (bi, i, 0)),
        scratch_shapes=[pltpu.VMEM((POOL_HALO + tb, c), F32)],
        compiler_params=_cparams("parallel", "parallel"),
    )(u, u, state16, w, scale)


CONV_HALO = SUBLANES
SSD_GH = SSD_HEADS // SSD_GROUPS
SSD_GW = SSD_GH * SSD_HEAD_DIM
CONV_CHUNK = 512


def _ssd_kernel(z_ref, dt_ref, xbc_ref, halo_ref, st_ref, h0_ref, cw_ref, cb_ref, dtb_ref, alog_ref, dsk_ref,
                nw_ref, ex_ref, ext_t_ref, y_ref, hT_ref, ext_ref, xs_ref, h_ref, *, q, t_valid):
    c = pl.program_id(1)

    @pl.when(c == 0)
    def _():
        h_ref[...] = h0_ref[0]

    ext_ref[0:CONV_HALO] = jnp.where(c == 0, st_ref[0], halo_ref[0])
    ext_ref[CONV_HALO:CONV_HALO + q] = xbc_ref[0]
    for ch in range(D_SSD_CONV // CONV_CHUNK):
        cc = slice(ch * CONV_CHUNK, (ch + 1) * CONV_CHUNK)
        xc = cb_ref[:, cc]
        for j in range(SSD_CONV):
            off = CONV_HALO - (SSD_CONV - 1) + j
            xc = xc + ext_ref[off:off + q, cc] * cw_ref[j:j + 1, cc]
        xs_ref[:, cc] = _silu(xc)
    x = xs_ref[:, :D_SSD]

    lane = lax.broadcasted_iota(jnp.int32, (q, LANES), 1)
    row = lax.broadcasted_iota(jnp.int32, (q, LANES), 0)
    v = dt_ref[0] + dtb_ref[...]
    dt = jnp.maximum(v, 0.0) + jnp.log1p(jnp.exp(-jnp.abs(v)))
    dt = jnp.where(jnp.logical_and(lane < SSD_HEADS, c * q + row < t_valid), dt, 0.0)
    a = dt * (-jnp.exp(alog_ref[...]))
    tri = jnp.where(lane <= row, 1.0, 0.0).astype(BF16)
    causal = lane <= row
    cs = _dot3_left(tri, a)
    cs_t = cs.T
    ex = ex_ref[...]
    cs_last = cs[q - 1:q, :]
    dt_x = _dot3(dt, ex)
    ecs_x = jnp.exp(_dot3(cs, ex))
    dec_x = jnp.exp(_dot3(cs_last - cs, ex))
    xdt = x * dt_x
    xdec = (xdt * dec_x).astype(BF16)
    xdt16 = xdt.astype(BF16)
    rowdec = jnp.exp(_dot3_left(ext_t_ref[...], jnp.broadcast_to(cs_t[:, q - 1:q], (LANES, LANES))))
    lo_half = lane < SSD_HEAD_DIM

    for g in range(SSD_GROUPS):
        bg = xs_ref[:, D_SSD + g * SSD_STATE:D_SSD + (g + 1) * SSD_STATE].astype(BF16)
        cg = xs_ref[:, D_SSD + (SSD_GROUPS + g) * SSD_STATE:D_SSD + (SSD_GROUPS + g + 1) * SSD_STATE].astype(BF16)
        cb = lax.dot_general(cg, bg, NT, preferred_element_type=F32)
        hs = slice(g * SSD_GW, (g + 1) * SSD_GW)
        hprev = h_ref[hs, :]
        y_off = lax.dot_general(cg, hprev.astype(BF16), NT, preferred_element_type=F32) * ecs_x[:, hs]
        for jp in range(SSD_GH // 2):
            cols = slice(g * SSD_GW + jp * LANES, g * SSD_GW + (jp + 1) * LANES)
            xp = xdt16[:, cols]
            halves = []
            for e in range(2):
                hd = g * SSD_GH + 2 * jp + e
                seg = cs[:, hd:hd + 1] - cs_t[hd:hd + 1, :]
                lmat = jnp.where(causal, jnp.exp(jnp.where(causal, seg, 0.0)), 0.0)
                halves.append(jnp.dot((cb * lmat).astype(BF16), xp, preferred_element_type=F32))
            y_diag = jnp.where(lo_half, halves[0], halves[1])
            yv = y_diag + y_off[:, jp * LANES:(jp + 1) * LANES] + dsk_ref[:, cols] * xs_ref[:, cols]
            zv = z_ref[0, :, cols]
            ext_ref[0:q, cols] = yv * _silu(zv)
        st_new = lax.dot_general(xdec[:, hs], bg, TN, preferred_element_type=F32)
        h_ref[hs, :] = hprev * rowdec[hs, :] + st_new

    for g in range(SSD_GROUPS):
        hs = slice(g * SSD_GW, (g + 1) * SSD_GW)
        yg = ext_ref[0:q, hs]
        yn = yg * lax.rsqrt(jnp.mean(yg * yg, axis=-1, keepdims=True) + EPS) * nw_ref[:, hs]
        y_ref[0, :, hs] = yn.astype(y_ref.dtype)

    @pl.when(c == pl.num_programs(1) - 1)
    def _():
        hT_ref[0] = h_ref[...]


def _dot3_left(w, v):
    h1, h2, h3 = _split3(v)
    return (jnp.dot(w, h1, preferred_element_type=F32) + jnp.dot(w, h2, preferred_element_type=F32)
            + jnp.dot(w, h3, preferred_element_type=F32))


def _ssd(zdt, xbc, state8, h0, h_off, cw, cb, dtb, alog, dsk, nw, ex, ext_t, *, t_valid):
    b, t, _ = xbc.shape
    q = SEQ_BLOCK
    hb = q // CONV_HALO
    cst = lambda r, c_: pl.BlockSpec((r, c_), lambda bi, i: (0, 0))
    return pl.pallas_call(
        functools.partial(_ssd_kernel, q=q, t_valid=t_valid),
        name="ssd_scan",
        out_shape=[jax.ShapeDtypeStruct((b, t, D_SSD), BF16), jax.ShapeDtypeStruct((b, D_SSD, SSD_STATE), F32)],
        grid=(b, t // q),
        in_specs=[pl.BlockSpec((1, q, D_SSD), lambda bi, i: (bi, i, 0)),
                  pl.BlockSpec((1, q, LANES), lambda bi, i: (bi, i, D_SSD // LANES)),
                  pl.BlockSpec((1, q, D_SSD_CONV), lambda bi, i: (bi, i, 0)),
                  pl.BlockSpec((1, CONV_HALO, D_SSD_CONV), lambda bi, i: (bi, jnp.maximum(i * hb - 1, 0), 0)),
                  pl.BlockSpec((1, CONV_HALO, D_SSD_CONV), lambda bi, i: (bi, 0, 0)),
                  pl.BlockSpec((1, D_SSD, SSD_STATE), lambda bi, i: (h_off + bi, 0, 0)),
                  cst(SSD_CONV, D_SSD_CONV), cst(1, D_SSD_CONV), cst(1, LANES), cst(1, LANES), cst(1, D_SSD),
                  cst(1, D_SSD), cst(LANES, D_SSD), cst(D_SSD, LANES)],
        out_specs=[pl.BlockSpec((1, q, D_SSD), lambda bi, i: (bi, i, 0)),
                   pl.BlockSpec((1, D_SSD, SSD_STATE), lambda bi, i: (bi, 0, 0))],
        scratch_shapes=[pltpu.VMEM((CONV_HALO + q, D_SSD_CONV), F32), pltpu.VMEM((q, D_SSD_CONV), F32),
                        pltpu.VMEM((D_SSD, SSD_STATE), F32)],
        compiler_params=_cparams("parallel", "arbitrary"),
    )(zdt, zdt, xbc, xbc, state8, h0, cw, cb, dtb, alog, dsk, nw, ex, ext_t)


def _merge_kernel(x_ref, oa_ref, op_ref, os_ref, gt_ref, g1_ref, wa_ref, wp_ref, ws_ref, wo_ref, o_ref):
    def gate(bi):
        return jax.nn.sigmoid(gt_ref[:, bi * D_MODEL:(bi + 1) * D_MODEL])
    merged = (gate(0) * jnp.dot(oa_ref[...], wa_ref[...], preferred_element_type=F32)
              + gate(1) * jnp.dot(op_ref[...], wp_ref[...], preferred_element_type=F32)
              + gate(2) * jnp.dot(os_ref[...], ws_ref[...], preferred_element_type=F32))
    o_ref[...] = x_ref[...] + g1_ref[0] * jnp.dot(merged.astype(BF16), wo_ref[...], preferred_element_type=F32)


def _merge(x, oa, op, os_, gates, mv_g1, mv_idx, wa, wp, ws, wo, tm):
    n, d = x.shape
    rb = mv_g1.shape[1]
    rowb = lambda w: pl.BlockSpec((tm, w), lambda i: (i, 0))
    cst = lambda r, c: pl.BlockSpec((r, c), lambda i: (0, 0))
    return pl.pallas_call(
        _merge_kernel,
        name="merge_out",
        out_shape=jax.ShapeDtypeStruct((n, d), F32),
        grid=(n // tm,),
        in_specs=[rowb(d), rowb(D_ATTN), rowb(D_POOL), rowb(D_SSD), rowb(N_BRANCH * D_MODEL),
                  pl.BlockSpec((1, rb, d), lambda i: (mv_idx(i), 0, 0)),
                  cst(D_ATTN, d), cst(D_POOL, d), cst(D_SSD, d), cst(d, d)],
        out_specs=rowb(d),
        compiler_params=_cparams("parallel"),
    )(x, oa, op, os_, gates, mv_g1, wa, wp, ws, wo)


FFN_TILE = D_FF // 2
FFN_ROWS = 512


def _ffn_kernel(x_ref, hx_ref, sa_ref, sg_ref, sh_ref, sc_ref, g2_ref, gn_ref, wu_ref, cwa_ref, cwg_ref, cba_ref,
                cbg_ref, wd_ref, o_ref, ta_ref, tg_ref, xin_ref, ea_ref, eg_ref, *, tb, tv_last):
    i = pl.program_id(1)
    first = i == 0
    last = i == pl.num_programs(1) - 1
    xin_ref[0:CONV_HALO] = hx_ref[0]
    xin_ref[CONV_HALO:CONV_HALO + tb] = x_ref[0]
    xin = xin_ref[...]
    y = xin * lax.rsqrt(jnp.mean(xin * xin, axis=-1, keepdims=True) + EPS) * gn_ref[...]
    h = (y * (1.0 + sc_ref[0]) + sh_ref[0]).astype(BF16)
    acc = jnp.zeros((tb, x_ref.shape[2]), F32)
    for f in range(D_FF // FFN_TILE):
        cols = slice(f * FFN_TILE, (f + 1) * FFN_TILE)

        def conv(e_ref, s_ref, t_ref, cw_ref, cb_ref, col0):
            u = jnp.dot(h, wu_ref[:, col0 + f * FFN_TILE:col0 + (f + 1) * FFN_TILE], preferred_element_type=F32)
            e_ref[...] = u
            @pl.when(first)
            def _():
                e_ref[0:CONV_HALO] = s_ref[0, :, cols]

            @pl.when(last)
            def _():
                t_ref[0, :, cols] = e_ref[tv_last:tv_last + CONV_HALO]
            out = cb_ref[:, cols]
            for j in range(FFN_CONV):
                off = CONV_HALO - (FFN_CONV - 1) + j
                out = out + e_ref[off:off + tb] * cw_ref[j:j + 1, cols]
            return out

        a = conv(ea_ref, sa_ref, ta_ref, cwa_ref, cba_ref, 0)
        g = conv(eg_ref, sg_ref, tg_ref, cwg_ref, cbg_ref, D_FF)
        acc = acc + jnp.dot((_silu(g) * a).astype(BF16), wd_ref[cols, :], preferred_element_type=F32)
    o_ref[0] = x_ref[0] + g2_ref[0] * acc


def _ffn(x, sa, sg, sh2, sc2, g2, gn, wu, cwa, cwg, cba, cbg, wd, *, tb, t_valid):
    b, t, d = x.shape
    hb = tb // CONV_HALO
    blk = pl.BlockSpec((1, tb, d), lambda bi, i: (bi, i, 0))
    halo = pl.BlockSpec((1, CONV_HALO, d), lambda bi, i: (bi, jnp.maximum(i * hb - 1, 0), 0))
    st = pl.BlockSpec((1, CONV_HALO, D_FF), lambda bi, i: (bi, 0, 0))
    seqv = pl.BlockSpec((1, 1, d), lambda bi, i: (bi, 0, 0))
    cst = lambda r, c: pl.BlockSpec((r, c), lambda bi, i: (0, 0))
    tv_last = t_valid - (t // tb - 1) * tb
    return pl.pallas_call(
        functools.partial(_ffn_kernel, tb=tb, tv_last=tv_last),
        name="conv_ffn",
        out_shape=[jax.ShapeDtypeStruct((b, t, d), F32), jax.ShapeDtypeStruct((b, CONV_HALO, D_FF), F32),
                   jax.ShapeDtypeStruct((b, CONV_HALO, D_FF), F32)],
        grid=(b, t // tb),
        in_specs=[blk, halo, st, st, seqv, seqv, seqv, cst(1, d), cst(d, 2 * D_FF), cst(FFN_CONV, D_FF),
                  cst(FFN_CONV, D_FF), cst(1, D_FF), cst(1, D_FF), cst(D_FF, d)],
        out_specs=[blk, st, st],
        scratch_shapes=[pltpu.VMEM((CONV_HALO + tb, d), F32), pltpu.VMEM((CONV_HALO + tb, FFN_TILE), F32),
                        pltpu.VMEM((CONV_HALO + tb, FFN_TILE), F32)],
        compiler_params=_cparams("parallel", "arbitrary"),
    )(x, x, sa, sg, sh2, sc2, g2, gn, wu, cwa, cwg, cba, cbg, wd)


def _pad_rows(a, rows, front=False):
    extra = rows - a.shape[1]
    if extra == 0:
        return a
    return jnp.pad(a, ((0, 0), (extra, 0) if front else (0, extra), (0, 0)))


def _rope_tables(pos):
    half = HEAD_DIM // 2
    freqs = ROPE_THETA ** (-jnp.arange(half, dtype=jnp.float32) / half)
    ang = pos.astype(jnp.float32)[:, None] * freqs[None, :]
    cos, sin = jnp.cos(ang), jnp.sin(ang)
    return jnp.concatenate([cos, cos, cos, cos], axis=-1), jnp.concatenate([-sin, sin, -sin, sin], axis=-1)


def _layer_weights(l, w_ada, b_ada, norm1, norm2, w_in, q_norm, k_norm, pool_w, pool_scale, ssd_conv_w, ssd_conv_b,
                   ssd_dt_bias, ssd_a_log, ssd_d, ssd_norm, w_branch_attn, w_branch_pool, w_branch_ssd, w_out,
                   ffn_up, ffn_conv_w, ffn_conv_b, ffn_down):
    o = _OFFS
    win = w_in[l]
    col = lambda a, b: win[:, a:b]
    zpad = lambda n: jnp.zeros((D_MODEL, n), win.dtype)
    row = lambda v: v.reshape(1, -1).astype(F32)
    lanepad = lambda v: jnp.pad(v.astype(F32), (0, LANES - v.shape[0])).reshape(1, LANES)
    return dict(
        w_ada=w_ada[l].astype(BF16), b_ada=row(b_ada[l]), norm1=row(norm1[l]), norm2=row(norm2[l]),
        w_attn=jnp.concatenate([col(o[0], o[6]), zpad(LANES - IDX_DIM - IDX_HEADS)], axis=1).astype(BF16),
        w_pool=col(o[6], o[7]).astype(BF16),
        w_zdt=jnp.concatenate([col(o[7], o[8]), col(o[9], o[10]), zpad(LANES - SSD_HEADS)], axis=1).astype(BF16),
        w_xbc=col(o[8], o[9]).astype(BF16),
        w_gate=col(o[10], o[11]).astype(BF16),
        qn=row(jnp.tile(q_norm[l], LANES // HEAD_DIM)), kn=row(jnp.tile(k_norm[l], LANES // HEAD_DIM)),
        pool_w=pool_w[l].astype(BF16), pool_scale=row(pool_scale[l]),
        cw=ssd_conv_w[l].astype(F32), cb=row(ssd_conv_b[l]), dtb=lanepad(ssd_dt_bias[l]), alog=lanepad(ssd_a_log[l]),
        dsk=row(jnp.repeat(ssd_d[l], SSD_HEAD_DIM)), nw=row(ssd_norm[l]),
        wa=w_branch_attn[l].astype(BF16), wp=w_branch_pool[l].astype(BF16), ws=w_branch_ssd[l].astype(BF16),
        wo=w_out[l].astype(BF16),
        w_up=ffn_up[l].astype(BF16),
        cwa=ffn_conv_w[l][:, :D_FF].astype(F32), cwg=ffn_conv_w[l][:, D_FF:].astype(F32),
        cba=row(ffn_conv_b[l][:D_FF]), cbg=row(ffn_conv_b[l][D_FF:]), wd=ffn_down[l].astype(BF16))


def _group_layer(x, mod, lw, consts, *, pos0, past, states):
    b, t, d = x.shape
    n = b * t
    tp = -(-t // SEQ_BLOCK) * SEQ_BLOCK
    per_seq = t % SEQ_BLOCK == 0
    tm = min(ROW_TILE, t) if per_seq else n
    sh1, sc1, g1, sh2, sc2, g2 = [mod[:, i * d:(i + 1) * d] for i in range(6)]
    pos = pos0 + jnp.arange(t)
    cos_t, sin_t = _rope_tables(pos)
    if per_seq:
        view = lambda m: m.reshape(b, 1, d)
        mv_idx = lambda i: i // (t // tm)
        tab = lambda a: a.reshape(t // tm, tm, LANES)
        tab_idx = lambda i: i % (t // tm)
    else:
        view = lambda m: jnp.repeat(m, t, axis=0).reshape(1, n, d)
        mv_idx = lambda i: 0
        tab = lambda a: jnp.tile(a, (b, 1)).reshape(1, n, LANES)
        tab_idx = lambda i: 0
    xf = x.reshape(n, d)
    nm = lambda w: _norm_matmul(xf, view(sh1), view(sc1), mv_idx, lw['norm1'], w, tm)
    pa, pool_u, zdt, xbc, gates = nm(lw['w_attn']), nm(lw['w_pool']), nm(lw['w_zdt']), nm(lw['w_xbc']), nm(lw['w_gate'])

    q_r, k_r, qi_r, ki2, wi_s = _attn_prep(pa, tab(cos_t), tab(sin_t), tab_idx, lw['qn'], lw['kn'], consts['seg'], tm)
    v_new = pa[:, D_ATTN + D_KV:D_ATTN + 2 * D_KV]
    seq3 = lambda a: a.reshape(b, t, a.shape[-1])
    if past is None:
        tq, tk, kw = ATTN_Q_ROWS, min(ATTN_KEYS, t), LANES
        k_all, v_all, ki_all = seq3(k_r).astype(BF16), seq3(v_new).astype(BF16), seq3(ki2).astype(BF16)
        s_valid = t
        qpad = lambda a: seq3(a)
    else:
        tq, kw = 2 * SUBLANES, LANES
        page_table, pool_off, ck, cv, ci = past
        k_all, v_all, ki_all = _page_gather(page_table, pool_off, ck, cv, ci, seq3(k_r), seq3(v_new), seq3(ki2))
        tk = min(ATTN_KEYS_DECODE, k_all.shape[1] // 2)
        s_valid = pos0 + t
        qpad = lambda a: _pad_rows(seq3(a), tq)
    assert k_all.shape[1] % tk == 0
    kk = min(INDEX_TOPK, s_valid // 4)
    o_attn = _attention(qpad(q_r), qpad(qi_r), qpad(wi_s), k_all, v_all, ki_all,
                        tq=tq, tk=tk, kw=kw, pos0=pos0, s_valid=s_valid, kk=kk)[:, :t].reshape(n, D_ATTN)

    st_pool, st_sconv, (st_ssd, ssd_off), st_fconv = states
    seqp = lambda a: _pad_rows(seq3(a), tp)
    o_pool = _pool(seqp(pool_u), _pad_rows(st_pool, POOL_HALO, front=True), lw['pool_w'], lw['pool_scale'], pos0=pos0)
    o_ssd, ssd_state = _ssd(seqp(zdt), seqp(xbc), _pad_rows(st_sconv, CONV_HALO, front=True),
                            st_ssd, ssd_off, lw['cw'], lw['cb'], lw['dtb'], lw['alog'], lw['dsk'],
                            lw['nw'], consts['ex'], consts['ext_t'], t_valid=t)
    unpad = lambda a: a[:, :t].reshape(n, a.shape[-1])
    x1 = _merge(xf, o_attn, unpad(o_pool), unpad(o_ssd), gates, view(g1), mv_idx,
                lw['wa'], lw['wp'], lw['ws'], lw['wo'], tm)

    st_f = _pad_rows(st_fconv, CONV_HALO, front=True)
    per_b = lambda m: m.reshape(b, 1, d)
    x2, tail_a, tail_g = _ffn(seqp(x1), st_f[:, :, :D_FF], st_f[:, :, D_FF:], per_b(sh2), per_b(sc2), per_b(g2),
                              lw['norm2'], lw['w_up'], lw['cwa'], lw['cwg'], lw['cba'], lw['cbg'], lw['wd'],
                              tb=min(FFN_ROWS, tp), t_valid=t)
    x2 = x2[:, :t]
    fconv_state = jnp.concatenate([tail_a, tail_g], axis=-1)[:, -(FFN_CONV - 1):]

    def tail(prev, news, keep):
        last = jnp.concatenate([seq3(a)[:, -min(keep, t):] for a in news], axis=-1)
        return jnp.concatenate([prev, last], axis=1)[:, -keep:]

    new_states = (seq3(k_r).reshape(b, t, N_KV_HEADS, HEAD_DIM), seq3(v_new).reshape(b, t, N_KV_HEADS, HEAD_DIM),
                  seq3(ki2)[:, :, :IDX_DIM], tail(st_pool, [pool_u], POOL_CTX), tail(st_sconv, [xbc], SSD_CONV - 1),
                  ssd_state.reshape(b, SSD_HEADS, SSD_HEAD_DIM, SSD_STATE), fconv_state)
    return x2, new_states


def kernel(x_prompt, x_sample, c_prompt, c_sample, cache_k, cache_v, cache_kidx, page_table, state_pool, state_ssd_conv, state_ssd, state_ffn_conv, w_ada, b_ada, norm1, norm2, w_in, q_norm, k_norm, pool_w, pool_scale, ssd_conv_w, ssd_conv_b, ssd_dt_bias, ssd_a_log, ssd_d, ssd_norm, w_branch_attn, w_branch_pool, w_branch_ssd, w_out, ffn_up, ffn_conv_w, ffn_conv_b, ffn_down):
    bp, tp, d = x_prompt.shape
    bs, ts, _ = x_sample.shape
    depth = w_ada.shape[0]
    n_pool, page = cache_k.shape[1], cache_k.shape[2]
    past_len = page_table.shape[1] * page
    dt = x_prompt.dtype

    r = lax.broadcasted_iota(jnp.int32, (LANES, LANES), 0)
    c = lax.broadcasted_iota(jnp.int32, (LANES, LANES), 1)
    hr = lax.broadcasted_iota(jnp.int32, (LANES, D_SSD), 0)
    hc = lax.broadcasted_iota(jnp.int32, (LANES, D_SSD), 1)
    ex = (hr == hc // SSD_HEAD_DIM).astype(BF16)
    consts = dict(seg=(r // HEAD_DIM == c // HEAD_DIM).astype(BF16), ex=ex, ext_t=ex.T)

    c_all = jnp.concatenate([c_prompt, c_sample], axis=0)
    rows = -(-c_all.shape[0] // SUBLANES) * SUBLANES
    c_pad = jnp.pad(c_all, ((0, rows - c_all.shape[0]), (0, 0)))

    ck_t = jnp.transpose(cache_k, (0, 1, 3, 4, 2)).reshape(depth * n_pool, D_KV, page)
    cv_t = jnp.transpose(cache_v, (0, 1, 3, 4, 2)).reshape(depth * n_pool, D_KV, page)
    ci_t = jnp.transpose(cache_kidx, (0, 1, 3, 2)).reshape(depth * n_pool, IDX_DIM, page)

    ssd_all = state_ssd.reshape(depth * bs, D_SSD, SSD_STATE)
    zeros = lambda *s: jnp.zeros(s, dt)
    xp, xs = x_prompt, x_sample
    new_p, new_s = [[] for _ in range(7)], [[] for _ in range(7)]
    for l in range(depth):
        lw = _layer_weights(l, w_ada, b_ada, norm1, norm2, w_in, q_norm, k_norm, pool_w, pool_scale, ssd_conv_w,
                            ssd_conv_b, ssd_dt_bias, ssd_a_log, ssd_d, ssd_norm, w_branch_attn, w_branch_pool,
                            w_branch_ssd, w_out, ffn_up, ffn_conv_w, ffn_conv_b, ffn_down)
        mod = _mod(c_pad, lw['w_ada'], lw['b_ada'])
        xp, st_p = _group_layer(
            xp, mod[:bp], lw, consts, pos0=0, past=None,
            states=(zeros(bp, POOL_CTX, D_POOL), zeros(bp, SSD_CONV - 1, D_SSD_CONV),
                    (zeros(bp, D_SSD, SSD_STATE), 0), zeros(bp, FFN_CONV - 1, 2 * D_FF)))
        past = (page_table, l * n_pool, ck_t, cv_t, ci_t)
        xs, st_s = _group_layer(xs, mod[bp:bp + bs], lw, consts, pos0=past_len, past=past,
                                states=(state_pool[l], state_ssd_conv[l], (ssd_all, l * bs), state_ffn_conv[l]))
        for j in range(7):
            new_p[j].append(st_p[j])
            new_s[j].append(st_s[j])
    outs_p = [jnp.stack(s) for s in new_p]
    outs_s = [jnp.stack(s) for s in new_s]
    return (xp, xs, *outs_p, *outs_s)
```

```python
import functools

import numpy as np
import jax
import jax.numpy as jnp
from jax import lax
from jax.experimental import pallas as pl
from jax.experimental.pallas import tpu as pltpu

D_MODEL = 1024
N_HEADS = 16
N_KV_HEADS = 4
HEAD_DIM = 64
D_ATTN = N_HEADS * HEAD_DIM
D_KV = N_KV_HEADS * HEAD_DIM
IDX_HEADS = 8
IDX_DIM = 64
INDEX_TOPK = 256
ROPE_THETA = 10000.0
POOL_WINDOWS = (2, 4, 8, 16)
D_POOL = D_MODEL
POOL_GROUP = D_POOL // len(POOL_WINDOWS)
POOL_CTX = max(POOL_WINDOWS) - 1
D_SSD = 2 * D_MODEL
SSD_HEAD_DIM = 64
SSD_HEADS = D_SSD // SSD_HEAD_DIM
SSD_GROUPS = 4
SSD_STATE = 128
SSD_CONV = 4
D_SSD_CONV = D_SSD + 2 * SSD_GROUPS * SSD_STATE
D_FF = 2816
FFN_CONV = 3
N_BRANCH = 3
EPS = 1e-6
_SPLITS = (D_ATTN, D_KV, D_KV, IDX_HEADS * IDX_DIM, IDX_DIM, IDX_HEADS, D_POOL, D_SSD, D_SSD_CONV,
           SSD_HEADS, N_BRANCH * D_MODEL)
_OFFS = tuple(int(v) for v in np.cumsum((0,) + _SPLITS))

F32 = jnp.float32
BF16 = jnp.bfloat16
LANES = 128
SUBLANES = 8
SEQ_BLOCK = 128
ROW_TILE = 512
ATTN_Q_ROWS = 256
ATTN_KEYS = 256
ATTN_KEYS_DECODE = 1024
VMEM_LIMIT = 56 * 1024 * 1024
NEG = -1e30
LOG2E = 1.4426950408889634
N_BISECT = 20
NT = (((1,), (1,)), ((), ()))
TN = (((0,), (0,)), ((), ()))


def _cparams(*sem):
    return pltpu.CompilerParams(dimension_semantics=sem, vmem_limit_bytes=VMEM_LIMIT)


def _silu(v):
    return (0.5 * v) * (1.0 + jnp.tanh(0.5 * v))


def _split3(v):
    h1 = v.astype(BF16)
    r1 = v - h1.astype(F32)
    h2 = r1.astype(BF16)
    h3 = (r1 - h2.astype(F32)).astype(BF16)
    return h1, h2, h3


def _dot3(v, w):
    h1, h2, h3 = _split3(v)
    return (jnp.dot(h1, w, preferred_element_type=F32) + jnp.dot(h2, w, preferred_element_type=F32)
            + jnp.dot(h3, w, preferred_element_type=F32))


def _mod_kernel(c_ref, w_ref, b_ref, o_ref):
    s = _silu(c_ref[...])
    o_ref[...] = jnp.dot(s.astype(BF16), w_ref[...], preferred_element_type=F32) + b_ref[...]


def _mod(c_pad, w, b):
    rows, d = c_pad.shape
    n = w.shape[1]
    tn = n // 4
    return pl.pallas_call(
        _mod_kernel,
        name="adaln_mod",
        out_shape=jax.ShapeDtypeStruct((rows, n), F32),
        grid=(n // tn,),
        in_specs=[pl.BlockSpec((rows, d), lambda j: (0, 0)),
                  pl.BlockSpec((d, tn), lambda j: (0, j)),
                  pl.BlockSpec((1, tn), lambda j: (0, j))],
        out_specs=pl.BlockSpec((rows, tn), lambda j: (0, j)),
        compiler_params=_cparams("parallel"),
    )(c_pad, w, b)


def _norm_matmul_kernel(x_ref, sh_ref, sc_ref, g_ref, w_ref, o_ref):
    x = x_ref[...]
    y = x * lax.rsqrt(jnp.mean(x * x, axis=-1, keepdims=True) + EPS) * g_ref[...]
    h = y * (1.0 + sc_ref[0]) + sh_ref[0]
    o_ref[...] = jnp.dot(h.astype(BF16), w_ref[...], preferred_element_type=F32).astype(o_ref.dtype)


def _norm_matmul(x, mv_sh, mv_sc, mv_idx, g, w, tm):
    n, d = x.shape
    nout = w.shape[1]
    rb = mv_sh.shape[1]
    return pl.pallas_call(
        _norm_matmul_kernel,
        name="norm_matmul",
        out_shape=jax.ShapeDtypeStruct((n, nout), F32),
        grid=(n // tm,),
        in_specs=[pl.BlockSpec((tm, d), lambda i: (i, 0)),
                  pl.BlockSpec((1, rb, d), lambda i: (mv_idx(i), 0, 0)),
                  pl.BlockSpec((1, rb, d), lambda i: (mv_idx(i), 0, 0)),
                  pl.BlockSpec((1, d), lambda i: (0, 0)),
                  pl.BlockSpec((d, nout), lambda i: (0, 0))],
        out_specs=pl.BlockSpec((tm, nout), lambda i: (i, 0)),
        compiler_params=_cparams("parallel"),
    )(x, mv_sh, mv_sc, g, w)


def _attn_prep_kernel(q_ref, k_ref, qi_ref, kw_ref, cos_ref, sin_ref, qn_ref, kn_ref, seg_ref,
                      qo_ref, ko_ref, qio_ref, kio_ref, wio_ref):
    tm = q_ref.shape[0]
    lane = lax.broadcasted_iota(jnp.int32, (tm, LANES), 1)
    first_half = (lane % HEAD_DIM) < (HEAD_DIM // 2)
    cos = cos_ref[0]
    sin = sin_ref[0]
    seg = seg_ref[...]

    def rope(v):
        up = pltpu.roll(v, LANES - HEAD_DIM // 2, 1)
        dn = pltpu.roll(v, HEAD_DIM // 2, 1)
        return v * cos + jnp.where(first_half, up, dn) * sin

    def headnorm(v, gain):
        sq = v * v
        hi = sq.astype(BF16)
        lo = (sq - hi.astype(F32)).astype(BF16)
        ss = jnp.dot(hi, seg, preferred_element_type=F32) + jnp.dot(lo, seg, preferred_element_type=F32)
        return v * lax.rsqrt(ss * (1.0 / HEAD_DIM) + EPS) * gain

    qn = qn_ref[...]
    kn = kn_ref[...]
    for j in range(D_ATTN // LANES):
        sl = slice(j * LANES, (j + 1) * LANES)
        qo_ref[:, sl] = rope(headnorm(q_ref[:, sl], qn)) * (HEAD_DIM ** -0.5 * LOG2E)
    for j in range(D_KV // LANES):
        sl = slice(j * LANES, (j + 1) * LANES)
        ko_ref[:, sl] = rope(headnorm(k_ref[:, sl], kn))
    for j in range(IDX_HEADS * IDX_DIM // LANES):
        sl = slice(j * LANES, (j + 1) * LANES)
        qio_ref[:, sl] = rope(qi_ref[:, sl])
    kw = kw_ref[...]
    kr = rope(kw)
    kio_ref[...] = jnp.where(lane < IDX_DIM, kr, pltpu.roll(kr, IDX_DIM, 1))
    wio_ref[...] = pltpu.roll(kw, LANES - IDX_DIM, 1) * (IDX_HEADS ** -0.5 * IDX_DIM ** -0.5)


def _attn_prep(pa, cos_t, sin_t, tab_idx, qn, kn, seg, tm):
    n = pa.shape[0]
    blk = lambda w, c: pl.BlockSpec((tm, w), lambda i: (i, c))
    tab = pl.BlockSpec((1, tm, LANES), lambda i: (tab_idx(i), 0, 0))
    cst = lambda r, c: pl.BlockSpec((r, c), lambda i: (0, 0))
    return pl.pallas_call(
        _attn_prep_kernel,
        name="attn_prep",
        out_shape=[jax.ShapeDtypeStruct((n, D_ATTN), F32), jax.ShapeDtypeStruct((n, D_KV), F32),
                   jax.ShapeDtypeStruct((n, IDX_HEADS * IDX_DIM), F32),
                   jax.ShapeDtypeStruct((n, LANES), F32), jax.ShapeDtypeStruct((n, LANES), F32)],
        grid=(n // tm,),
        in_specs=[blk(D_ATTN, 0), blk(D_KV, D_ATTN // D_KV), blk(IDX_HEADS * IDX_DIM, 3), blk(LANES, 16),
                  tab, tab, cst(1, LANES), cst(1, LANES), cst(LANES, LANES)],
        out_specs=[blk(D_ATTN, 0), blk(D_KV, 0), blk(IDX_HEADS * IDX_DIM, 0), blk(LANES, 0), blk(LANES, 0)],
        compiler_params=_cparams("parallel"),
    )(pa, pa, pa, pa, cos_t, sin_t, qn, kn, seg)


GATHER_PAGES = 16


def _page_gather_kernel(pt_ref, *refs, n_steps, pg, page):
    ins, outs = refs[:3 * pg + 3], refs[3 * pg + 3:]
    past = pl.program_id(1) < n_steps
    for t in range(3):
        new_ref, out_ref = ins[3 * pg + t], outs[t]
        for i in range(pg):
            rows = slice(i * page, (i + 1) * page)
            tile = ins[t * pg + i][0]
            if t == 2:
                tile = jnp.concatenate([tile, tile], axis=0)
            out_ref[0, rows, :] = jnp.where(past, tile.T, new_ref[0, rows, :]).astype(out_ref.dtype)


def _page_gather(page_table, pool_off, ck, cv, ci, nk, nv, ni):
    bd, n_pages = page_table.shape
    page = ck.shape[2]
    pg = max(g for g in range(1, GATHER_PAGES + 1) if n_pages % g == 0)
    n_steps = n_pages // pg
    nk, nv, ni = [_pad_rows(a, pg * page) for a in (nk, nv, ni)]

    def cache(w, i):
        return pl.BlockSpec((1, w, page),
                            lambda b, p, pt: (pool_off + pt[b, jnp.minimum(p, n_steps - 1) * pg + i], 0, 0))

    new = lambda w: pl.BlockSpec((1, pg * page, w), lambda b, p, pt: (b, 0, 0))
    out = lambda w: pl.BlockSpec((1, pg * page, w), lambda b, p, pt: (b, p, 0))
    s_pad = (n_steps + 1) * pg * page
    in_specs = ([cache(D_KV, i) for i in range(pg)] + [cache(D_KV, i) for i in range(pg)]
                + [cache(IDX_DIM, i) for i in range(pg)] + [new(D_KV), new(D_KV), new(2 * IDX_DIM)])
    return pl.pallas_call(
        functools.partial(_page_gather_kernel, n_steps=n_steps, pg=pg, page=page),
        out_shape=[jax.ShapeDtypeStruct((bd, s_pad, D_KV), BF16), jax.ShapeDtypeStruct((bd, s_pad, D_KV), BF16),
                   jax.ShapeDtypeStruct((bd, s_pad, 2 * IDX_DIM), BF16)],
        grid_spec=pltpu.PrefetchScalarGridSpec(
            num_scalar_prefetch=1, grid=(bd, n_steps + 1), in_specs=in_specs,
            out_specs=[out(D_KV), out(D_KV), out(2 * IDX_DIM)]),
        compiler_params=_cparams("parallel", "parallel"),
        name="page_gather",
    )(page_table, *([ck] * pg), *([cv] * pg), *([ci] * pg), nk, nv, ni)


GQA = N_HEADS // N_KV_HEADS


def _attn_kernel(q_ref, qi_ref, wi_ref, k_ref, v_ref, ki_ref, o_ref,
                 sc_ref, qh_ref, qim_ref, wb_ref, m_ref, l_ref, acc_ref, *, tq, tk, kw, nblk, n_static, pos0, s_valid, kk):
    q0 = pl.program_id(1) * tq
    nch = tk // LANES
    ngrp = D_KV // kw
    hpg = N_HEADS // ngrp
    lane = lax.broadcasted_iota(jnp.int32, (tq, LANES), 1)
    row = lax.broadcasted_iota(jnp.int32, (tq, LANES), 0)
    qpos = pos0 + q0 + row
    lo_half = lane < HEAD_DIM
    n_kv = jnp.minimum(nblk, (pos0 + q0 + tq - 1) // tk + 1)
    kkf = float(kk)
    chunk = lambda a, j: a[..., j * LANES:(j + 1) * LANES]
    rg = min(tq, LANES)

    def kv_loop(body, init):
        if n_static is None:
            return lax.fori_loop(0, n_kv, body, init)
        for c in range(n_static):
            init = body(c, init)
        return init

    def key_off(c):
        return c * tk if n_static is not None else pl.multiple_of(c * tk, tk)

    def bcast(col):
        return jnp.broadcast_to(col, col.shape[:-1] + (LANES,))

    wi = wi_ref[0]
    for j in range(IDX_HEADS // 2):
        t = qi_ref[0, :, j * LANES:(j + 1) * LANES]
        qim_ref[2 * j * tq:(2 * j + 1) * tq] = jnp.where(lo_half, t, 0.0).astype(BF16)
        qim_ref[(2 * j + 1) * tq:(2 * j + 2) * tq] = jnp.where(lo_half, 0.0, t).astype(BF16)
    for h in range(IDX_HEADS):
        wb_ref[h] = bcast(wi[:, h:h + 1])
    def kv_tile(h):
        kvh = h // GQA
        return (kvh * HEAD_DIM % kw) // LANES, kvh % 2

    if kw > LANES:
        qh_ref[...] = jnp.zeros(qh_ref.shape, BF16)
    for h in range(N_HEADS):
        t = q_ref[0, :, (h // 2) * LANES:(h // 2 + 1) * LANES]
        tile, kv_half = kv_tile(h)
        if h % 2 != kv_half:
            t = pltpu.roll(t, HEAD_DIM, 1)
        keep = lo_half if kv_half == 0 else jnp.logical_not(lo_half)
        hh = h % hpg
        qh_ref[h // hpg, hh * tq:(hh + 1) * tq, tile * LANES:(tile + 1) * LANES] = jnp.where(keep, t, 0.0).astype(BF16)
    m_ref[...] = jnp.full(m_ref.shape, NEG, F32)
    l_ref[...] = jnp.zeros(l_ref.shape, F32)
    acc_ref[...] = jnp.zeros(acc_ref.shape, F32)

    def p1(c, carry):
        mn, mx = carry
        kid = ki_ref[0, pl.ds(key_off(c), tk), :]
        y = lax.dot_general(qim_ref[...], kid, NT, preferred_element_type=F32)
        y = jnp.maximum(y, 0.0).reshape(IDX_HEADS, tq, tk)
        for j in range(nch):
            acc = wb_ref[0] * chunk(y[0], j)
            for h in range(1, IDX_HEADS):
                acc = acc + wb_ref[h] * chunk(y[h], j)
            kpos = c * tk + j * LANES + lane
            vis = jnp.logical_and(kpos <= qpos, kpos < s_valid)
            sc_ref[c, :, j * LANES:(j + 1) * LANES] = jnp.where(vis, acc, -jnp.inf)
            mn = jnp.minimum(mn, jnp.where(vis, acc, jnp.inf))
            mx = jnp.maximum(mx, jnp.where(vis, acc, -jnp.inf))
        return mn, mx

    mn, mx = kv_loop(p1, (jnp.full((tq, LANES), jnp.inf, F32), jnp.full((tq, LANES), -jnp.inf, F32)))
    rmin = bcast(jnp.min(mn, axis=1, keepdims=True))
    rmax = bcast(jnp.max(mx, axis=1, keepdims=True))

    def row_pass(step, init, reduce):
        outs = []
        for r in range(tq // rg):
            rows = slice(r * rg, (r + 1) * rg)

            def body(c, acc, rows=rows):
                x = sc_ref[c, rows, :]
                for j in range(nch):
                    acc = step(acc, chunk(x, j), rows)
                return acc
            outs.append(bcast(reduce(kv_loop(body, jnp.full((rg, LANES), init, F32)), axis=1, keepdims=True)))
        return outs[0] if len(outs) == 1 else jnp.concatenate(outs, axis=0)

    def count_gt(t_b):
        return row_pass(lambda acc, xj, rows: acc + jnp.where(xj > t_b[rows], 1.0, 0.0), 0.0, jnp.sum)

    def next_above(t_b):
        return row_pass(lambda acc, xj, rows: jnp.minimum(acc, jnp.where(xj > t_b[rows], xj, jnp.inf)), jnp.inf, jnp.min)

    at_min = jnp.minimum(qpos + 1, s_valid).astype(F32) < kkf
    below = rmin - jnp.maximum((rmax - rmin) * 2.0 ** -8, jnp.abs(rmin) * 2.0 ** -20 + 1e-30)

    def bisect(i, carry):
        lo, hi = carry
        mid = lo + 0.5 * (hi - lo)
        up = count_gt(mid) >= kkf
        return jnp.where(up, mid, lo), jnp.where(up, hi, mid)

    lo, hi = lax.fori_loop(0, N_BISECT, bisect, (below, rmax))
    tau = next_above(lo)
    ct = count_gt(tau)

    def pending(ct_):
        return jnp.logical_and(ct_ >= kkf, jnp.logical_not(at_min))

    def w_cond(st):
        return jnp.max(jnp.where(pending(st[2]), 1.0, 0.0)) > 0.5

    def w_body(st):
        lo_, tau_, ct_ = st
        need = pending(ct_)
        lo2 = jnp.where(need, tau_, lo_)
        tau2 = next_above(lo2)
        ct2 = count_gt(tau2)
        return lo2, jnp.where(need, tau2, tau_), jnp.where(need, ct2, ct_)

    lo, tau, ct = lax.while_loop(w_cond, w_body, (lo, tau, ct))
    tau = jnp.where(at_min, rmin, tau)
    m_need = jnp.where(at_min, kkf, kkf - ct)

    rk = lax.broadcasted_iota(jnp.int32, (tk, tk), 0)
    ck = lax.broadcasted_iota(jnp.int32, (tk, tk), 1)
    before = jnp.where(rk < ck, 1.0, 0.0).astype(BF16)

    def p2(c, cum):
        x = sc_ref[c]
        eqs = [chunk(x, j) == tau for j in range(nch)]
        eqf = [jnp.where(e, 1.0, 0.0) for e in eqs]
        pre = jnp.dot(jnp.concatenate(eqf, axis=-1).astype(BF16), before, preferred_element_type=F32)
        bias = [jnp.where(chunk(x, j) > tau, 0.0,
                          jnp.where(eqs[j], jnp.where(chunk(pre, j) + cum < m_need, 0.0, NEG), NEG))
                for j in range(nch)]
        tot = eqf[0]
        for j in range(1, nch):
            tot = tot + eqf[j]
        cum = cum + bcast(jnp.sum(tot, axis=1, keepdims=True))
        koff = key_off(c)
        for p in range(ngrp):
            kp = k_ref[0, pl.ds(koff, tk), p * kw:(p + 1) * kw]
            vp = v_ref[0, pl.ds(koff, tk), p * kw:(p + 1) * kw]
            s = lax.dot_general(qh_ref[p], kp, NT, preferred_element_type=F32).reshape(hpg, tq, tk)
            sm = [chunk(s, j) + bias[j][None] for j in range(nch)]
            mx_ = sm[0]
            for j in range(1, nch):
                mx_ = jnp.maximum(mx_, sm[j])
            m_old = m_ref[p]
            m_new = jnp.maximum(m_old, bcast(jnp.max(mx_, axis=-1, keepdims=True)))
            pe = [jnp.exp2(sm[j] - m_new) for j in range(nch)]
            ps = pe[0]
            for j in range(1, nch):
                ps = ps + pe[j]
            alpha = jnp.exp2(m_old - m_new)
            l_ref[p] = alpha * l_ref[p] + bcast(jnp.sum(ps, axis=-1, keepdims=True))
            pv = jnp.dot(jnp.concatenate(pe, axis=-1).reshape(hpg * tq, tk).astype(BF16), vp,
                         preferred_element_type=F32)
            alpha = alpha.reshape(hpg * tq, LANES)
            acc_ref[p] = jnp.concatenate([alpha] * (kw // LANES), axis=-1) * acc_ref[p] + pv
            m_ref[p] = m_new
        return cum

    kv_loop(p2, jnp.zeros((tq, LANES), F32))

    def head_out(h):
        p, hh = h // hpg, h % hpg
        tile, kv_half = kv_tile(h)
        r = acc_ref[p, hh * tq:(hh + 1) * tq, tile * LANES:(tile + 1) * LANES] / l_ref[p, hh]
        if kv_half != h % 2:
            r = pltpu.roll(r, HEAD_DIM, 1)
        return r

    for j in range(N_HEADS // 2):
        o_ref[0, :, j * LANES:(j + 1) * LANES] = jnp.where(lo_half, head_out(2 * j), head_out(2 * j + 1)).astype(o_ref.dtype)


def _attention(q, qi, wi, k_all, v_all, ki2, *, tq, tk, kw, pos0, s_valid, kk):
    b, tq_pad, _ = q.shape
    s_pad = k_all.shape[1]
    nblk = s_pad // tk
    ngrp = D_KV // kw
    hpg = N_HEADS // ngrp
    n_static = min(nblk, (pos0 + tq - 1) // tk + 1) if tq_pad == tq else None
    qblk = lambda w: pl.BlockSpec((1, tq, w), lambda bi, qi_: (bi, qi_, 0))
    seq = lambda w: pl.BlockSpec((1, s_pad, w), lambda bi, qi_: (bi, 0, 0))
    return pl.pallas_call(
        functools.partial(_attn_kernel, tq=tq, tk=tk, kw=kw, nblk=nblk, n_static=n_static, pos0=pos0, s_valid=s_valid,
                          kk=kk),
        out_shape=jax.ShapeDtypeStruct((b, tq_pad, D_ATTN), BF16),
        grid=(b, tq_pad // tq),
        in_specs=[qblk(D_ATTN), qblk(IDX_HEADS * IDX_DIM), qblk(LANES), seq(D_KV), seq(D_KV), seq(LANES)],
        out_specs=qblk(D_ATTN),
        scratch_shapes=[pltpu.VMEM((nblk, tq, tk), F32), pltpu.VMEM((ngrp, hpg * tq, kw), BF16),
                        pltpu.VMEM((IDX_HEADS * tq, LANES), BF16), pltpu.VMEM((IDX_HEADS, tq, LANES), F32),
                        pltpu.VMEM((ngrp, hpg, tq, LANES), F32), pltpu.VMEM((ngrp, hpg, tq, LANES), F32),
                        pltpu.VMEM((ngrp, hpg * tq, kw), F32)],
        compiler_params=_cparams("parallel", "parallel"),
        name="sparse_attention",
    )(q, qi, wi, k_all, v_all, ki2)


POOL_HALO = 16


def _pool_kernel(u_ref, halo_ref, st_ref, w_ref, scale_ref, o_ref, ext_ref, *, tb, pos0):
    i = pl.program_id(1)
    ext_ref[0:POOL_HALO] = jnp.where(i == 0, st_ref[0], halo_ref[0])
    ext_ref[POOL_HALO:POOL_HALO + tb] = u_ref[0]
    pos = pos0 + i * tb + lax.broadcasted_iota(jnp.int32, (tb, 1), 0)
    for gi, w in enumerate(POOL_WINDOWS):
        sl = slice(gi * POOL_GROUP, (gi + 1) * POOL_GROUP)
        tot = ext_ref[POOL_HALO:POOL_HALO + tb, sl]
        for k in range(1, w):
            tot = tot + ext_ref[POOL_HALO - k:POOL_HALO - k + tb, sl]
        cnt = jnp.minimum(w, pos + 1).astype(F32)
        pooled = tot / cnt - u_ref[0, :, sl]
        mixed = jnp.dot(pooled.astype(BF16), w_ref[gi], preferred_element_type=F32) * scale_ref[:, sl]
        o_ref[0, :, sl] = mixed.astype(o_ref.dtype)


def _pool(u, state16, w, scale, *, pos0):
    b, t, c = u.shape
    tb = SEQ_BLOCK
    hb = tb // POOL_HALO
    return pl.pallas_call(
        functools.partial(_pool_kernel, tb=tb, pos0=pos0),
        name="pool_mix",
        out_shape=jax.ShapeDtypeStruct((b, t, c), BF16),
        grid=(b, t // tb),
        in_specs=[pl.BlockSpec((1, tb, c), lambda bi, i: (bi, i, 0)),
                  pl.BlockSpec((1, POOL_HALO, c), lambda bi, i: (bi, jnp.maximum(i * hb - 1, 0), 0)),
                  pl.BlockSpec((1, POOL_HALO, c), lambda bi, i: (bi, 0, 0)),
                  pl.BlockSpec((len(POOL_WINDOWS), POOL_GROUP, POOL_GROUP), lambda bi, i: (0, 0, 0)),
                  pl.BlockSpec((1, c), lambda bi, i: (0, 0))],
        out_specs=pl.BlockSpec((1, tb, c), lambda bi, i: (bi, i, 0)),
        scratch_shapes=[pltpu.VMEM((POOL_HALO + tb, c), F32)],
        compiler_params=_cparams("parallel", "parallel"),
    )(u, u, state16, w, scale)


CONV_HALO = SUBLANES
SSD_GH = SSD_HEADS // SSD_GROUPS
SSD_GW = SSD_GH * SSD_HEAD_DIM
CONV_CHUNK = 512


def _ssd_kernel(z_ref, dt_ref, xbc_ref, halo_ref, st_ref, h0_ref, cw_ref, cb_ref, dtb_ref, alog_ref, dsk_ref,
                nw_ref, ex_ref, ext_t_ref, y_ref, hT_ref, ext_ref, xs_ref, h_ref, *, q, t_valid):
    c = pl.program_id(1)

    @pl.when(c == 0)
    def _():
        h_ref[...] = h0_ref[0]

    ext_ref[0:CONV_HALO] = jnp.where(c == 0, st_ref[0], halo_ref[0])
    ext_ref[CONV_HALO:CONV_HALO + q] = xbc_ref[0]
    for ch in range(D_SSD_CONV // CONV_CHUNK):
        cc = slice(ch * CONV_CHUNK, (ch + 1) * CONV_CHUNK)
        xc = cb_ref[:, cc]
        for j in range(SSD_CONV):
            off = CONV_HALO - (SSD_CONV - 1) + j
            xc = xc + ext_ref[off:off + q, cc] * cw_ref[j:j + 1, cc]
        xs_ref[:, cc] = _silu(xc)
    x = xs_ref[:, :D_SSD]

    lane = lax.broadcasted_iota(jnp.int32, (q, LANES), 1)
    row = lax.broadcasted_iota(jnp.int32, (q, LANES), 0)
    v = dt_ref[0] + dtb_ref[...]
    dt = jnp.maximum(v, 0.0) + jnp.log1p(jnp.exp(-jnp.abs(v)))
    dt = jnp.where(jnp.logical_and(lane < SSD_HEADS, c * q + row < t_valid), dt, 0.0)
    a = dt * (-jnp.exp(alog_ref[...]))
    tri = jnp.where(lane <= row, 1.0, 0.0).astype(BF16)
    causal = lane <= row
    cs = _dot3_left(tri, a)
    cs_t = cs.T
    ex = ex_ref[...]
    cs_last = cs[q - 1:q, :]
    dt_x = _dot3(dt, ex)
    ecs_x = jnp.exp(_dot3(cs, ex))
    dec_x = jnp.exp(_dot3(cs_last - cs, ex))
    xdt = x * dt_x
    xdec = (xdt * dec_x).astype(BF16)
    xdt16 = xdt.astype(BF16)
    rowdec = jnp.exp(_dot3_left(ext_t_ref[...], jnp.broadcast_to(cs_t[:, q - 1:q], (LANES, LANES))))
    lo_half = lane < SSD_HEAD_DIM

    for g in range(SSD_GROUPS):
        bg = xs_ref[:, D_SSD + g * SSD_STATE:D_SSD + (g + 1) * SSD_STATE].astype(BF16)
        cg = xs_ref[:, D_SSD + (SSD_GROUPS + g) * SSD_STATE:D_SSD + (SSD_GROUPS + g + 1) * SSD_STATE].astype(BF16)
        cb = lax.dot_general(cg, bg, NT, preferred_element_type=F32)
        hs = slice(g * SSD_GW, (g + 1) * SSD_GW)
        hprev = h_ref[hs, :]
        y_off = lax.dot_general(cg, hprev.astype(BF16), NT, preferred_element_type=F32) * ecs_x[:, hs]
        for jp in range(SSD_GH // 2):
            cols = slice(g * SSD_GW + jp * LANES, g * SSD_GW + (jp + 1) * LANES)
            xp = xdt16[:, cols]
            halves = []
            for e in range(2):
                hd = g * SSD_GH + 2 * jp + e
                seg = cs[:, hd:hd + 1] - cs_t[hd:hd + 1, :]
                lmat = jnp.where(causal, jnp.exp(jnp.where(causal, seg, 0.0)), 0.0)
                halves.append(jnp.dot((cb * lmat).astype(BF16), xp, preferred_element_type=F32))
            y_diag = jnp.where(lo_half, halves[0], halves[1])
            yv = y_diag + y_off[:, jp * LANES:(jp + 1) * LANES] + dsk_ref[:, cols] * xs_ref[:, cols]
            zv = z_ref[0, :, cols]
            ext_ref[0:q, cols] = yv * _silu(zv)
        st_new = lax.dot_general(xdec[:, hs], bg, TN, preferred_element_type=F32)
        h_ref[hs, :] = hprev * rowdec[hs, :] + st_new

    for g in range(SSD_GROUPS):
        hs = slice(g * SSD_GW, (g + 1) * SSD_GW)
        yg = ext_ref[0:q, hs]
        yn = yg * lax.rsqrt(jnp.mean(yg * yg, axis=-1, keepdims=True) + EPS) * nw_ref[:, hs]
        y_ref[0, :, hs] = yn.astype(y_ref.dtype)

    @pl.when(c == pl.num_programs(1) - 1)
    def _():
        hT_ref[0] = h_ref[...]


def _dot3_left(w, v):
    h1, h2, h3 = _split3(v)
    return (jnp.dot(w, h1, preferred_element_type=F32) + jnp.dot(w, h2, preferred_element_type=F32)
            + jnp.dot(w, h3, preferred_element_type=F32))


def _ssd(zdt, xbc, state8, h0, h_off, cw, cb, dtb, alog, dsk, nw, ex, ext_t, *, t_valid):
    b, t, _ = xbc.shape
    q = SEQ_BLOCK
    hb = q // CONV_HALO
    cst = lambda r, c_: pl.BlockSpec((r, c_), lambda bi, i: (0, 0))
    return pl.pallas_call(
        functools.partial(_ssd_kernel, q=q, t_valid=t_valid),
        name="ssd_scan",
        out_shape=[jax.ShapeDtypeStruct((b, t, D_SSD), BF16), jax.ShapeDtypeStruct((b, D_SSD, SSD_STATE), F32)],
        grid=(b, t // q),
        in_specs=[pl.BlockSpec((1, q, D_SSD), lambda bi, i: (bi, i, 0)),
                  pl.BlockSpec((1, q, LANES), lambda bi, i: (bi, i, D_SSD // LANES)),
                  pl.BlockSpec((1, q, D_SSD_CONV), lambda bi, i: (bi, i, 0)),
                  pl.BlockSpec((1, CONV_HALO, D_SSD_CONV), lambda bi, i: (bi, jnp.maximum(i * hb - 1, 0), 0)),
                  pl.BlockSpec((1, CONV_HALO, D_SSD_CONV), lambda bi, i: (bi, 0, 0)),
                  pl.BlockSpec((1, D_SSD, SSD_STATE), lambda bi, i: (h_off + bi, 0, 0)),
                  cst(SSD_CONV, D_SSD_CONV), cst(1, D_SSD_CONV), cst(1, LANES), cst(1, LANES), cst(1, D_SSD),
                  cst(1, D_SSD), cst(LANES, D_SSD), cst(D_SSD, LANES)],
        out_specs=[pl.BlockSpec((1, q, D_SSD), lambda bi, i: (bi, i, 0)),
                   pl.BlockSpec((1, D_SSD, SSD_STATE), lambda bi, i: (bi, 0, 0))],
        scratch_shapes=[pltpu.VMEM((CONV_HALO + q, D_SSD_CONV), F32), pltpu.VMEM((q, D_SSD_CONV), F32),
                        pltpu.VMEM((D_SSD, SSD_STATE), F32)],
        compiler_params=_cparams("parallel", "arbitrary"),
    )(zdt, zdt, xbc, xbc, state8, h0, cw, cb, dtb, alog, dsk, nw, ex, ext_t)


def _merge_kernel(x_ref, oa_ref, op_ref, os_ref, gt_ref, g1_ref, wa_ref, wp_ref, ws_ref, wo_ref, o_ref):
    def gate(bi):
        return jax.nn.sigmoid(gt_ref[:, bi * D_MODEL:(bi + 1) * D_MODEL])
    merged = (gate(0) * jnp.dot(oa_ref[...], wa_ref[...], preferred_element_type=F32)
              + gate(1) * jnp.dot(op_ref[...], wp_ref[...], preferred_element_type=F32)
              + gate(2) * jnp.dot(os_ref[...], ws_ref[...], preferred_element_type=F32))
    o_ref[...] = x_ref[...] + g1_ref[0] * jnp.dot(merged.astype(BF16), wo_ref[...], preferred_element_type=F32)


def _merge(x, oa, op, os_, gates, mv_g1, mv_idx, wa, wp, ws, wo, tm):
    n, d = x.shape
    rb = mv_g1.shape[1]
    rowb = lambda w: pl.BlockSpec((tm, w), lambda i: (i, 0))
    cst = lambda r, c: pl.BlockSpec((r, c), lambda i: (0, 0))
    return pl.pallas_call(
        _merge_kernel,
        name="merge_out",
        out_shape=jax.ShapeDtypeStruct((n, d), F32),
        grid=(n // tm,),
        in_specs=[rowb(d), rowb(D_ATTN), rowb(D_POOL), rowb(D_SSD), rowb(N_BRANCH * D_MODEL),
                  pl.BlockSpec((1, rb, d), lambda i: (mv_idx(i), 0, 0)),
                  cst(D_ATTN, d), cst(D_POOL, d), cst(D_SSD, d), cst(d, d)],
        out_specs=rowb(d),
        compiler_params=_cparams("parallel"),
    )(x, oa, op, os_, gates, mv_g1, wa, wp, ws, wo)


FFN_TILE = D_FF // 2
FFN_ROWS = 512
FFN_SLOT = 16


def _ffn_kernel(x_ref, hx_ref, sa_ref, sg_ref, sh_ref, sc_ref, g2_ref, gn_ref, wu_ref, cwa_ref, cwg_ref, cba_ref,
                cbg_ref, wd_ref, o_ref, ta_ref, tg_ref, xin_ref, ea_ref, eg_ref, acc_ref, *, tb, t_valid, tail_row, slot):
    i = pl.program_id(1)
    if slot is None:
        carried_in = [(0, 0)]
        tails = [(tail_row, 0)]
    else:
        carried_in = [(CONV_HALO + k * slot, k) for k in range(tb // slot)]
        tails = [(CONV_HALO + k * slot + t_valid, k) for k in range(tb // slot)]
    xin_ref[0:CONV_HALO] = hx_ref[0]
    xin_ref[CONV_HALO:CONV_HALO + tb] = x_ref[0]
    xin = xin_ref[...]
    y = xin * lax.rsqrt(jnp.mean(xin * xin, axis=-1, keepdims=True) + EPS) * gn_ref[...]

    def rows_of(m):
        return m if m.shape[0] == 1 else jnp.concatenate([m[:CONV_HALO], m], axis=0)

    h = (y * (1.0 + rows_of(sc_ref[0])) + rows_of(sh_ref[0])).astype(BF16)
    for f in range(D_FF // FFN_TILE):
        cols = slice(f * FFN_TILE, (f + 1) * FFN_TILE)

        def conv(e_ref, s_ref, t_ref, cw_ref, cb_ref, col0):
            u = jnp.dot(h, wu_ref[:, col0 + f * FFN_TILE:col0 + (f + 1) * FFN_TILE], preferred_element_type=F32)
            e_ref[...] = u

            def take_carried():
                for row, k in carried_in:
                    e_ref[row:row + CONV_HALO] = s_ref[k, :, cols]

            def emit_tails():
                for row, k in tails:
                    t_ref[k, :, cols] = e_ref[row:row + CONV_HALO]

            if slot is None:
                pl.when(i == 0)(take_carried)
                pl.when(i == pl.num_programs(1) - 1)(emit_tails)
            else:
                take_carried()
                emit_tails()
            out = cb_ref[:, cols]
            for j in range(FFN_CONV):
                off = CONV_HALO - (FFN_CONV - 1) + j
                out = out + e_ref[off:off + tb] * cw_ref[j:j + 1, cols]
            return out

        a = conv(ea_ref, sa_ref, ta_ref, cwa_ref, cba_ref, 0)
        g = conv(eg_ref, sg_ref, tg_ref, cwg_ref, cbg_ref, D_FF)
        part = jnp.dot((_silu(g) * a).astype(BF16), wd_ref[cols, :], preferred_element_type=F32)
        if f == 0:
            acc_ref[...] = part
        else:
            acc_ref[...] += part
    o_ref[0] = x_ref[0] + g2_ref[0] * acc_ref[...]


def _ffn(x, sa, sg, sh2, sc2, g2, gn, wu, cwa, cwg, cba, cbg, wd, *, tb, t_valid, slot=None):
    b, t, d = x.shape
    n = sa.shape[0]
    hb = tb // CONV_HALO
    blk = pl.BlockSpec((1, tb, d), lambda bi, i: (bi, i, 0))
    halo = pl.BlockSpec((1, CONV_HALO, d), lambda bi, i: (bi, jnp.maximum(i * hb - 1, 0), 0))
    if slot is None:
        st = pl.BlockSpec((1, CONV_HALO, D_FF), lambda bi, i: (bi, 0, 0))
        seqv = pl.BlockSpec((1, 1, d), lambda bi, i: (bi, 0, 0))
    else:
        st = pl.BlockSpec((tb // slot, CONV_HALO, D_FF), lambda bi, i: (i, 0, 0))
        seqv = blk
    cst = lambda r, c: pl.BlockSpec((r, c), lambda bi, i: (0, 0))
    return pl.pallas_call(
        functools.partial(_ffn_kernel, tb=tb, t_valid=t_valid, tail_row=t_valid - (t // tb - 1) * tb, slot=slot),
        name="conv_ffn",
        out_shape=[jax.ShapeDtypeStruct((b, t, d), F32), jax.ShapeDtypeStruct((n, CONV_HALO, D_FF), F32),
                   jax.ShapeDtypeStruct((n, CONV_HALO, D_FF), F32)],
        grid=(b, t // tb),
        in_specs=[blk, halo, st, st, seqv, seqv, seqv, cst(1, d), cst(d, 2 * D_FF), cst(FFN_CONV, D_FF),
                  cst(FFN_CONV, D_FF), cst(1, D_FF), cst(1, D_FF), cst(D_FF, d)],
        out_specs=[blk, st, st],
        scratch_shapes=[pltpu.VMEM((CONV_HALO + tb, d), F32), pltpu.VMEM((CONV_HALO + tb, FFN_TILE), F32),
                        pltpu.VMEM((CONV_HALO + tb, FFN_TILE), F32), pltpu.VMEM((tb, d), F32)],
        compiler_params=_cparams("parallel", "arbitrary"),
    )(x, x, sa, sg, sh2, sc2, g2, gn, wu, cwa, cwg, cba, cbg, wd)


def _pad_rows(a, rows, front=False):
    extra = rows - a.shape[1]
    if extra == 0:
        return a
    return jnp.pad(a, ((0, 0), (extra, 0) if front else (0, extra), (0, 0)))


def _rope_tables(pos):
    half = HEAD_DIM // 2
    freqs = ROPE_THETA ** (-jnp.arange(half, dtype=jnp.float32) / half)
    ang = pos.astype(jnp.float32)[:, None] * freqs[None, :]
    cos, sin = jnp.cos(ang), jnp.sin(ang)
    return jnp.concatenate([cos, cos, cos, cos], axis=-1), jnp.concatenate([-sin, sin, -sin, sin], axis=-1)


def _layer_weights(l, w_ada, b_ada, norm1, norm2, w_in, q_norm, k_norm, pool_w, pool_scale, ssd_conv_w, ssd_conv_b,
                   ssd_dt_bias, ssd_a_log, ssd_d, ssd_norm, w_branch_attn, w_branch_pool, w_branch_ssd, w_out,
                   ffn_up, ffn_conv_w, ffn_conv_b, ffn_down):
    o = _OFFS
    win = w_in[l]
    col = lambda a, b: win[:, a:b]
    zpad = lambda n: jnp.zeros((D_MODEL, n), win.dtype)
    row = lambda v: v.reshape(1, -1).astype(F32)
    lanepad = lambda v: jnp.pad(v.astype(F32), (0, LANES - v.shape[0])).reshape(1, LANES)
    return dict(
        w_ada=w_ada[l].astype(BF16), b_ada=row(b_ada[l]), norm1=row(norm1[l]), norm2=row(norm2[l]),
        w_attn=jnp.concatenate([col(o[0], o[6]), zpad(LANES - IDX_DIM - IDX_HEADS)], axis=1).astype(BF16),
        w_pool=col(o[6], o[7]).astype(BF16),
        w_zdt=jnp.concatenate([col(o[7], o[8]), col(o[9], o[10]), zpad(LANES - SSD_HEADS)], axis=1).astype(BF16),
        w_xbc=col(o[8], o[9]).astype(BF16),
        w_gate=col(o[10], o[11]).astype(BF16),
        qn=row(jnp.tile(q_norm[l], LANES // HEAD_DIM)), kn=row(jnp.tile(k_norm[l], LANES // HEAD_DIM)),
        pool_w=pool_w[l].astype(BF16), pool_scale=row(pool_scale[l]),
        cw=ssd_conv_w[l].astype(F32), cb=row(ssd_conv_b[l]), dtb=lanepad(ssd_dt_bias[l]), alog=lanepad(ssd_a_log[l]),
        dsk=row(jnp.repeat(ssd_d[l], SSD_HEAD_DIM)), nw=row(ssd_norm[l]),
        wa=w_branch_attn[l].astype(BF16), wp=w_branch_pool[l].astype(BF16), ws=w_branch_ssd[l].astype(BF16),
        wo=w_out[l].astype(BF16),
        w_up=ffn_up[l].astype(BF16),
        cwa=ffn_conv_w[l][:, :D_FF].astype(F32), cwg=ffn_conv_w[l][:, D_FF:].astype(F32),
        cba=row(ffn_conv_b[l][:D_FF]), cbg=row(ffn_conv_b[l][D_FF:]), wd=ffn_down[l].astype(BF16))


def _group_layer(x, mod, lw, consts, *, pos0, past, states):
    b, t, d = x.shape
    n = b * t
    tp = -(-t // SEQ_BLOCK) * SEQ_BLOCK
    per_seq = t % SEQ_BLOCK == 0
    tm = min(ROW_TILE, t) if per_seq else n
    sh1, sc1, g1, sh2, sc2, g2 = [mod[:, i * d:(i + 1) * d] for i in range(6)]
    pos = pos0 + jnp.arange(t)
    cos_t, sin_t = _rope_tables(pos)
    if per_seq:
        view = lambda m: m.reshape(b, 1, d)
        mv_idx = lambda i: i // (t // tm)
        tab = lambda a: a.reshape(t // tm, tm, LANES)
        tab_idx = lambda i: i % (t // tm)
    else:
        view = lambda m: jnp.repeat(m, t, axis=0).reshape(1, n, d)
        mv_idx = lambda i: 0
        tab = lambda a: jnp.tile(a, (b, 1)).reshape(1, n, LANES)
        tab_idx = lambda i: 0
    xf = x.reshape(n, d)
    nm = lambda w: _norm_matmul(xf, view(sh1), view(sc1), mv_idx, lw['norm1'], w, tm)
    pa, pool_u, zdt, xbc, gates = nm(lw['w_attn']), nm(lw['w_pool']), nm(lw['w_zdt']), nm(lw['w_xbc']), nm(lw['w_gate'])

    q_r, k_r, qi_r, ki2, wi_s = _attn_prep(pa, tab(cos_t), tab(sin_t), tab_idx, lw['qn'], lw['kn'], consts['seg'], tm)
    v_new = pa[:, D_ATTN + D_KV:D_ATTN + 2 * D_KV]
    seq3 = lambda a: a.reshape(b, t, a.shape[-1])
    if past is None:
        tq, tk, kw = ATTN_Q_ROWS, min(ATTN_KEYS, t), LANES
        k_all, v_all, ki_all = seq3(k_r).astype(BF16), seq3(v_new).astype(BF16), seq3(ki2).astype(BF16)
        s_valid = t
        qpad = lambda a: seq3(a)
    else:
        tq, kw = 2 * SUBLANES, LANES
        page_table, pool_off, ck, cv, ci = past
        k_all, v_all, ki_all = _page_gather(page_table, pool_off, ck, cv, ci, seq3(k_r), seq3(v_new), seq3(ki2))
        tk = min(ATTN_KEYS_DECODE, k_all.shape[1] // 2)
        s_valid = pos0 + t
        qpad = lambda a: _pad_rows(seq3(a), tq)
    assert k_all.shape[1] % tk == 0
    kk = min(INDEX_TOPK, s_valid // 4)
    o_attn = _attention(qpad(q_r), qpad(qi_r), qpad(wi_s), k_all, v_all, ki_all,
                        tq=tq, tk=tk, kw=kw, pos0=pos0, s_valid=s_valid, kk=kk)[:, :t].reshape(n, D_ATTN)

    st_pool, st_sconv, (st_ssd, ssd_off), st_fconv = states
    seqp = lambda a: _pad_rows(seq3(a), tp)
    o_pool = _pool(seqp(pool_u), _pad_rows(st_pool, POOL_HALO, front=True), lw['pool_w'], lw['pool_scale'], pos0=pos0)
    o_ssd, ssd_state = _ssd(seqp(zdt), seqp(xbc), _pad_rows(st_sconv, CONV_HALO, front=True),
                            st_ssd, ssd_off, lw['cw'], lw['cb'], lw['dtb'], lw['alog'], lw['dsk'],
                            lw['nw'], consts['ex'], consts['ext_t'], t_valid=t)
    unpad = lambda a: a[:, :t].reshape(n, a.shape[-1])
    x1 = _merge(xf, o_attn, unpad(o_pool), unpad(o_ssd), gates, view(g1), mv_idx,
                lw['wa'], lw['wp'], lw['ws'], lw['wo'], tm)

    st_f = _pad_rows(st_fconv, CONV_HALO, front=True)
    ffn_w = (lw['norm2'], lw['w_up'], lw['cwa'], lw['cwg'], lw['cba'], lw['cbg'], lw['wd'])
    if per_seq or t > FFN_SLOT - CONV_HALO:
        per_b = lambda m: m.reshape(b, 1, d)
        x2, tail_a, tail_g = _ffn(seqp(x1), st_f[:, :, :D_FF], st_f[:, :, D_FF:], per_b(sh2), per_b(sc2), per_b(g2),
                                  *ffn_w, tb=min(FFN_ROWS, tp), t_valid=t)
        x2 = x2[:, :t]
    else:
        in_slot = lambda a: jnp.pad(a, ((0, 0), (CONV_HALO, FFN_SLOT - CONV_HALO - t), (0, 0))).reshape(1, b * FFN_SLOT, d)
        per_row = lambda m: jnp.repeat(m, FFN_SLOT, axis=0).reshape(1, b * FFN_SLOT, d)
        slots_per_step = max(k for k in range(1, FFN_ROWS // FFN_SLOT + 1) if b % k == 0)
        x2, tail_a, tail_g = _ffn(in_slot(seq3(x1)), st_f[:, :, :D_FF], st_f[:, :, D_FF:], per_row(sh2), per_row(sc2),
                                  per_row(g2), *ffn_w, tb=slots_per_step * FFN_SLOT, t_valid=t, slot=FFN_SLOT)
        x2 = x2.reshape(b, FFN_SLOT, d)[:, CONV_HALO:CONV_HALO + t]
    fconv_state = jnp.concatenate([tail_a, tail_g], axis=-1)[:, -(FFN_CONV - 1):]

    def tail(prev, news, keep):
        last = jnp.concatenate([seq3(a)[:, -min(keep, t):] for a in news], axis=-1)
        return jnp.concatenate([prev, last], axis=1)[:, -keep:]

    new_states = (seq3(k_r).reshape(b, t, N_KV_HEADS, HEAD_DIM), seq3(v_new).reshape(b, t, N_KV_HEADS, HEAD_DIM),
                  seq3(ki2)[:, :, :IDX_DIM], tail(st_pool, [pool_u], POOL_CTX), tail(st_sconv, [xbc], SSD_CONV - 1),
                  ssd_state.reshape(b, SSD_HEADS, SSD_HEAD_DIM, SSD_STATE), fconv_state)
    return x2, new_states


def kernel(x_prompt, x_sample, c_prompt, c_sample, cache_k, cache_v, cache_kidx, page_table, state_pool, state_ssd_conv, state_ssd, state_ffn_conv, w_ada, b_ada, norm1, norm2, w_in, q_norm, k_norm, pool_w, pool_scale, ssd_conv_w, ssd_conv_b, ssd_dt_bias, ssd_a_log, ssd_d, ssd_norm, w_branch_attn, w_branch_pool, w_branch_ssd, w_out, ffn_up, ffn_conv_w, ffn_conv_b, ffn_down):
    bp, tp, d = x_prompt.shape
    bs, ts, _ = x_sample.shape
    depth = w_ada.shape[0]
    n_pool, page = cache_k.shape[1], cache_k.shape[2]
    past_len = page_table.shape[1] * page
    dt = x_prompt.dtype

    r = lax.broadcasted_iota(jnp.int32, (LANES, LANES), 0)
    c = lax.broadcasted_iota(jnp.int32, (LANES, LANES), 1)
    hr = lax.broadcasted_iota(jnp.int32, (LANES, D_SSD), 0)
    hc = lax.broadcasted_iota(jnp.int32, (LANES, D_SSD), 1)
    ex = (hr == hc // SSD_HEAD_DIM).astype(BF16)
    consts = dict(seg=(r // HEAD_DIM == c // HEAD_DIM).astype(BF16), ex=ex, ext_t=ex.T)

    c_all = jnp.concatenate([c_prompt, c_sample], axis=0)
    rows = -(-c_all.shape[0] // SUBLANES) * SUBLANES
    c_pad = jnp.pad(c_all, ((0, rows - c_all.shape[0]), (0, 0)))

    ck_t = jnp.transpose(cache_k, (0, 1, 3, 4, 2)).reshape(depth * n_pool, D_KV, page)
    cv_t = jnp.transpose(cache_v, (0, 1, 3, 4, 2)).reshape(depth * n_pool, D_KV, page)
    ci_t = jnp.transpose(cache_kidx, (0, 1, 3, 2)).reshape(depth * n_pool, IDX_DIM, page)

    ssd_all = state_ssd.reshape(depth * bs, D_SSD, SSD_STATE)
    zeros = lambda *s: jnp.zeros(s, dt)
    xp, xs = x_prompt, x_sample
    new_p, new_s = [[] for _ in range(7)], [[] for _ in range(7)]
    for l in range(depth):
        lw = _layer_weights(l, w_ada, b_ada, norm1, norm2, w_in, q_norm, k_norm, pool_w, pool_scale, ssd_conv_w,
                            ssd_conv_b, ssd_dt_bias, ssd_a_log, ssd_d, ssd_norm, w_branch_attn, w_branch_pool,
                            w_branch_ssd, w_out, ffn_up, ffn_conv_w, ffn_conv_b, ffn_down)
        mod = _mod(c_pad, lw['w_ada'], lw['b_ada'])
        xp, st_p = _group_layer(
            xp, mod[:bp], lw, consts, pos0=0, past=None,
            states=(zeros(bp, POOL_CTX, D_POOL), zeros(bp, SSD_CONV - 1, D_SSD_CONV),
                    (zeros(bp, D_SSD, SSD_STATE), 0), zeros(bp, FFN_CONV - 1, 2 * D_FF)))
        past = (page_table, l * n_pool, ck_t, cv_t, ci_t)
        xs, st_s = _group_layer(xs, mod[bp:bp + bs], lw, consts, pos0=past_len, past=past,
                                states=(state_pool[l], state_ssd_conv[l], (ssd_all, l * bs), state_ffn_conv[l]))
        for j in range(7):
            new_p[j].append(st_p[j])
            new_s[j].append(st_s[j])
    outs_p = [jnp.stack(s) for s in new_p]
    outs_s = [jnp.stack(s) for s in new_s]
    return (xp, xs, *outs_p, *outs_s)
```

```python
import functools

import numpy as np
import jax
import jax.numpy as jnp
from jax import lax
from jax.experimental import pallas as pl
from jax.experimental.pallas import tpu as pltpu

D_MODEL = 1024
N_HEADS = 16
N_KV_HEADS = 4
HEAD_DIM = 64
D_ATTN = N_HEADS * HEAD_DIM
D_KV = N_KV_HEADS * HEAD_DIM
IDX_HEADS = 8
IDX_DIM = 64
INDEX_TOPK = 256
ROPE_THETA = 10000.0
POOL_WINDOWS = (2, 4, 8, 16)
D_POOL = D_MODEL
POOL_GROUP = D_POOL // len(POOL_WINDOWS)
POOL_CTX = max(POOL_WINDOWS) - 1
D_SSD = 2 * D_MODEL
SSD_HEAD_DIM = 64
SSD_HEADS = D_SSD // SSD_HEAD_DIM
SSD_GROUPS = 4
SSD_STATE = 128
SSD_CONV = 4
D_SSD_CONV = D_SSD + 2 * SSD_GROUPS * SSD_STATE
D_FF = 2816
FFN_CONV = 3
N_BRANCH = 3
EPS = 1e-6
_SPLITS = (D_ATTN, D_KV, D_KV, IDX_HEADS * IDX_DIM, IDX_DIM, IDX_HEADS, D_POOL, D_SSD, D_SSD_CONV,
           SSD_HEADS, N_BRANCH * D_MODEL)
_OFFS = tuple(int(v) for v in np.cumsum((0,) + _SPLITS))

F32 = jnp.float32
BF16 = jnp.bfloat16
LANES = 128
SUBLANES = 8
SEQ_BLOCK = 128
ROW_TILE = 512
ATTN_Q_ROWS = 256
ATTN_KEYS = 256
ATTN_KEYS_DECODE = 1024
VMEM_LIMIT = 56 * 1024 * 1024
NEG = -1e30
LOG2E = 1.4426950408889634
N_BISECT = 20
NT = (((1,), (1,)), ((), ()))
TN = (((0,), (0,)), ((), ()))


def _cparams(*sem):
    return pltpu.CompilerParams(dimension_semantics=sem, vmem_limit_bytes=VMEM_LIMIT)


def _silu(v):
    return (0.5 * v) * (1.0 + jnp.tanh(0.5 * v))


def _split3(v):
    h1 = v.astype(BF16)
    r1 = v - h1.astype(F32)
    h2 = r1.astype(BF16)
    h3 = (r1 - h2.astype(F32)).astype(BF16)
    return h1, h2, h3


def _dot3(v, w):
    h1, h2, h3 = _split3(v)
    return (jnp.dot(h1, w, preferred_element_type=F32) + jnp.dot(h2, w, preferred_element_type=F32)
            + jnp.dot(h3, w, preferred_element_type=F32))


def _mod_kernel(c_ref, w_ref, b_ref, o_ref):
    s = _silu(c_ref[...])
    o_ref[...] = jnp.dot(s.astype(BF16), w_ref[...], preferred_element_type=F32) + b_ref[...]


def _mod(c_pad, w, b):
    rows, d = c_pad.shape
    n = w.shape[1]
    tn = n // 4
    return pl.pallas_call(
        _mod_kernel,
        name="adaln_mod",
        out_shape=jax.ShapeDtypeStruct((rows, n), F32),
        grid=(n // tn,),
        in_specs=[pl.BlockSpec((rows, d), lambda j: (0, 0)),
                  pl.BlockSpec((d, tn), lambda j: (0, j)),
                  pl.BlockSpec((1, tn), lambda j: (0, j))],
        out_specs=pl.BlockSpec((rows, tn), lambda j: (0, j)),
        compiler_params=_cparams("parallel"),
    )(c_pad, w, b)


def _norm_matmul_kernel(x_ref, sh_ref, sc_ref, g_ref, w_ref, o_ref):
    x = x_ref[...]
    y = x * lax.rsqrt(jnp.mean(x * x, axis=-1, keepdims=True) + EPS) * g_ref[...]
    h = y * (1.0 + sc_ref[0]) + sh_ref[0]
    o_ref[...] = jnp.dot(h.astype(BF16), w_ref[...], preferred_element_type=F32).astype(o_ref.dtype)


def _norm_matmul(x, mv_sh, mv_sc, mv_idx, g, w, tm):
    n, d = x.shape
    nout = w.shape[1]
    rb = mv_sh.shape[1]
    return pl.pallas_call(
        _norm_matmul_kernel,
        name="norm_matmul",
        out_shape=jax.ShapeDtypeStruct((n, nout), F32),
        grid=(n // tm,),
        in_specs=[pl.BlockSpec((tm, d), lambda i: (i, 0)),
                  pl.BlockSpec((1, rb, d), lambda i: (mv_idx(i), 0, 0)),
                  pl.BlockSpec((1, rb, d), lambda i: (mv_idx(i), 0, 0)),
                  pl.BlockSpec((1, d), lambda i: (0, 0)),
                  pl.BlockSpec((d, nout), lambda i: (0, 0))],
        out_specs=pl.BlockSpec((tm, nout), lambda i: (i, 0)),
        compiler_params=_cparams("parallel"),
    )(x, mv_sh, mv_sc, g, w)


def _attn_prep_kernel(q_ref, k_ref, qi_ref, kw_ref, cos_ref, sin_ref, qn_ref, kn_ref, seg_ref,
                      qo_ref, ko_ref, qio_ref, kio_ref, wio_ref):
    tm = q_ref.shape[0]
    lane = lax.broadcasted_iota(jnp.int32, (tm, LANES), 1)
    first_half = (lane % HEAD_DIM) < (HEAD_DIM // 2)
    cos = cos_ref[0]
    sin = sin_ref[0]
    seg = seg_ref[...]

    def rope(v):
        up = pltpu.roll(v, LANES - HEAD_DIM // 2, 1)
        dn = pltpu.roll(v, HEAD_DIM // 2, 1)
        return v * cos + jnp.where(first_half, up, dn) * sin

    def headnorm(v, gain):
        sq = v * v
        hi = sq.astype(BF16)
        lo = (sq - hi.astype(F32)).astype(BF16)
        ss = jnp.dot(hi, seg, preferred_element_type=F32) + jnp.dot(lo, seg, preferred_element_type=F32)
        return v * lax.rsqrt(ss * (1.0 / HEAD_DIM) + EPS) * gain

    qn = qn_ref[...]
    kn = kn_ref[...]
    for j in range(D_ATTN // LANES):
        sl = slice(j * LANES, (j + 1) * LANES)
        qo_ref[:, sl] = rope(headnorm(q_ref[:, sl], qn)) * (HEAD_DIM ** -0.5 * LOG2E)
    for j in range(D_KV // LANES):
        sl = slice(j * LANES, (j + 1) * LANES)
        ko_ref[:, sl] = rope(headnorm(k_ref[:, sl], kn))
    for j in range(IDX_HEADS * IDX_DIM // LANES):
        sl = slice(j * LANES, (j + 1) * LANES)
        qio_ref[:, sl] = rope(qi_ref[:, sl])
    kw = kw_ref[...]
    kr = rope(kw)
    kio_ref[...] = jnp.where(lane < IDX_DIM, kr, pltpu.roll(kr, IDX_DIM, 1))
    wio_ref[...] = pltpu.roll(kw, LANES - IDX_DIM, 1) * (IDX_HEADS ** -0.5 * IDX_DIM ** -0.5)


def _attn_prep(pa, cos_t, sin_t, tab_idx, qn, kn, seg, tm):
    n = pa.shape[0]
    blk = lambda w, c: pl.BlockSpec((tm, w), lambda i: (i, c))
    tab = pl.BlockSpec((1, tm, LANES), lambda i: (tab_idx(i), 0, 0))
    cst = lambda r, c: pl.BlockSpec((r, c), lambda i: (0, 0))
    return pl.pallas_call(
        _attn_prep_kernel,
        name="attn_prep",
        out_shape=[jax.ShapeDtypeStruct((n, D_ATTN), F32), jax.ShapeDtypeStruct((n, D_KV), F32),
                   jax.ShapeDtypeStruct((n, IDX_HEADS * IDX_DIM), F32),
                   jax.ShapeDtypeStruct((n, LANES), F32), jax.ShapeDtypeStruct((n, LANES), F32)],
        grid=(n // tm,),
        in_specs=[blk(D_ATTN, 0), blk(D_KV, D_ATTN // D_KV), blk(IDX_HEADS * IDX_DIM, 3), blk(LANES, 16),
                  tab, tab, cst(1, LANES), cst(1, LANES), cst(LANES, LANES)],
        out_specs=[blk(D_ATTN, 0), blk(D_KV, 0), blk(IDX_HEADS * IDX_DIM, 0), blk(LANES, 0), blk(LANES, 0)],
        compiler_params=_cparams("parallel"),
    )(pa, pa, pa, pa, cos_t, sin_t, qn, kn, seg)


GATHER_PAGES = 16


def _page_gather_kernel(pt_ref, *refs, n_steps, pg, page):
    ins, outs = refs[:3 * pg + 3], refs[3 * pg + 3:]
    past = pl.program_id(1) < n_steps
    for t in range(3):
        new_ref, out_ref = ins[3 * pg + t], outs[t]
        for i in range(pg):
            rows = slice(i * page, (i + 1) * page)
            tile = ins[t * pg + i][0]
            if t == 2:
                tile = jnp.concatenate([tile, tile], axis=0)
            out_ref[0, rows, :] = jnp.where(past, tile.T, new_ref[0, rows, :]).astype(out_ref.dtype)


def _page_gather(page_table, pool_off, ck, cv, ci, nk, nv, ni):
    bd, n_pages = page_table.shape
    page = ck.shape[2]
    pg = max(g for g in range(1, GATHER_PAGES + 1) if n_pages % g == 0)
    n_steps = n_pages // pg
    nk, nv, ni = [_pad_rows(a, pg * page) for a in (nk, nv, ni)]

    def cache(w, i):
        return pl.BlockSpec((1, w, page),
                            lambda b, p, pt: (pool_off + pt[b, jnp.minimum(p, n_steps - 1) * pg + i], 0, 0))

    new = lambda w: pl.BlockSpec((1, pg * page, w), lambda b, p, pt: (b, 0, 0))
    out = lambda w: pl.BlockSpec((1, pg * page, w), lambda b, p, pt: (b, p, 0))
    s_pad = (n_steps + 1) * pg * page
    in_specs = ([cache(D_KV, i) for i in range(pg)] + [cache(D_KV, i) for i in range(pg)]
                + [cache(IDX_DIM, i) for i in range(pg)] + [new(D_KV), new(D_KV), new(2 * IDX_DIM)])
    return pl.pallas_call(
        functools.partial(_page_gather_kernel, n_steps=n_steps, pg=pg, page=page),
        out_shape=[jax.ShapeDtypeStruct((bd, s_pad, D_KV), BF16), jax.ShapeDtypeStruct((bd, s_pad, D_KV), BF16),
                   jax.ShapeDtypeStruct((bd, s_pad, 2 * IDX_DIM), BF16)],
        grid_spec=pltpu.PrefetchScalarGridSpec(
            num_scalar_prefetch=1, grid=(bd, n_steps + 1), in_specs=in_specs,
            out_specs=[out(D_KV), out(D_KV), out(2 * IDX_DIM)]),
        compiler_params=_cparams("parallel", "parallel"),
        name="page_gather",
    )(page_table, *([ck] * pg), *([cv] * pg), *([ci] * pg), nk, nv, ni)


GQA = N_HEADS // N_KV_HEADS


def _attn_kernel(q_ref, qi_ref, wi_ref, k_ref, v_ref, ki_ref, o_ref,
                 sc_ref, qh_ref, qim_ref, wb_ref, m_ref, l_ref, acc_ref, *, tq, tk, kw, nblk, n_static, pos0, s_valid, kk):
    q0 = pl.program_id(1) * tq
    nch = tk // LANES
    ngrp = D_KV // kw
    hpg = N_HEADS // ngrp
    lane = lax.broadcasted_iota(jnp.int32, (tq, LANES), 1)
    row = lax.broadcasted_iota(jnp.int32, (tq, LANES), 0)
    qpos = pos0 + q0 + row
    lo_half = lane < HEAD_DIM
    n_kv = jnp.minimum(nblk, (pos0 + q0 + tq - 1) // tk + 1)
    kkf = float(kk)
    chunk = lambda a, j: a[..., j * LANES:(j + 1) * LANES]
    rg = min(tq, LANES)

    def kv_loop(body, init):
        if n_static is None:
            return lax.fori_loop(0, n_kv, body, init)
        for c in range(n_static):
            init = body(c, init)
        return init

    def key_off(c):
        return c * tk if n_static is not None else pl.multiple_of(c * tk, tk)

    def bcast(col):
        return jnp.broadcast_to(col, col.shape[:-1] + (LANES,))

    wi = wi_ref[0]
    for j in range(IDX_HEADS // 2):
        t = qi_ref[0, :, j * LANES:(j + 1) * LANES]
        qim_ref[2 * j * tq:(2 * j + 1) * tq] = jnp.where(lo_half, t, 0.0).astype(BF16)
        qim_ref[(2 * j + 1) * tq:(2 * j + 2) * tq] = jnp.where(lo_half, 0.0, t).astype(BF16)
    for h in range(IDX_HEADS):
        wb_ref[h] = bcast(wi[:, h:h + 1])
    def kv_tile(h):
        kvh = h // GQA
        return (kvh * HEAD_DIM % kw) // LANES, kvh % 2

    if kw > LANES:
        qh_ref[...] = jnp.zeros(qh_ref.shape, BF16)
    for h in range(N_HEADS):
        t = q_ref[0, :, (h // 2) * LANES:(h // 2 + 1) * LANES]
        tile, kv_half = kv_tile(h)
        if h % 2 != kv_half:
            t = pltpu.roll(t, HEAD_DIM, 1)
        keep = lo_half if kv_half == 0 else jnp.logical_not(lo_half)
        hh = h % hpg
        qh_ref[h // hpg, hh * tq:(hh + 1) * tq, tile * LANES:(tile + 1) * LANES] = jnp.where(keep, t, 0.0).astype(BF16)
    m_ref[...] = jnp.full(m_ref.shape, NEG, F32)
    l_ref[...] = jnp.zeros(l_ref.shape, F32)
    acc_ref[...] = jnp.zeros(acc_ref.shape, F32)

    def p1(c, carry):
        mn, mx = carry
        kid = ki_ref[0, pl.ds(key_off(c), tk), :]
        y = lax.dot_general(qim_ref[...], kid, NT, preferred_element_type=F32)
        y = jnp.maximum(y, 0.0).reshape(IDX_HEADS, tq, tk)
        for j in range(nch):
            acc = wb_ref[0] * chunk(y[0], j)
            for h in range(1, IDX_HEADS):
                acc = acc + wb_ref[h] * chunk(y[h], j)
            kpos = c * tk + j * LANES + lane
            vis = jnp.logical_and(kpos <= qpos, kpos < s_valid)
            sc_ref[c, :, j * LANES:(j + 1) * LANES] = jnp.where(vis, acc, -jnp.inf)
            mn = jnp.minimum(mn, jnp.where(vis, acc, jnp.inf))
            mx = jnp.maximum(mx, jnp.where(vis, acc, -jnp.inf))
        return mn, mx

    mn, mx = kv_loop(p1, (jnp.full((tq, LANES), jnp.inf, F32), jnp.full((tq, LANES), -jnp.inf, F32)))
    rmin = bcast(jnp.min(mn, axis=1, keepdims=True))
    rmax = bcast(jnp.max(mx, axis=1, keepdims=True))

    def row_pass(step, init, reduce):
        accs = []
        for r in range(tq // rg):
            rows = slice(r * rg, (r + 1) * rg)

            def body(c, acc, rows=rows):
                x = sc_ref[c, rows, :]
                for j in range(nch):
                    acc = step(acc, chunk(x, j), rows)
                return acc
            accs.append(kv_loop(body, jnp.full((rg, LANES), init, F32)))
        outs = [bcast(reduce(a, axis=1, keepdims=True)) for a in accs]
        return outs[0] if len(outs) == 1 else jnp.concatenate(outs, axis=0)

    def count_gt(t_b):
        return row_pass(lambda acc, xj, rows: acc + jnp.where(xj > t_b[rows], 1.0, 0.0), 0.0, jnp.sum)

    def next_above(t_b):
        return row_pass(lambda acc, xj, rows: jnp.minimum(acc, jnp.where(xj > t_b[rows], xj, jnp.inf)), jnp.inf, jnp.min)

    at_min = jnp.minimum(qpos + 1, s_valid).astype(F32) < kkf
    below = rmin - jnp.maximum((rmax - rmin) * 2.0 ** -8, jnp.abs(rmin) * 2.0 ** -20 + 1e-30)

    def bisect(i, carry):
        lo, hi = carry
        mid = lo + 0.5 * (hi - lo)
        up = count_gt(mid) >= kkf
        return jnp.where(up, mid, lo), jnp.where(up, hi, mid)

    lo, hi = lax.fori_loop(0, N_BISECT, bisect, (below, rmax))
    tau = next_above(lo)
    ct = count_gt(tau)

    def pending(ct_):
        return jnp.logical_and(ct_ >= kkf, jnp.logical_not(at_min))

    def w_cond(st):
        return jnp.max(jnp.where(pending(st[2]), 1.0, 0.0)) > 0.5

    def w_body(st):
        lo_, tau_, ct_ = st
        need = pending(ct_)
        lo2 = jnp.where(need, tau_, lo_)
        tau2 = next_above(lo2)
        ct2 = count_gt(tau2)
        return lo2, jnp.where(need, tau2, tau_), jnp.where(need, ct2, ct_)

    lo, tau, ct = lax.while_loop(w_cond, w_body, (lo, tau, ct))
    tau = jnp.where(at_min, rmin, tau)
    m_need = jnp.where(at_min, kkf, kkf - ct)

    rk = lax.broadcasted_iota(jnp.int32, (tk, tk), 0)
    ck = lax.broadcasted_iota(jnp.int32, (tk, tk), 1)
    before = jnp.where(rk < ck, 1.0, 0.0).astype(BF16)

    def p2(c, cum):
        x = sc_ref[c]
        eqs = [chunk(x, j) == tau for j in range(nch)]
        eqf = [jnp.where(e, 1.0, 0.0) for e in eqs]
        pre = jnp.dot(jnp.concatenate(eqf, axis=-1).astype(BF16), before, preferred_element_type=F32)
        bias = [jnp.where(chunk(x, j) > tau, 0.0,
                          jnp.where(eqs[j], jnp.where(chunk(pre, j) + cum < m_need, 0.0, NEG), NEG))
                for j in range(nch)]
        tot = eqf[0]
        for j in range(1, nch):
            tot = tot + eqf[j]
        cum = cum + bcast(jnp.sum(tot, axis=1, keepdims=True))
        koff = key_off(c)
        for p in range(ngrp):
            kp = k_ref[0, pl.ds(koff, tk), p * kw:(p + 1) * kw]
            vp = v_ref[0, pl.ds(koff, tk), p * kw:(p + 1) * kw]
            s = lax.dot_general(qh_ref[p], kp, NT, preferred_element_type=F32).reshape(hpg, tq, tk)
            sm = [chunk(s, j) + bias[j][None] for j in range(nch)]
            mx_ = sm[0]
            for j in range(1, nch):
                mx_ = jnp.maximum(mx_, sm[j])
            m_old = m_ref[p]
            m_new = jnp.maximum(m_old, bcast(jnp.max(mx_, axis=-1, keepdims=True)))
            pe = [jnp.exp2(sm[j] - m_new) for j in range(nch)]
            ps = pe[0]
            for j in range(1, nch):
                ps = ps + pe[j]
            alpha = jnp.exp2(m_old - m_new)
            l_ref[p] = alpha * l_ref[p] + bcast(jnp.sum(ps, axis=-1, keepdims=True))
            pv = jnp.dot(jnp.concatenate(pe, axis=-1).reshape(hpg * tq, tk).astype(BF16), vp,
                         preferred_element_type=F32)
            alpha = alpha.reshape(hpg * tq, LANES)
            acc_ref[p] = jnp.concatenate([alpha] * (kw // LANES), axis=-1) * acc_ref[p] + pv
            m_ref[p] = m_new
        return cum

    kv_loop(p2, jnp.zeros((tq, LANES), F32))

    def head_out(h):
        p, hh = h // hpg, h % hpg
        tile, kv_half = kv_tile(h)
        r = acc_ref[p, hh * tq:(hh + 1) * tq, tile * LANES:(tile + 1) * LANES] / l_ref[p, hh]
        if kv_half != h % 2:
            r = pltpu.roll(r, HEAD_DIM, 1)
        return r

    for j in range(N_HEADS // 2):
        o_ref[0, :, j * LANES:(j + 1) * LANES] = jnp.where(lo_half, head_out(2 * j), head_out(2 * j + 1)).astype(o_ref.dtype)


def _attention(q, qi, wi, k_all, v_all, ki2, *, tq, tk, kw, pos0, s_valid, kk):
    b, tq_pad, _ = q.shape
    s_pad = k_all.shape[1]
    nblk = s_pad // tk
    ngrp = D_KV // kw
    hpg = N_HEADS // ngrp
    n_static = min(nblk, (pos0 + tq - 1) // tk + 1) if tq_pad == tq else None
    qblk = lambda w: pl.BlockSpec((1, tq, w), lambda bi, qi_: (bi, qi_, 0))
    seq = lambda w: pl.BlockSpec((1, s_pad, w), lambda bi, qi_: (bi, 0, 0))
    return pl.pallas_call(
        functools.partial(_attn_kernel, tq=tq, tk=tk, kw=kw, nblk=nblk, n_static=n_static, pos0=pos0, s_valid=s_valid,
                          kk=kk),
        out_shape=jax.ShapeDtypeStruct((b, tq_pad, D_ATTN), BF16),
        grid=(b, tq_pad // tq),
        in_specs=[qblk(D_ATTN), qblk(IDX_HEADS * IDX_DIM), qblk(LANES), seq(D_KV), seq(D_KV), seq(LANES)],
        out_specs=qblk(D_ATTN),
        scratch_shapes=[pltpu.VMEM((nblk, tq, tk), F32), pltpu.VMEM((ngrp, hpg * tq, kw), BF16),
                        pltpu.VMEM((IDX_HEADS * tq, LANES), BF16), pltpu.VMEM((IDX_HEADS, tq, LANES), F32),
                        pltpu.VMEM((ngrp, hpg, tq, LANES), F32), pltpu.VMEM((ngrp, hpg, tq, LANES), F32),
                        pltpu.VMEM((ngrp, hpg * tq, kw), F32)],
        compiler_params=_cparams("parallel", "parallel"),
        name="sparse_attention",
    )(q, qi, wi, k_all, v_all, ki2)


POOL_HALO = 16


def _pool_kernel(u_ref, halo_ref, st_ref, w_ref, scale_ref, o_ref, ext_ref, *, tb, pos0):
    i = pl.program_id(1)
    ext_ref[0:POOL_HALO] = jnp.where(i == 0, st_ref[0], halo_ref[0])
    ext_ref[POOL_HALO:POOL_HALO + tb] = u_ref[0]
    pos = pos0 + i * tb + lax.broadcasted_iota(jnp.int32, (tb, 1), 0)
    for gi, w in enumerate(POOL_WINDOWS):
        sl = slice(gi * POOL_GROUP, (gi + 1) * POOL_GROUP)
        tot = ext_ref[POOL_HALO:POOL_HALO + tb, sl]
        for k in range(1, w):
            tot = tot + ext_ref[POOL_HALO - k:POOL_HALO - k + tb, sl]
        cnt = jnp.minimum(w, pos + 1).astype(F32)
        pooled = tot / cnt - u_ref[0, :, sl]
        mixed = jnp.dot(pooled.astype(BF16), w_ref[gi], preferred_element_type=F32) * scale_ref[:, sl]
        o_ref[0, :, sl] = mixed.astype(o_ref.dtype)


def _pool(u, state16, w, scale, *, pos0):
    b, t, c = u.shape
    tb = SEQ_BLOCK
    hb = tb // POOL_HALO
    return pl.pallas_call(
        functools.partial(_pool_kernel, tb=tb, pos0=pos0),
        name="pool_mix",
        out_shape=jax.ShapeDtypeStruct((b, t, c), BF16),
        grid=(b, t // tb),
        in_specs=[pl.BlockSpec((1, tb, c), lambda bi, i: (bi, i, 0)),
                  pl.BlockSpec((1, POOL_HALO, c), lambda bi, i: (bi, jnp.maximum(i * hb - 1, 0), 0)),
                  pl.BlockSpec((1, POOL_HALO, c), lambda bi, i: (bi, 0, 0)),
                  pl.BlockSpec((len(POOL_WINDOWS), POOL_GROUP, POOL_GROUP), lambda bi, i: (0, 0, 0)),
                  pl.BlockSpec((1, c), lambda bi, i: (0, 0))],
        out_specs=pl.BlockSpec((1, tb, c), lambda bi, i: (bi, i, 0)),
        scratch_shapes=[pltpu.VMEM((POOL_HALO + tb, c), F32)],
        compiler_params=_cparams("parallel", "parallel"),
    )(u, u, state16, w, scale)


CONV_HALO = SUBLANES
SSD_GH = SSD_HEADS // SSD_GROUPS
SSD_GW = SSD_GH * SSD_HEAD_DIM
CONV_CHUNK = 512


def _ssd_kernel(z_ref, dt_ref, xbc_ref, halo_ref, st_ref, h0_ref, cw_ref, cb_ref, dtb_ref, alog_ref, dsk_ref,
                nw_ref, ex_ref, ext_t_ref, y_ref, hT_ref, ext_ref, xs_ref, h_ref, *, q, t_valid):
    c = pl.program_id(1)

    @pl.when(c == 0)
    def _():
        h_ref[...] = h0_ref[0]

    ext_ref[0:CONV_HALO] = jnp.where(c == 0, st_ref[0], halo_ref[0])
    ext_ref[CONV_HALO:CONV_HALO + q] = xbc_ref[0]
    for ch in range(D_SSD_CONV // CONV_CHUNK):
        cc = slice(ch * CONV_CHUNK, (ch + 1) * CONV_CHUNK)
        xc = cb_ref[:, cc]
        for j in range(SSD_CONV):
            off = CONV_HALO - (SSD_CONV - 1) + j
            xc = xc + ext_ref[off:off + q, cc] * cw_ref[j:j + 1, cc]
        xs_ref[:, cc] = _silu(xc)
    x = xs_ref[:, :D_SSD]

    lane = lax.broadcasted_iota(jnp.int32, (q, LANES), 1)
    row = lax.broadcasted_iota(jnp.int32, (q, LANES), 0)
    v = dt_ref[0] + dtb_ref[...]
    dt = jnp.maximum(v, 0.0) + jnp.log1p(jnp.exp(-jnp.abs(v)))
    dt = jnp.where(jnp.logical_and(lane < SSD_HEADS, c * q + row < t_valid), dt, 0.0)
    a = dt * (-jnp.exp(alog_ref[...]))
    tri = jnp.where(lane <= row, 1.0, 0.0).astype(BF16)
    causal = lane <= row
    cs = _dot3_left(tri, a)
    cs_t = cs.T
    ex = ex_ref[...]
    cs_last = cs[q - 1:q, :]
    dt_x = _dot3(dt, ex)
    ecs_x = jnp.exp(_dot3(cs, ex))
    dec_x = jnp.exp(_dot3(cs_last - cs, ex))
    xdt = x * dt_x
    xdec = (xdt * dec_x).astype(BF16)
    xdt16 = xdt.astype(BF16)
    rowdec = jnp.exp(_dot3_left(ext_t_ref[...], jnp.broadcast_to(cs_t[:, q - 1:q], (LANES, LANES))))
    lo_half = lane < SSD_HEAD_DIM

    for g in range(SSD_GROUPS):
        bg = xs_ref[:, D_SSD + g * SSD_STATE:D_SSD + (g + 1) * SSD_STATE].astype(BF16)
        cg = xs_ref[:, D_SSD + (SSD_GROUPS + g) * SSD_STATE:D_SSD + (SSD_GROUPS + g + 1) * SSD_STATE].astype(BF16)
        cb = lax.dot_general(cg, bg, NT, preferred_element_type=F32)
        hs = slice(g * SSD_GW, (g + 1) * SSD_GW)
        hprev = h_ref[hs, :]
        y_off = lax.dot_general(cg, hprev.astype(BF16), NT, preferred_element_type=F32) * ecs_x[:, hs]
        for jp in range(SSD_GH // 2):
            cols = slice(g * SSD_GW + jp * LANES, g * SSD_GW + (jp + 1) * LANES)
            xp = xdt16[:, cols]
            halves = []
            for e in range(2):
                hd = g * SSD_GH + 2 * jp + e
                seg = cs[:, hd:hd + 1] - cs_t[hd:hd + 1, :]
                lmat = jnp.where(causal, jnp.exp(jnp.where(causal, seg, 0.0)), 0.0)
                halves.append(jnp.dot((cb * lmat).astype(BF16), xp, preferred_element_type=F32))
            y_diag = jnp.where(lo_half, halves[0], halves[1])
            yv = y_diag + y_off[:, jp * LANES:(jp + 1) * LANES] + dsk_ref[:, cols] * xs_ref[:, cols]
            zv = z_ref[0, :, cols]
            ext_ref[0:q, cols] = yv * _silu(zv)
        st_new = lax.dot_general(xdec[:, hs], bg, TN, preferred_element_type=F32)
        h_ref[hs, :] = hprev * rowdec[hs, :] + st_new

    for g in range(SSD_GROUPS):
        hs = slice(g * SSD_GW, (g + 1) * SSD_GW)
        yg = ext_ref[0:q, hs]
        yn = yg * lax.rsqrt(jnp.mean(yg * yg, axis=-1, keepdims=True) + EPS) * nw_ref[:, hs]
        y_ref[0, :, hs] = yn.astype(y_ref.dtype)

    @pl.when(c == pl.num_programs(1) - 1)
    def _():
        hT_ref[0] = h_ref[...]


def _dot3_left(w, v):
    h1, h2, h3 = _split3(v)
    return (jnp.dot(w, h1, preferred_element_type=F32) + jnp.dot(w, h2, preferred_element_type=F32)
            + jnp.dot(w, h3, preferred_element_type=F32))


def _ssd(zdt, xbc, state8, h0, h_off, cw, cb, dtb, alog, dsk, nw, ex, ext_t, *, t_valid):
    b, t, _ = xbc.shape
    q = SEQ_BLOCK
    hb = q // CONV_HALO
    cst = lambda r, c_: pl.BlockSpec((r, c_), lambda bi, i: (0, 0))
    return pl.pallas_call(
        functools.partial(_ssd_kernel, q=q, t_valid=t_valid),
        name="ssd_scan",
        out_shape=[jax.ShapeDtypeStruct((b, t, D_SSD), BF16), jax.ShapeDtypeStruct((b, D_SSD, SSD_STATE), F32)],
        grid=(b, t // q),
        in_specs=[pl.BlockSpec((1, q, D_SSD), lambda bi, i: (bi, i, 0)),
                  pl.BlockSpec((1, q, LANES), lambda bi, i: (bi, i, D_SSD // LANES)),
                  pl.BlockSpec((1, q, D_SSD_CONV), lambda bi, i: (bi, i, 0)),
                  pl.BlockSpec((1, CONV_HALO, D_SSD_CONV), lambda bi, i: (bi, jnp.maximum(i * hb - 1, 0), 0)),
                  pl.BlockSpec((1, CONV_HALO, D_SSD_CONV), lambda bi, i: (bi, 0, 0)),
                  pl.BlockSpec((1, D_SSD, SSD_STATE), lambda bi, i: (h_off + bi, 0, 0)),
                  cst(SSD_CONV, D_SSD_CONV), cst(1, D_SSD_CONV), cst(1, LANES), cst(1, LANES), cst(1, D_SSD),
                  cst(1, D_SSD), cst(LANES, D_SSD), cst(D_SSD, LANES)],
        out_specs=[pl.BlockSpec((1, q, D_SSD), lambda bi, i: (bi, i, 0)),
                   pl.BlockSpec((1, D_SSD, SSD_STATE), lambda bi, i: (bi, 0, 0))],
        scratch_shapes=[pltpu.VMEM((CONV_HALO + q, D_SSD_CONV), F32), pltpu.VMEM((q, D_SSD_CONV), F32),
                        pltpu.VMEM((D_SSD, SSD_STATE), F32)],
        compiler_params=_cparams("parallel", "arbitrary"),
    )(zdt, zdt, xbc, xbc, state8, h0, cw, cb, dtb, alog, dsk, nw, ex, ext_t)


def _merge_kernel(x_ref, oa_ref, op_ref, os_ref, gt_ref, g1_ref, wa_ref, wp_ref, ws_ref, wo_ref, o_ref):
    def gate(bi):
        return jax.nn.sigmoid(gt_ref[:, bi * D_MODEL:(bi + 1) * D_MODEL])
    merged = (gate(0) * jnp.dot(oa_ref[...], wa_ref[...], preferred_element_type=F32)
              + gate(1) * jnp.dot(op_ref[...], wp_ref[...], preferred_element_type=F32)
              + gate(2) * jnp.dot(os_ref[...], ws_ref[...], preferred_element_type=F32))
    o_ref[...] = x_ref[...] + g1_ref[0] * jnp.dot(merged.astype(BF16), wo_ref[...], preferred_element_type=F32)


def _merge(x, oa, op, os_, gates, mv_g1, mv_idx, wa, wp, ws, wo, tm):
    n, d = x.shape
    rb = mv_g1.shape[1]
    rowb = lambda w: pl.BlockSpec((tm, w), lambda i: (i, 0))
    cst = lambda r, c: pl.BlockSpec((r, c), lambda i: (0, 0))
    return pl.pallas_call(
        _merge_kernel,
        name="merge_out",
        out_shape=jax.ShapeDtypeStruct((n, d), F32),
        grid=(n // tm,),
        in_specs=[rowb(d), rowb(D_ATTN), rowb(D_POOL), rowb(D_SSD), rowb(N_BRANCH * D_MODEL),
                  pl.BlockSpec((1, rb, d), lambda i: (mv_idx(i), 0, 0)),
                  cst(D_ATTN, d), cst(D_POOL, d), cst(D_SSD, d), cst(d, d)],
        out_specs=rowb(d),
        compiler_params=_cparams("parallel"),
    )(x, oa, op, os_, gates, mv_g1, wa, wp, ws, wo)


FFN_TILE = D_FF // 2
FFN_ROWS = 512
FFN_SLOT = 16


def _ffn_kernel(x_ref, hx_ref, sa_ref, sg_ref, sh_ref, sc_ref, g2_ref, gn_ref, wu_ref, cwa_ref, cwg_ref, cba_ref,
                cbg_ref, wd_ref, o_ref, ta_ref, tg_ref, xin_ref, ea_ref, eg_ref, acc_ref, *, tb, t_valid, tail_row, slot):
    i = pl.program_id(1)
    if slot is None:
        carried_in = [(0, 0)]
        tails = [(tail_row, 0)]
    else:
        carried_in = [(CONV_HALO + k * slot, k) for k in range(tb // slot)]
        tails = [(CONV_HALO + k * slot + t_valid, k) for k in range(tb // slot)]
    xin_ref[0:CONV_HALO] = hx_ref[0]
    xin_ref[CONV_HALO:CONV_HALO + tb] = x_ref[0]
    xin = xin_ref[...]
    y = xin * lax.rsqrt(jnp.mean(xin * xin, axis=-1, keepdims=True) + EPS) * gn_ref[...]

    def rows_of(m):
        return m if m.shape[0] == 1 else jnp.concatenate([m[:CONV_HALO], m], axis=0)

    h = (y * (1.0 + rows_of(sc_ref[0])) + rows_of(sh_ref[0])).astype(BF16)
    for f in range(D_FF // FFN_TILE):
        cols = slice(f * FFN_TILE, (f + 1) * FFN_TILE)

        def conv(e_ref, s_ref, t_ref, cw_ref, cb_ref, col0):
            u = jnp.dot(h, wu_ref[:, col0 + f * FFN_TILE:col0 + (f + 1) * FFN_TILE], preferred_element_type=F32)
            e_ref[...] = u

            def take_carried():
                for row, k in carried_in:
                    e_ref[row:row + CONV_HALO] = s_ref[k, :, cols]

            def emit_tails():
                for row, k in tails:
                    t_ref[k, :, cols] = e_ref[row:row + CONV_HALO]

            if slot is None:
                pl.when(i == 0)(take_carried)
                pl.when(i == pl.num_programs(1) - 1)(emit_tails)
            else:
                take_carried()
                emit_tails()
            out = cb_ref[:, cols]
            for j in range(FFN_CONV):
                off = CONV_HALO - (FFN_CONV - 1) + j
                out = out + e_ref[off:off + tb] * cw_ref[j:j + 1, cols]
            return out

        a = conv(ea_ref, sa_ref, ta_ref, cwa_ref, cba_ref, 0)
        g = conv(eg_ref, sg_ref, tg_ref, cwg_ref, cbg_ref, D_FF)
        part = jnp.dot((_silu(g) * a).astype(BF16), wd_ref[cols, :], preferred_element_type=F32)
        if f == 0:
            acc_ref[...] = part
        else:
            acc_ref[...] += part
    o_ref[0] = x_ref[0] + g2_ref[0] * acc_ref[...]


def _ffn(x, sa, sg, sh2, sc2, g2, gn, wu, cwa, cwg, cba, cbg, wd, *, tb, t_valid, slot=None):
    b, t, d = x.shape
    n = sa.shape[0]
    hb = tb // CONV_HALO
    blk = pl.BlockSpec((1, tb, d), lambda bi, i: (bi, i, 0))
    halo = pl.BlockSpec((1, CONV_HALO, d), lambda bi, i: (bi, jnp.maximum(i * hb - 1, 0), 0))
    if slot is None:
        st = pl.BlockSpec((1, CONV_HALO, D_FF), lambda bi, i: (bi, 0, 0))
        seqv = pl.BlockSpec((1, 1, d), lambda bi, i: (bi, 0, 0))
    else:
        st = pl.BlockSpec((tb // slot, CONV_HALO, D_FF), lambda bi, i: (i, 0, 0))
        seqv = blk
    cst = lambda r, c: pl.BlockSpec((r, c), lambda bi, i: (0, 0))
    return pl.pallas_call(
        functools.partial(_ffn_kernel, tb=tb, t_valid=t_valid, tail_row=t_valid - (t // tb - 1) * tb, slot=slot),
        name="conv_ffn",
        out_shape=[jax.ShapeDtypeStruct((b, t, d), F32), jax.ShapeDtypeStruct((n, CONV_HALO, D_FF), F32),
                   jax.ShapeDtypeStruct((n, CONV_HALO, D_FF), F32)],
        grid=(b, t // tb),
        in_specs=[blk, halo, st, st, seqv, seqv, seqv, cst(1, d), cst(d, 2 * D_FF), cst(FFN_CONV, D_FF),
                  cst(FFN_CONV, D_FF), cst(1, D_FF), cst(1, D_FF), cst(D_FF, d)],
        out_specs=[blk, st, st],
        scratch_shapes=[pltpu.VMEM((CONV_HALO + tb, d), F32), pltpu.VMEM((CONV_HALO + tb, FFN_TILE), F32),
                        pltpu.VMEM((CONV_HALO + tb, FFN_TILE), F32), pltpu.VMEM((tb, d), F32)],
        compiler_params=_cparams("parallel", "arbitrary"),
    )(x, x, sa, sg, sh2, sc2, g2, gn, wu, cwa, cwg, cba, cbg, wd)


def _pad_rows(a, rows, front=False):
    extra = rows - a.shape[1]
    if extra == 0:
        return a
    return jnp.pad(a, ((0, 0), (extra, 0) if front else (0, extra), (0, 0)))


def _rope_tables(pos):
    half = HEAD_DIM // 2
    freqs = ROPE_THETA ** (-jnp.arange(half, dtype=jnp.float32) / half)
    ang = pos.astype(jnp.float32)[:, None] * freqs[None, :]
    cos, sin = jnp.cos(ang), jnp.sin(ang)
    return jnp.concatenate([cos, cos, cos, cos], axis=-1), jnp.concatenate([-sin, sin, -sin, sin], axis=-1)


def _layer_weights(l, w_ada, b_ada, norm1, norm2, w_in, q_norm, k_norm, pool_w, pool_scale, ssd_conv_w, ssd_conv_b,
                   ssd_dt_bias, ssd_a_log, ssd_d, ssd_norm, w_branch_attn, w_branch_pool, w_branch_ssd, w_out,
                   ffn_up, ffn_conv_w, ffn_conv_b, ffn_down):
    o = _OFFS
    win = w_in[l]
    col = lambda a, b: win[:, a:b]
    zpad = lambda n: jnp.zeros((D_MODEL, n), win.dtype)
    row = lambda v: v.reshape(1, -1).astype(F32)
    lanepad = lambda v: jnp.pad(v.astype(F32), (0, LANES - v.shape[0])).reshape(1, LANES)
    return dict(
        w_ada=w_ada[l].astype(BF16), b_ada=row(b_ada[l]), norm1=row(norm1[l]), norm2=row(norm2[l]),
        w_attn=jnp.concatenate([col(o[0], o[6]), zpad(LANES - IDX_DIM - IDX_HEADS)], axis=1).astype(BF16),
        w_pool=col(o[6], o[7]).astype(BF16),
        w_zdt=jnp.concatenate([col(o[7], o[8]), col(o[9], o[10]), zpad(LANES - SSD_HEADS)], axis=1).astype(BF16),
        w_xbc=col(o[8], o[9]).astype(BF16),
        w_gate=col(o[10], o[11]).astype(BF16),
        qn=row(jnp.tile(q_norm[l], LANES // HEAD_DIM)), kn=row(jnp.tile(k_norm[l], LANES // HEAD_DIM)),
        pool_w=pool_w[l].astype(BF16), pool_scale=row(pool_scale[l]),
        cw=ssd_conv_w[l].astype(F32), cb=row(ssd_conv_b[l]), dtb=lanepad(ssd_dt_bias[l]), alog=lanepad(ssd_a_log[l]),
        dsk=row(jnp.repeat(ssd_d[l], SSD_HEAD_DIM)), nw=row(ssd_norm[l]),
        wa=w_branch_attn[l].astype(BF16), wp=w_branch_pool[l].astype(BF16), ws=w_branch_ssd[l].astype(BF16),
        wo=w_out[l].astype(BF16),
        w_up=ffn_up[l].astype(BF16),
        cwa=ffn_conv_w[l][:, :D_FF].astype(F32), cwg=ffn_conv_w[l][:, D_FF:].astype(F32),
        cba=row(ffn_conv_b[l][:D_FF]), cbg=row(ffn_conv_b[l][D_FF:]), wd=ffn_down[l].astype(BF16))


def _group_layer(x, mod, lw, consts, *, pos0, past, states):
    b, t, d = x.shape
    n = b * t
    tp = -(-t // SEQ_BLOCK) * SEQ_BLOCK
    per_seq = t % SEQ_BLOCK == 0
    tm = min(ROW_TILE, t) if per_seq else n
    sh1, sc1, g1, sh2, sc2, g2 = [mod[:, i * d:(i + 1) * d] for i in range(6)]
    pos = pos0 + jnp.arange(t)
    cos_t, sin_t = _rope_tables(pos)
    if per_seq:
        view = lambda m: m.reshape(b, 1, d)
        mv_idx = lambda i: i // (t // tm)
        tab = lambda a: a.reshape(t // tm, tm, LANES)
        tab_idx = lambda i: i % (t // tm)
    else:
        view = lambda m: jnp.repeat(m, t, axis=0).reshape(1, n, d)
        mv_idx = lambda i: 0
        tab = lambda a: jnp.tile(a, (b, 1)).reshape(1, n, LANES)
        tab_idx = lambda i: 0
    xf = x.reshape(n, d)
    nm = lambda w: _norm_matmul(xf, view(sh1), view(sc1), mv_idx, lw['norm1'], w, tm)
    pa, pool_u, zdt, xbc, gates = nm(lw['w_attn']), nm(lw['w_pool']), nm(lw['w_zdt']), nm(lw['w_xbc']), nm(lw['w_gate'])

    q_r, k_r, qi_r, ki2, wi_s = _attn_prep(pa, tab(cos_t), tab(sin_t), tab_idx, lw['qn'], lw['kn'], consts['seg'], tm)
    v_new = pa[:, D_ATTN + D_KV:D_ATTN + 2 * D_KV]
    seq3 = lambda a: a.reshape(b, t, a.shape[-1])
    if past is None:
        tq, tk, kw = ATTN_Q_ROWS, min(ATTN_KEYS, t), LANES
        k_all, v_all, ki_all = seq3(k_r).astype(BF16), seq3(v_new).astype(BF16), seq3(ki2).astype(BF16)
        s_valid = t
        qpad = lambda a: seq3(a)
    else:
        tq, kw = 2 * SUBLANES, LANES
        page_table, pool_off, ck, cv, ci = past
        k_all, v_all, ki_all = _page_gather(page_table, pool_off, ck, cv, ci, seq3(k_r), seq3(v_new), seq3(ki2))
        tk = min(ATTN_KEYS_DECODE, k_all.shape[1] // 2)
        s_valid = pos0 + t
        qpad = lambda a: _pad_rows(seq3(a), tq)
    assert k_all.shape[1] % tk == 0
    kk = min(INDEX_TOPK, s_valid // 4)
    o_attn = _attention(qpad(q_r), qpad(qi_r), qpad(wi_s), k_all, v_all, ki_all,
                        tq=tq, tk=tk, kw=kw, pos0=pos0, s_valid=s_valid, kk=kk)[:, :t].reshape(n, D_ATTN)

    st_pool, st_sconv, (st_ssd, ssd_off), st_fconv = states
    seqp = lambda a: _pad_rows(seq3(a), tp)
    o_pool = _pool(seqp(pool_u), _pad_rows(st_pool, POOL_HALO, front=True), lw['pool_w'], lw['pool_scale'], pos0=pos0)
    o_ssd, ssd_state = _ssd(seqp(zdt), seqp(xbc), _pad_rows(st_sconv, CONV_HALO, front=True),
                            st_ssd, ssd_off, lw['cw'], lw['cb'], lw['dtb'], lw['alog'], lw['dsk'],
                            lw['nw'], consts['ex'], consts['ext_t'], t_valid=t)
    unpad = lambda a: a[:, :t].reshape(n, a.shape[-1])
    x1 = _merge(xf, o_attn, unpad(o_pool), unpad(o_ssd), gates, view(g1), mv_idx,
                lw['wa'], lw['wp'], lw['ws'], lw['wo'], tm)

    st_f = _pad_rows(st_fconv, CONV_HALO, front=True)
    ffn_w = (lw['norm2'], lw['w_up'], lw['cwa'], lw['cwg'], lw['cba'], lw['cbg'], lw['wd'])
    if per_seq or t > FFN_SLOT - CONV_HALO:
        per_b = lambda m: m.reshape(b, 1, d)
        x2, tail_a, tail_g = _ffn(seqp(x1), st_f[:, :, :D_FF], st_f[:, :, D_FF:], per_b(sh2), per_b(sc2), per_b(g2),
                                  *ffn_w, tb=min(FFN_ROWS, tp), t_valid=t)
        x2 = x2[:, :t]
    else:
        in_slot = lambda a: jnp.pad(a, ((0, 0), (CONV_HALO, FFN_SLOT - CONV_HALO - t), (0, 0))).reshape(1, b * FFN_SLOT, d)
        per_row = lambda m: jnp.repeat(m, FFN_SLOT, axis=0).reshape(1, b * FFN_SLOT, d)
        slots_per_step = max(k for k in range(1, FFN_ROWS // FFN_SLOT + 1) if b % k == 0)
        x2, tail_a, tail_g = _ffn(in_slot(seq3(x1)), st_f[:, :, :D_FF], st_f[:, :, D_FF:], per_row(sh2), per_row(sc2),
                                  per_row(g2), *ffn_w, tb=slots_per_step * FFN_SLOT, t_valid=t, slot=FFN_SLOT)
        x2 = x2.reshape(b, FFN_SLOT, d)[:, CONV_HALO:CONV_HALO + t]
    fconv_state = jnp.concatenate([tail_a, tail_g], axis=-1)[:, -(FFN_CONV - 1):]

    def tail(prev, news, keep):
        last = jnp.concatenate([seq3(a)[:, -min(keep, t):] for a in news], axis=-1)
        return jnp.concatenate([prev, last], axis=1)[:, -keep:]

    new_states = (seq3(k_r).reshape(b, t, N_KV_HEADS, HEAD_DIM), seq3(v_new).reshape(b, t, N_KV_HEADS, HEAD_DIM),
                  seq3(ki2)[:, :, :IDX_DIM], tail(st_pool, [pool_u], POOL_CTX), tail(st_sconv, [xbc], SSD_CONV - 1),
                  ssd_state.reshape(b, SSD_HEADS, SSD_HEAD_DIM, SSD_STATE), fconv_state)
    return x2, new_states


def kernel(x_prompt, x_sample, c_prompt, c_sample, cache_k, cache_v, cache_kidx, page_table, state_pool, state_ssd_conv, state_ssd, state_ffn_conv, w_ada, b_ada, norm1, norm2, w_in, q_norm, k_norm, pool_w, pool_scale, ssd_conv_w, ssd_conv_b, ssd_dt_bias, ssd_a_log, ssd_d, ssd_norm, w_branch_attn, w_branch_pool, w_branch_ssd, w_out, ffn_up, ffn_conv_w, ffn_conv_b, ffn_down):
    bp, tp, d = x_prompt.shape
    bs, ts, _ = x_sample.shape
    depth = w_ada.shape[0]
    n_pool, page = cache_k.shape[1], cache_k.shape[2]
    past_len = page_table.shape[1] * page
    dt = x_prompt.dtype

    r = lax.broadcasted_iota(jnp.int32, (LANES, LANES), 0)
    c = lax.broadcasted_iota(jnp.int32, (LANES, LANES), 1)
    hr = lax.broadcasted_iota(jnp.int32, (LANES, D_SSD), 0)
    hc = lax.broadcasted_iota(jnp.int32, (LANES, D_SSD), 1)
    ex = (hr == hc // SSD_HEAD_DIM).astype(BF16)
    consts = dict(seg=(r // HEAD_DIM == c // HEAD_DIM).astype(BF16), ex=ex, ext_t=ex.T)

    c_all = jnp.concatenate([c_prompt, c_sample], axis=0)
    rows = -(-c_all.shape[0] // SUBLANES) * SUBLANES
    c_pad = jnp.pad(c_all, ((0, rows - c_all.shape[0]), (0, 0)))

    ck_t = jnp.transpose(cache_k, (0, 1, 3, 4, 2)).reshape(depth * n_pool, D_KV, page)
    cv_t = jnp.transpose(cache_v, (0, 1, 3, 4, 2)).reshape(depth * n_pool, D_KV, page)
    ci_t = jnp.transpose(cache_kidx, (0, 1, 3, 2)).reshape(depth * n_pool, IDX_DIM, page)

    ssd_all = state_ssd.reshape(depth * bs, D_SSD, SSD_STATE)
    zeros = lambda *s: jnp.zeros(s, dt)
    xp, xs = x_prompt, x_sample
    new_p, new_s = [[] for _ in range(7)], [[] for _ in range(7)]
    for l in range(depth):
        lw = _layer_weights(l, w_ada, b_ada, norm1, norm2, w_in, q_norm, k_norm, pool_w, pool_scale, ssd_conv_w,
                            ssd_conv_b, ssd_dt_bias, ssd_a_log, ssd_d, ssd_norm, w_branch_attn, w_branch_pool,
                            w_branch_ssd, w_out, ffn_up, ffn_conv_w, ffn_conv_b, ffn_down)
        mod = _mod(c_pad, lw['w_ada'], lw['b_ada'])
        xp, st_p = _group_layer(
            xp, mod[:bp], lw, consts, pos0=0, past=None,
            states=(zeros(bp, POOL_CTX, D_POOL), zeros(bp, SSD_CONV - 1, D_SSD_CONV),
                    (zeros(bp, D_SSD, SSD_STATE), 0), zeros(bp, FFN_CONV - 1, 2 * D_FF)))
        past = (page_table, l * n_pool, ck_t, cv_t, ci_t)
        xs, st_s = _group_layer(xs, mod[bp:bp + bs], lw, consts, pos0=past_len, past=past,
                                states=(state_pool[l], state_ssd_conv[l], (ssd_all, l * bs), state_ffn_conv[l]))
        for j in range(7):
            new_p[j].append(st_p[j])
            new_s[j].append(st_s[j])
    outs_p = [jnp.stack(s) for s in new_p]
    outs_s = [jnp.stack(s) for s in new_s]
    return (xp, xs, *outs_p, *outs_s)
```

```python
import functools

import numpy as np
import jax
import jax.numpy as jnp
from jax import lax
from jax.experimental import pallas as pl
from jax.experimental.pallas import tpu as pltpu

D_MODEL = 1024
N_HEADS = 16
N_KV_HEADS = 4
HEAD_DIM = 64
D_ATTN = N_HEADS * HEAD_DIM
D_KV = N_KV_HEADS * HEAD_DIM
IDX_HEADS = 8
IDX_DIM = 64
INDEX_TOPK = 256
ROPE_THETA = 10000.0
POOL_WINDOWS = (2, 4, 8, 16)
D_POOL = D_MODEL
POOL_GROUP = D_POOL // len(POOL_WINDOWS)
POOL_CTX = max(POOL_WINDOWS) - 1
D_SSD = 2 * D_MODEL
SSD_HEAD_DIM = 64
SSD_HEADS = D_SSD // SSD_HEAD_DIM
SSD_GROUPS = 4
SSD_STATE = 128
SSD_CONV = 4
D_SSD_CONV = D_SSD + 2 * SSD_GROUPS * SSD_STATE
D_FF = 2816
FFN_CONV = 3
N_BRANCH = 3
EPS = 1e-6
_SPLITS = (D_ATTN, D_KV, D_KV, IDX_HEADS * IDX_DIM, IDX_DIM, IDX_HEADS, D_POOL, D_SSD, D_SSD_CONV,
           SSD_HEADS, N_BRANCH * D_MODEL)
_OFFS = tuple(int(v) for v in np.cumsum((0,) + _SPLITS))

F32 = jnp.float32
BF16 = jnp.bfloat16
LANES = 128
SUBLANES = 8
SEQ_BLOCK = 128
ROW_TILE = 512
ATTN_Q_ROWS = 256
ATTN_KEYS = 256
ATTN_KEYS_DECODE = 1024
VMEM_LIMIT = 56 * 1024 * 1024
NEG = -1e30
LOG2E = 1.4426950408889634
N_BISECT = 20
NT = (((1,), (1,)), ((), ()))
TN = (((0,), (0,)), ((), ()))


def _cparams(*sem):
    return pltpu.CompilerParams(dimension_semantics=sem, vmem_limit_bytes=VMEM_LIMIT)


def _silu(v):
    return (0.5 * v) * (1.0 + jnp.tanh(0.5 * v))


def _split3(v):
    h1 = v.astype(BF16)
    r1 = v - h1.astype(F32)
    h2 = r1.astype(BF16)
    h3 = (r1 - h2.astype(F32)).astype(BF16)
    return h1, h2, h3


def _dot3(v, w):
    h1, h2, h3 = _split3(v)
    return (jnp.dot(h1, w, preferred_element_type=F32) + jnp.dot(h2, w, preferred_element_type=F32)
            + jnp.dot(h3, w, preferred_element_type=F32))


def _mod_kernel(c_ref, w_ref, b_ref, o_ref):
    s = _silu(c_ref[...])
    o_ref[...] = jnp.dot(s.astype(BF16), w_ref[...], preferred_element_type=F32) + b_ref[...]


def _mod(c_pad, w, b):
    rows, d = c_pad.shape
    n = w.shape[1]
    tn = n // 4
    return pl.pallas_call(
        _mod_kernel,
        name="adaln_mod",
        out_shape=jax.ShapeDtypeStruct((rows, n), F32),
        grid=(n // tn,),
        in_specs=[pl.BlockSpec((rows, d), lambda j: (0, 0)),
                  pl.BlockSpec((d, tn), lambda j: (0, j)),
                  pl.BlockSpec((1, tn), lambda j: (0, j))],
        out_specs=pl.BlockSpec((rows, tn), lambda j: (0, j)),
        compiler_params=_cparams("parallel"),
    )(c_pad, w, b)


def _norm_matmul_kernel(x_ref, sh_ref, sc_ref, g_ref, w_ref, o_ref):
    x = x_ref[...]
    y = x * lax.rsqrt(jnp.mean(x * x, axis=-1, keepdims=True) + EPS) * g_ref[...]
    h = y * (1.0 + sc_ref[0]) + sh_ref[0]
    o_ref[...] = jnp.dot(h.astype(BF16), w_ref[...], preferred_element_type=F32).astype(o_ref.dtype)


def _norm_matmul(x, mv_sh, mv_sc, mv_idx, g, w, tm):
    n, d = x.shape
    nout = w.shape[1]
    rb = mv_sh.shape[1]
    return pl.pallas_call(
        _norm_matmul_kernel,
        name="norm_matmul",
        out_shape=jax.ShapeDtypeStruct((n, nout), F32),
        grid=(n // tm,),
        in_specs=[pl.BlockSpec((tm, d), lambda i: (i, 0)),
                  pl.BlockSpec((1, rb, d), lambda i: (mv_idx(i), 0, 0)),
                  pl.BlockSpec((1, rb, d), lambda i: (mv_idx(i), 0, 0)),
                  pl.BlockSpec((1, d), lambda i: (0, 0)),
                  pl.BlockSpec((d, nout), lambda i: (0, 0))],
        out_specs=pl.BlockSpec((tm, nout), lambda i: (i, 0)),
        compiler_params=_cparams("parallel"),
    )(x, mv_sh, mv_sc, g, w)


def _attn_prep_kernel(q_ref, k_ref, qi_ref, kw_ref, cos_ref, sin_ref, qn_ref, kn_ref, seg_ref,
                      qo_ref, ko_ref, qio_ref, kio_ref, wio_ref):
    tm = q_ref.shape[0]
    lane = lax.broadcasted_iota(jnp.int32, (tm, LANES), 1)
    first_half = (lane % HEAD_DIM) < (HEAD_DIM // 2)
    cos = cos_ref[0]
    sin = sin_ref[0]
    seg = seg_ref[...]

    def rope(v):
        up = pltpu.roll(v, LANES - HEAD_DIM // 2, 1)
        dn = pltpu.roll(v, HEAD_DIM // 2, 1)
        return v * cos + jnp.where(first_half, up, dn) * sin

    def headnorm(v, gain):
        sq = v * v
        hi = sq.astype(BF16)
        lo = (sq - hi.astype(F32)).astype(BF16)
        ss = jnp.dot(hi, seg, preferred_element_type=F32) + jnp.dot(lo, seg, preferred_element_type=F32)
        return v * lax.rsqrt(ss * (1.0 / HEAD_DIM) + EPS) * gain

    qn = qn_ref[...]
    kn = kn_ref[...]
    for j in range(D_ATTN // LANES):
        sl = slice(j * LANES, (j + 1) * LANES)
        qo_ref[:, sl] = rope(headnorm(q_ref[:, sl], qn)) * (HEAD_DIM ** -0.5 * LOG2E)
    for j in range(D_KV // LANES):
        sl = slice(j * LANES, (j + 1) * LANES)
        ko_ref[:, sl] = rope(headnorm(k_ref[:, sl], kn))
    for j in range(IDX_HEADS * IDX_DIM // LANES):
        sl = slice(j * LANES, (j + 1) * LANES)
        qio_ref[:, sl] = rope(qi_ref[:, sl])
    kw = kw_ref[...]
    kr = rope(kw)
    kio_ref[...] = jnp.where(lane < IDX_DIM, kr, pltpu.roll(kr, IDX_DIM, 1))
    wio_ref[...] = pltpu.roll(kw, LANES - IDX_DIM, 1) * (IDX_HEADS ** -0.5 * IDX_DIM ** -0.5)


def _attn_prep(pa, cos_t, sin_t, tab_idx, qn, kn, seg, tm):
    n = pa.shape[0]
    blk = lambda w, c: pl.BlockSpec((tm, w), lambda i: (i, c))
    tab = pl.BlockSpec((1, tm, LANES), lambda i: (tab_idx(i), 0, 0))
    cst = lambda r, c: pl.BlockSpec((r, c), lambda i: (0, 0))
    return pl.pallas_call(
        _attn_prep_kernel,
        name="attn_prep",
        out_shape=[jax.ShapeDtypeStruct((n, D_ATTN), F32), jax.ShapeDtypeStruct((n, D_KV), F32),
                   jax.ShapeDtypeStruct((n, IDX_HEADS * IDX_DIM), F32),
                   jax.ShapeDtypeStruct((n, LANES), F32), jax.ShapeDtypeStruct((n, LANES), F32)],
        grid=(n // tm,),
        in_specs=[blk(D_ATTN, 0), blk(D_KV, D_ATTN // D_KV), blk(IDX_HEADS * IDX_DIM, 3), blk(LANES, 16),
                  tab, tab, cst(1, LANES), cst(1, LANES), cst(LANES, LANES)],
        out_specs=[blk(D_ATTN, 0), blk(D_KV, 0), blk(IDX_HEADS * IDX_DIM, 0), blk(LANES, 0), blk(LANES, 0)],
        compiler_params=_cparams("parallel"),
    )(pa, pa, pa, pa, cos_t, sin_t, qn, kn, seg)


GATHER_PAGES = 16


def _page_gather_kernel(pt_ref, *refs, n_steps, pg, page):
    ins, outs = refs[:3 * pg + 3], refs[3 * pg + 3:]
    past = pl.program_id(1) < n_steps
    for t in range(3):
        new_ref, out_ref = ins[3 * pg + t], outs[t]
        for i in range(pg):
            rows = slice(i * page, (i + 1) * page)
            tile = ins[t * pg + i][0]
            if t == 2:
                tile = jnp.concatenate([tile, tile], axis=0)
            out_ref[0, rows, :] = jnp.where(past, tile.T, new_ref[0, rows, :]).astype(out_ref.dtype)


def _page_gather(page_table, pool_off, ck, cv, ci, nk, nv, ni):
    bd, n_pages = page_table.shape
    page = ck.shape[2]
    pg = max(g for g in range(1, GATHER_PAGES + 1) if n_pages % g == 0)
    n_steps = n_pages // pg
    nk, nv, ni = [_pad_rows(a, pg * page) for a in (nk, nv, ni)]

    def cache(w, i):
        return pl.BlockSpec((1, w, page),
                            lambda b, p, pt: (pool_off + pt[b, jnp.minimum(p, n_steps - 1) * pg + i], 0, 0))

    new = lambda w: pl.BlockSpec((1, pg * page, w), lambda b, p, pt: (b, 0, 0))
    out = lambda w: pl.BlockSpec((1, pg * page, w), lambda b, p, pt: (b, p, 0))
    s_pad = (n_steps + 1) * pg * page
    in_specs = ([cache(D_KV, i) for i in range(pg)] + [cache(D_KV, i) for i in range(pg)]
                + [cache(IDX_DIM, i) for i in range(pg)] + [new(D_KV), new(D_KV), new(2 * IDX_DIM)])
    return pl.pallas_call(
        functools.partial(_page_gather_kernel, n_steps=n_steps, pg=pg, page=page),
        out_shape=[jax.ShapeDtypeStruct((bd, s_pad, D_KV), BF16), jax.ShapeDtypeStruct((bd, s_pad, D_KV), BF16),
                   jax.ShapeDtypeStruct((bd, s_pad, 2 * IDX_DIM), BF16)],
        grid_spec=pltpu.PrefetchScalarGridSpec(
            num_scalar_prefetch=1, grid=(bd, n_steps + 1), in_specs=in_specs,
            out_specs=[out(D_KV), out(D_KV), out(2 * IDX_DIM)]),
        compiler_params=_cparams("parallel", "parallel"),
        name="page_gather",
    )(page_table, *([ck] * pg), *([cv] * pg), *([ci] * pg), nk, nv, ni)


GQA = N_HEADS // N_KV_HEADS
N_PAIRS = D_KV // LANES
HEADS_PER_PAIR = N_HEADS // N_PAIRS


def _attn_kernel(q_ref, qi_ref, wi_ref, k_ref, v_ref, ki_ref, o_ref,
                 sc_ref, qh_ref, qim_ref, wb_ref, m_ref, l_ref, acc_ref, *, tq, tk, nblk, n_static, pos0, s_valid, kk):
    q0 = pl.program_id(1) * tq
    nch = tk // LANES
    hpg = HEADS_PER_PAIR
    lane = lax.broadcasted_iota(jnp.int32, (tq, LANES), 1)
    row = lax.broadcasted_iota(jnp.int32, (tq, LANES), 0)
    qpos = pos0 + q0 + row
    lo_half = lane < HEAD_DIM
    n_kv = jnp.minimum(nblk, (pos0 + q0 + tq - 1) // tk + 1)
    kkf = float(kk)
    chunk = lambda a, j: a[..., j * LANES:(j + 1) * LANES]
    rg = min(tq, LANES)

    def kv_loop(body, init):
        if n_static is None:
            return lax.fori_loop(0, n_kv, body, init)
        for c in range(n_static):
            init = body(c, init)
        return init

    def key_off(c):
        return c * tk if n_static is not None else pl.multiple_of(c * tk, tk)

    def bcast(col):
        return jnp.broadcast_to(col, col.shape[:-1] + (LANES,))

    wi = wi_ref[0]
    for j in range(IDX_HEADS // 2):
        t = qi_ref[0, :, j * LANES:(j + 1) * LANES]
        qim_ref[2 * j * tq:(2 * j + 1) * tq] = jnp.where(lo_half, t, 0.0).astype(BF16)
        qim_ref[(2 * j + 1) * tq:(2 * j + 2) * tq] = jnp.where(lo_half, 0.0, t).astype(BF16)
    for h in range(IDX_HEADS):
        wb_ref[h] = bcast(wi[:, h:h + 1])
    for h in range(N_HEADS):
        t = q_ref[0, :, (h // 2) * LANES:(h // 2 + 1) * LANES]
        kv_half = (h // GQA) % 2
        if h % 2 != kv_half:
            t = pltpu.roll(t, HEAD_DIM, 1)
        keep = lo_half if kv_half == 0 else jnp.logical_not(lo_half)
        hh = h % hpg
        qh_ref[h // hpg, hh * tq:(hh + 1) * tq] = jnp.where(keep, t, 0.0).astype(BF16)
    m_ref[...] = jnp.full(m_ref.shape, NEG, F32)
    l_ref[...] = jnp.zeros(l_ref.shape, F32)
    acc_ref[...] = jnp.zeros(acc_ref.shape, F32)

    def p1(c, carry):
        mn, mx = carry
        kid = ki_ref[0, pl.ds(key_off(c), tk), :]
        y = lax.dot_general(qim_ref[...], kid, NT, preferred_element_type=F32)
        y = jnp.maximum(y, 0.0).reshape(IDX_HEADS, tq, tk)
        for j in range(nch):
            acc = wb_ref[0] * chunk(y[0], j)
            for h in range(1, IDX_HEADS):
                acc = acc + wb_ref[h] * chunk(y[h], j)
            kpos = c * tk + j * LANES + lane
            vis = jnp.logical_and(kpos <= qpos, kpos < s_valid)
            sc_ref[c, :, j * LANES:(j + 1) * LANES] = jnp.where(vis, acc, -jnp.inf)
            mn = jnp.minimum(mn, jnp.where(vis, acc, jnp.inf))
            mx = jnp.maximum(mx, jnp.where(vis, acc, -jnp.inf))
        return mn, mx

    mn, mx = kv_loop(p1, (jnp.full((tq, LANES), jnp.inf, F32), jnp.full((tq, LANES), -jnp.inf, F32)))
    rmin = bcast(jnp.min(mn, axis=1, keepdims=True))
    rmax = bcast(jnp.max(mx, axis=1, keepdims=True))

    def row_pass(step, init, reduce):
        accs = []
        for r in range(tq // rg):
            rows = slice(r * rg, (r + 1) * rg)

            def body(c, acc, rows=rows):
                x = sc_ref[c, rows, :]
                for j in range(nch):
                    acc = step(acc, chunk(x, j), rows)
                return acc
            accs.append(kv_loop(body, jnp.full((rg, LANES), init, F32)))
        outs = [bcast(reduce(a, axis=1, keepdims=True)) for a in accs]
        return outs[0] if len(outs) == 1 else jnp.concatenate(outs, axis=0)

    def count_gt(t_b):
        return row_pass(lambda acc, xj, rows: acc + jnp.where(xj > t_b[rows], 1.0, 0.0), 0.0, jnp.sum)

    def next_above(t_b):
        return row_pass(lambda acc, xj, rows: jnp.minimum(acc, jnp.where(xj > t_b[rows], xj, jnp.inf)), jnp.inf, jnp.min)

    at_min = jnp.minimum(qpos + 1, s_valid).astype(F32) < kkf
    below = rmin - jnp.maximum((rmax - rmin) * 2.0 ** -8, jnp.abs(rmin) * 2.0 ** -20 + 1e-30)

    def bisect(i, carry):
        lo, hi = carry
        mid = lo + 0.5 * (hi - lo)
        up = count_gt(mid) >= kkf
        return jnp.where(up, mid, lo), jnp.where(up, hi, mid)

    lo, hi = lax.fori_loop(0, N_BISECT, bisect, (below, rmax))
    tau = next_above(lo)
    ct = count_gt(tau)

    def pending(ct_):
        return jnp.logical_and(ct_ >= kkf, jnp.logical_not(at_min))

    def w_cond(st):
        return jnp.max(jnp.where(pending(st[2]), 1.0, 0.0)) > 0.5

    def w_body(st):
        lo_, tau_, ct_ = st
        need = pending(ct_)
        lo2 = jnp.where(need, tau_, lo_)
        tau2 = next_above(lo2)
        ct2 = count_gt(tau2)
        return lo2, jnp.where(need, tau2, tau_), jnp.where(need, ct2, ct_)

    lo, tau, ct = lax.while_loop(w_cond, w_body, (lo, tau, ct))
    tau = jnp.where(at_min, rmin, tau)
    m_need = jnp.where(at_min, kkf, kkf - ct)

    rk = lax.broadcasted_iota(jnp.int32, (tk, tk), 0)
    ck = lax.broadcasted_iota(jnp.int32, (tk, tk), 1)
    before = jnp.where(rk < ck, 1.0, 0.0).astype(BF16)

    def p2(c, cum):
        x = sc_ref[c]
        eqs = [chunk(x, j) == tau for j in range(nch)]
        eqf = [jnp.where(e, 1.0, 0.0) for e in eqs]
        pre = jnp.dot(jnp.concatenate(eqf, axis=-1).astype(BF16), before, preferred_element_type=F32)
        bias = [jnp.where(chunk(x, j) > tau, 0.0,
                          jnp.where(eqs[j], jnp.where(chunk(pre, j) + cum < m_need, 0.0, NEG), NEG))
                for j in range(nch)]
        tot = eqf[0]
        for j in range(1, nch):
            tot = tot + eqf[j]
        cum = cum + bcast(jnp.sum(tot, axis=1, keepdims=True))
        koff = key_off(c)
        for p in range(N_PAIRS):
            kp = k_ref[0, pl.ds(koff, tk), p * LANES:(p + 1) * LANES]
            vp = v_ref[0, pl.ds(koff, tk), p * LANES:(p + 1) * LANES]
            s = lax.dot_general(qh_ref[p], kp, NT, preferred_element_type=F32).reshape(hpg, tq, tk)
            sm = [chunk(s, j) + bias[j][None] for j in range(nch)]
            mx_ = sm[0]
            for j in range(1, nch):
                mx_ = jnp.maximum(mx_, sm[j])
            m_old = m_ref[p]
            m_new = jnp.maximum(m_old, bcast(jnp.max(mx_, axis=-1, keepdims=True)))
            pe = [jnp.exp2(sm[j] - m_new) for j in range(nch)]
            ps = pe[0]
            for j in range(1, nch):
                ps = ps + pe[j]
            alpha = jnp.exp2(m_old - m_new)
            l_ref[p] = alpha * l_ref[p] + bcast(jnp.sum(ps, axis=-1, keepdims=True))
            pv = jnp.dot(jnp.concatenate(pe, axis=-1).reshape(hpg * tq, tk).astype(BF16), vp,
                         preferred_element_type=F32)
            acc_ref[p] = alpha.reshape(hpg * tq, LANES) * acc_ref[p] + pv
            m_ref[p] = m_new
        return cum

    kv_loop(p2, jnp.zeros((tq, LANES), F32))

    def head_out(h):
        p, hh = h // hpg, h % hpg
        r = acc_ref[p, hh * tq:(hh + 1) * tq] / l_ref[p, hh]
        if (h // GQA) % 2 != h % 2:
            r = pltpu.roll(r, HEAD_DIM, 1)
        return r

    for j in range(N_HEADS // 2):
        o_ref[0, :, j * LANES:(j + 1) * LANES] = jnp.where(lo_half, head_out(2 * j), head_out(2 * j + 1)).astype(o_ref.dtype)


def _attention(q, qi, wi, k_all, v_all, ki2, *, tq, tk, pos0, s_valid, kk):
    b, tq_pad, _ = q.shape
    s_pad = k_all.shape[1]
    nblk = s_pad // tk
    n_static = min(nblk, (pos0 + tq - 1) // tk + 1) if tq_pad == tq else None
    qblk = lambda w: pl.BlockSpec((1, tq, w), lambda bi, qi_: (bi, qi_, 0))
    seq = lambda w: pl.BlockSpec((1, s_pad, w), lambda bi, qi_: (bi, 0, 0))
    return pl.pallas_call(
        functools.partial(_attn_kernel, tq=tq, tk=tk, nblk=nblk, n_static=n_static, pos0=pos0, s_valid=s_valid, kk=kk),
        out_shape=jax.ShapeDtypeStruct((b, tq_pad, D_ATTN), BF16),
        grid=(b, tq_pad // tq),
        in_specs=[qblk(D_ATTN), qblk(IDX_HEADS * IDX_DIM), qblk(LANES), seq(D_KV), seq(D_KV), seq(LANES)],
        out_specs=qblk(D_ATTN),
        scratch_shapes=[pltpu.VMEM((nblk, tq, tk), F32), pltpu.VMEM((N_PAIRS, HEADS_PER_PAIR * tq, LANES), BF16),
                        pltpu.VMEM((IDX_HEADS * tq, LANES), BF16), pltpu.VMEM((IDX_HEADS, tq, LANES), F32),
                        pltpu.VMEM((N_PAIRS, HEADS_PER_PAIR, tq, LANES), F32),
                        pltpu.VMEM((N_PAIRS, HEADS_PER_PAIR, tq, LANES), F32),
                        pltpu.VMEM((N_PAIRS, HEADS_PER_PAIR * tq, LANES), F32)],
        compiler_params=_cparams("parallel", "parallel"),
        name="sparse_attention",
    )(q, qi, wi, k_all, v_all, ki2)


POOL_HALO = 16


def _pool_kernel(u_ref, halo_ref, st_ref, w_ref, scale_ref, o_ref, ext_ref, *, tb, pos0):
    i = pl.program_id(1)
    ext_ref[0:POOL_HALO] = jnp.where(i == 0, st_ref[0], halo_ref[0])
    ext_ref[POOL_HALO:POOL_HALO + tb] = u_ref[0]
    pos = pos0 + i * tb + lax.broadcasted_iota(jnp.int32, (tb, 1), 0)
    for gi, w in enumerate(POOL_WINDOWS):
        sl = slice(gi * POOL_GROUP, (gi + 1) * POOL_GROUP)
        tot = ext_ref[POOL_HALO:POOL_HALO + tb, sl]
        for k in range(1, w):
            tot = tot + ext_ref[POOL_HALO - k:POOL_HALO - k + tb, sl]
        cnt = jnp.minimum(w, pos + 1).astype(F32)
        pooled = tot / cnt - u_ref[0, :, sl]
        mixed = jnp.dot(pooled.astype(BF16), w_ref[gi], preferred_element_type=F32) * scale_ref[:, sl]
        o_ref[0, :, sl] = mixed.astype(o_ref.dtype)


def _pool(u, state16, w, scale, *, pos0):
    b, t, c = u.shape
    tb = SEQ_BLOCK
    hb = tb // POOL_HALO
    return pl.pallas_call(
        functools.partial(_pool_kernel, tb=tb, pos0=pos0),
        name="pool_mix",
        out_shape=jax.ShapeDtypeStruct((b, t, c), BF16),
        grid=(b, t // tb),
        in_specs=[pl.BlockSpec((1, tb, c), lambda bi, i: (bi, i, 0)),
                  pl.BlockSpec((1, POOL_HALO, c), lambda bi, i: (bi, jnp.maximum(i * hb - 1, 0), 0)),
                  pl.BlockSpec((1, POOL_HALO, c), lambda bi, i: (bi, 0, 0)),
                  pl.BlockSpec((len(POOL_WINDOWS), POOL_GROUP, POOL_GROUP), lambda bi, i: (0, 0, 0)),
                  pl.BlockSpec((1, c), lambda bi, i: (0, 0))],
        out_specs=pl.BlockSpec((1, tb, c), lambda bi, i: (bi, i, 0)),
        scratch_shapes=[pltpu.VMEM((POOL_HALO + tb, c), F32)],
        compiler_params=_cparams("parallel", "parallel"),
    )(u, u, state16, w, scale)


CONV_HALO = SUBLANES
SSD_GH = SSD_HEADS // SSD_GROUPS
SSD_GW = SSD_GH * SSD_HEAD_DIM
CONV_CHUNK = 512


def _ssd_kernel(z_ref, dt_ref, xbc_ref, halo_ref, st_ref, h0_ref, cw_ref, cb_ref, dtb_ref, alog_ref, dsk_ref,
                nw_ref, ex_ref, ext_t_ref, y_ref, hT_ref, ext_ref, xs_ref, h_ref, *, q, t_valid):
    c = pl.program_id(1)

    @pl.when(c == 0)
    def _():
        h_ref[...] = h0_ref[0]

    ext_ref[0:CONV_HALO] = jnp.where(c == 0, st_ref[0], halo_ref[0])
    ext_ref[CONV_HALO:CONV_HALO + q] = xbc_ref[0]
    for ch in range(D_SSD_CONV // CONV_CHUNK):
        cc = slice(ch * CONV_CHUNK, (ch + 1) * CONV_CHUNK)
        xc = cb_ref[:, cc]
        for j in range(SSD_CONV):
            off = CONV_HALO - (SSD_CONV - 1) + j
            xc = xc + ext_ref[off:off + q, cc] * cw_ref[j:j + 1, cc]
        xs_ref[:, cc] = _silu(xc)
    x = xs_ref[:, :D_SSD]

    lane = lax.broadcasted_iota(jnp.int32, (q, LANES), 1)
    row = lax.broadcasted_iota(jnp.int32, (q, LANES), 0)
    v = dt_ref[0] + dtb_ref[...]
    dt = jnp.maximum(v, 0.0) + jnp.log1p(jnp.exp(-jnp.abs(v)))
    dt = jnp.where(jnp.logical_and(lane < SSD_HEADS, c * q + row < t_valid), dt, 0.0)
    a = dt * (-jnp.exp(alog_ref[...]))
    tri = jnp.where(lane <= row, 1.0, 0.0).astype(BF16)
    causal = lane <= row
    cs = _dot3_left(tri, a)
    cs_t = cs.T
    ex = ex_ref[...]
    cs_last = cs[q - 1:q, :]
    dt_x = _dot3(dt, ex)
    ecs_x = jnp.exp(_dot3(cs, ex))
    dec_x = jnp.exp(_dot3(cs_last - cs, ex))
    xdt = x * dt_x
    xdec = (xdt * dec_x).astype(BF16)
    xdt16 = xdt.astype(BF16)
    rowdec = jnp.exp(_dot3_left(ext_t_ref[...], jnp.broadcast_to(cs_t[:, q - 1:q], (LANES, LANES))))
    lo_half = lane < SSD_HEAD_DIM

    for g in range(SSD_GROUPS):
        bg = xs_ref[:, D_SSD + g * SSD_STATE:D_SSD + (g + 1) * SSD_STATE].astype(BF16)
        cg = xs_ref[:, D_SSD + (SSD_GROUPS + g) * SSD_STATE:D_SSD + (SSD_GROUPS + g + 1) * SSD_STATE].astype(BF16)
        cb = lax.dot_general(cg, bg, NT, preferred_element_type=F32)
        hs = slice(g * SSD_GW, (g + 1) * SSD_GW)
        hprev = h_ref[hs, :]
        y_off = lax.dot_general(cg, hprev.astype(BF16), NT, preferred_element_type=F32) * ecs_x[:, hs]
        for jp in range(SSD_GH // 2):
            cols = slice(g * SSD_GW + jp * LANES, g * SSD_GW + (jp + 1) * LANES)
            xp = xdt16[:, cols]
            halves = []
            for e in range(2):
                hd = g * SSD_GH + 2 * jp + e
                seg = cs[:, hd:hd + 1] - cs_t[hd:hd + 1, :]
                lmat = jnp.where(causal, jnp.exp(jnp.where(causal, seg, 0.0)), 0.0)
                halves.append(jnp.dot((cb * lmat).astype(BF16), xp, preferred_element_type=F32))
            y_diag = jnp.where(lo_half, halves[0], halves[1])
            yv = y_diag + y_off[:, jp * LANES:(jp + 1) * LANES] + dsk_ref[:, cols] * xs_ref[:, cols]
            zv = z_ref[0, :, cols]
            ext_ref[0:q, cols] = yv * _silu(zv)
        st_new = lax.dot_general(xdec[:, hs], bg, TN, preferred_element_type=F32)
        h_ref[hs, :] = hprev * rowdec[hs, :] + st_new

    for g in range(SSD_GROUPS):
        hs = slice(g * SSD_GW, (g + 1) * SSD_GW)
        yg = ext_ref[0:q, hs]
        yn = yg * lax.rsqrt(jnp.mean(yg * yg, axis=-1, keepdims=True) + EPS) * nw_ref[:, hs]
        y_ref[0, :, hs] = yn.astype(y_ref.dtype)

    @pl.when(c == pl.num_programs(1) - 1)
    def _():
        hT_ref[0] = h_ref[...]


def _dot3_left(w, v):
    h1, h2, h3 = _split3(v)
    return (jnp.dot(w, h1, preferred_element_type=F32) + jnp.dot(w, h2, preferred_element_type=F32)
            + jnp.dot(w, h3, preferred_element_type=F32))


def _ssd(zdt, xbc, state8, h0, h_off, cw, cb, dtb, alog, dsk, nw, ex, ext_t, *, t_valid):
    b, t, _ = xbc.shape
    q = SEQ_BLOCK
    hb = q // CONV_HALO
    cst = lambda r, c_: pl.BlockSpec((r, c_), lambda bi, i: (0, 0))
    return pl.pallas_call(
        functools.partial(_ssd_kernel, q=q, t_valid=t_valid),
        name="ssd_scan",
        out_shape=[jax.ShapeDtypeStruct((b, t, D_SSD), BF16), jax.ShapeDtypeStruct((b, D_SSD, SSD_STATE), F32)],
        grid=(b, t // q),
        in_specs=[pl.BlockSpec((1, q, D_SSD), lambda bi, i: (bi, i, 0)),
                  pl.BlockSpec((1, q, LANES), lambda bi, i: (bi, i, D_SSD // LANES)),
                  pl.BlockSpec((1, q, D_SSD_CONV), lambda bi, i: (bi, i, 0)),
                  pl.BlockSpec((1, CONV_HALO, D_SSD_CONV), lambda bi, i: (bi, jnp.maximum(i * hb - 1, 0), 0)),
                  pl.BlockSpec((1, CONV_HALO, D_SSD_CONV), lambda bi, i: (bi, 0, 0)),
                  pl.BlockSpec((1, D_SSD, SSD_STATE), lambda bi, i: (h_off + bi, 0, 0)),
                  cst(SSD_CONV, D_SSD_CONV), cst(1, D_SSD_CONV), cst(1, LANES), cst(1, LANES), cst(1, D_SSD),
                  cst(1, D_SSD), cst(LANES, D_SSD), cst(D_SSD, LANES)],
        out_specs=[pl.BlockSpec((1, q, D_SSD), lambda bi, i: (bi, i, 0)),
                   pl.BlockSpec((1, D_SSD, SSD_STATE), lambda bi, i: (bi, 0, 0))],
        scratch_shapes=[pltpu.VMEM((CONV_HALO + q, D_SSD_CONV), F32), pltpu.VMEM((q, D_SSD_CONV), F32),
                        pltpu.VMEM((D_SSD, SSD_STATE), F32)],
        compiler_params=_cparams("parallel", "arbitrary"),
    )(zdt, zdt, xbc, xbc, state8, h0, cw, cb, dtb, alog, dsk, nw, ex, ext_t)


def _merge_kernel(x_ref, oa_ref, op_ref, os_ref, gt_ref, g1_ref, wa_ref, wp_ref, ws_ref, wo_ref, o_ref):
    def gate(bi):
        return jax.nn.sigmoid(gt_ref[:, bi * D_MODEL:(bi + 1) * D_MODEL])
    merged = (gate(0) * jnp.dot(oa_ref[...], wa_ref[...], preferred_element_type=F32)
              + gate(1) * jnp.dot(op_ref[...], wp_ref[...], preferred_element_type=F32)
              + gate(2) * jnp.dot(os_ref[...], ws_ref[...], preferred_element_type=F32))
    o_ref[...] = x_ref[...] + g1_ref[0] * jnp.dot(merged.astype(BF16), wo_ref[...], preferred_element_type=F32)


def _merge(x, oa, op, os_, gates, mv_g1, mv_idx, wa, wp, ws, wo, tm):
    n, d = x.shape
    rb = mv_g1.shape[1]
    rowb = lambda w: pl.BlockSpec((tm, w), lambda i: (i, 0))
    cst = lambda r, c: pl.BlockSpec((r, c), lambda i: (0, 0))
    return pl.pallas_call(
        _merge_kernel,
        name="merge_out",
        out_shape=jax.ShapeDtypeStruct((n, d), F32),
        grid=(n // tm,),
        in_specs=[rowb(d), rowb(D_ATTN), rowb(D_POOL), rowb(D_SSD), rowb(N_BRANCH * D_MODEL),
                  pl.BlockSpec((1, rb, d), lambda i: (mv_idx(i), 0, 0)),
                  cst(D_ATTN, d), cst(D_POOL, d), cst(D_SSD, d), cst(d, d)],
        out_specs=rowb(d),
        compiler_params=_cparams("parallel"),
    )(x, oa, op, os_, gates, mv_g1, wa, wp, ws, wo)


FFN_TILE = D_FF // 2
FFN_ROWS = 512
FFN_SLOT = 16


def _ffn_kernel(x_ref, hx_ref, sa_ref, sg_ref, sh_ref, sc_ref, g2_ref, gn_ref, wu_ref, cwa_ref, cwg_ref, cba_ref,
                cbg_ref, wd_ref, o_ref, ta_ref, tg_ref, xin_ref, ea_ref, eg_ref, acc_ref, *, tb, t_valid, tail_row, slot):
    i = pl.program_id(1)
    if slot is None:
        carried_in = [(0, 0)]
        tails = [(tail_row, 0)]
    else:
        carried_in = [(CONV_HALO + k * slot, k) for k in range(tb // slot)]
        tails = [(CONV_HALO + k * slot + t_valid, k) for k in range(tb // slot)]
    xin_ref[0:CONV_HALO] = hx_ref[0]
    xin_ref[CONV_HALO:CONV_HALO + tb] = x_ref[0]
    xin = xin_ref[...]
    y = xin * lax.rsqrt(jnp.mean(xin * xin, axis=-1, keepdims=True) + EPS) * gn_ref[...]

    def rows_of(m):
        return m if m.shape[0] == 1 else jnp.concatenate([m[:CONV_HALO], m], axis=0)

    h = (y * (1.0 + rows_of(sc_ref[0])) + rows_of(sh_ref[0])).astype(BF16)
    for f in range(D_FF // FFN_TILE):
        cols = slice(f * FFN_TILE, (f + 1) * FFN_TILE)

        def conv(e_ref, s_ref, t_ref, cw_ref, cb_ref, col0):
            u = jnp.dot(h, wu_ref[:, col0 + f * FFN_TILE:col0 + (f + 1) * FFN_TILE], preferred_element_type=F32)
            e_ref[...] = u

            def take_carried():
                for row, k in carried_in:
                    e_ref[row:row + CONV_HALO] = s_ref[k, :, cols]

            def emit_tails():
                for row, k in tails:
                    t_ref[k, :, cols] = e_ref[row:row + CONV_HALO]

            if slot is None:
                pl.when(i == 0)(take_carried)
                pl.when(i == pl.num_programs(1) - 1)(emit_tails)
            else:
                take_carried()
                emit_tails()
            out = cb_ref[:, cols]
            for j in range(FFN_CONV):
                off = CONV_HALO - (FFN_CONV - 1) + j
                out = out + e_ref[off:off + tb] * cw_ref[j:j + 1, cols]
            return out

        a = conv(ea_ref, sa_ref, ta_ref, cwa_ref, cba_ref, 0)
        g = conv(eg_ref, sg_ref, tg_ref, cwg_ref, cbg_ref, D_FF)
        part = jnp.dot((_silu(g) * a).astype(BF16), wd_ref[cols, :], preferred_element_type=F32)
        if f == 0:
            acc_ref[...] = part
        else:
            acc_ref[...] += part
    o_ref[0] = x_ref[0] + g2_ref[0] * acc_ref[...]


def _ffn(x, sa, sg, sh2, sc2, g2, gn, wu, cwa, cwg, cba, cbg, wd, *, tb, t_valid, slot=None):
    b, t, d = x.shape
    n = sa.shape[0]
    hb = tb // CONV_HALO
    blk = pl.BlockSpec((1, tb, d), lambda bi, i: (bi, i, 0))
    halo = pl.BlockSpec((1, CONV_HALO, d), lambda bi, i: (bi, jnp.maximum(i * hb - 1, 0), 0))
    if slot is None:
        st = pl.BlockSpec((1, CONV_HALO, D_FF), lambda bi, i: (bi, 0, 0))
        seqv = pl.BlockSpec((1, 1, d), lambda bi, i: (bi, 0, 0))
    else:
        st = pl.BlockSpec((tb // slot, CONV_HALO, D_FF), lambda bi, i: (i, 0, 0))
        seqv = blk
    cst = lambda r, c: pl.BlockSpec((r, c), lambda bi, i: (0, 0))
    return pl.pallas_call(
        functools.partial(_ffn_kernel, tb=tb, t_valid=t_valid, tail_row=t_valid - (t // tb - 1) * tb, slot=slot),
        name="conv_ffn",
        out_shape=[jax.ShapeDtypeStruct((b, t, d), F32), jax.ShapeDtypeStruct((n, CONV_HALO, D_FF), F32),
                   jax.ShapeDtypeStruct((n, CONV_HALO, D_FF), F32)],
        grid=(b, t // tb),
        in_specs=[blk, halo, st, st, seqv, seqv, seqv, cst(1, d), cst(d, 2 * D_FF), cst(FFN_CONV, D_FF),
                  cst(FFN_CONV, D_FF), cst(1, D_FF), cst(1, D_FF), cst(D_FF, d)],
        out_specs=[blk, st, st],
        scratch_shapes=[pltpu.VMEM((CONV_HALO + tb, d), F32), pltpu.VMEM((CONV_HALO + tb, FFN_TILE), F32),
                        pltpu.VMEM((CONV_HALO + tb, FFN_TILE), F32), pltpu.VMEM((tb, d), F32)],
        compiler_params=_cparams("parallel", "arbitrary"),
    )(x, x, sa, sg, sh2, sc2, g2, gn, wu, cwa, cwg, cba, cbg, wd)


def _pad_rows(a, rows, front=False):
    extra = rows - a.shape[1]
    if extra == 0:
        return a
    return jnp.pad(a, ((0, 0), (extra, 0) if front else (0, extra), (0, 0)))


def _rope_tables(pos):
    half = HEAD_DIM // 2
    freqs = ROPE_THETA ** (-jnp.arange(half, dtype=jnp.float32) / half)
    ang = pos.astype(jnp.float32)[:, None] * freqs[None, :]
    cos, sin = jnp.cos(ang), jnp.sin(ang)
    return jnp.concatenate([cos, cos, cos, cos], axis=-1), jnp.concatenate([-sin, sin, -sin, sin], axis=-1)


def _layer_weights(l, w_ada, b_ada, norm1, norm2, w_in, q_norm, k_norm, pool_w, pool_scale, ssd_conv_w, ssd_conv_b,
                   ssd_dt_bias, ssd_a_log, ssd_d, ssd_norm, w_branch_attn, w_branch_pool, w_branch_ssd, w_out,
                   ffn_up, ffn_conv_w, ffn_conv_b, ffn_down):
    o = _OFFS
    win = w_in[l]
    col = lambda a, b: win[:, a:b]
    zpad = lambda n: jnp.zeros((D_MODEL, n), win.dtype)
    row = lambda v: v.reshape(1, -1).astype(F32)
    lanepad = lambda v: jnp.pad(v.astype(F32), (0, LANES - v.shape[0])).reshape(1, LANES)
    return dict(
        w_ada=w_ada[l].astype(BF16), b_ada=row(b_ada[l]), norm1=row(norm1[l]), norm2=row(norm2[l]),
        w_attn=jnp.concatenate([col(o[0], o[6]), zpad(LANES - IDX_DIM - IDX_HEADS)], axis=1).astype(BF16),
        w_pool=col(o[6], o[7]).astype(BF16),
        w_zdt=jnp.concatenate([col(o[7], o[8]), col(o[9], o[10]), zpad(LANES - SSD_HEADS)], axis=1).astype(BF16),
        w_xbc=col(o[8], o[9]).astype(BF16),
        w_gate=col(o[10], o[11]).astype(BF16),
        qn=row(jnp.tile(q_norm[l], LANES // HEAD_DIM)), kn=row(jnp.tile(k_norm[l], LANES // HEAD_DIM)),
        pool_w=pool_w[l].astype(BF16), pool_scale=row(pool_scale[l]),
        cw=ssd_conv_w[l].astype(F32), cb=row(ssd_conv_b[l]), dtb=lanepad(ssd_dt_bias[l]), alog=lanepad(ssd_a_log[l]),
        dsk=row(jnp.repeat(ssd_d[l], SSD_HEAD_DIM)), nw=row(ssd_norm[l]),
        wa=w_branch_attn[l].astype(BF16), wp=w_branch_pool[l].astype(BF16), ws=w_branch_ssd[l].astype(BF16),
        wo=w_out[l].astype(BF16),
        w_up=ffn_up[l].astype(BF16),
        cwa=ffn_conv_w[l][:, :D_FF].astype(F32), cwg=ffn_conv_w[l][:, D_FF:].astype(F32),
        cba=row(ffn_conv_b[l][:D_FF]), cbg=row(ffn_conv_b[l][D_FF:]), wd=ffn_down[l].astype(BF16))


def _group_layer(x, mod, lw, consts, *, pos0, past, states):
    b, t, d = x.shape
    n = b * t
    tp = -(-t // SEQ_BLOCK) * SEQ_BLOCK
    per_seq = t % SEQ_BLOCK == 0
    tm = min(ROW_TILE, t) if per_seq else n
    sh1, sc1, g1, sh2, sc2, g2 = [mod[:, i * d:(i + 1) * d] for i in range(6)]
    pos = pos0 + jnp.arange(t)
    cos_t, sin_t = _rope_tables(pos)
    if per_seq:
        view = lambda m: m.reshape(b, 1, d)
        mv_idx = lambda i: i // (t // tm)
        tab = lambda a: a.reshape(t // tm, tm, LANES)
        tab_idx = lambda i: i % (t // tm)
    else:
        view = lambda m: jnp.repeat(m, t, axis=0).reshape(1, n, d)
        mv_idx = lambda i: 0
        tab = lambda a: jnp.tile(a, (b, 1)).reshape(1, n, LANES)
        tab_idx = lambda i: 0
    xf = x.reshape(n, d)
    nm = lambda w: _norm_matmul(xf, view(sh1), view(sc1), mv_idx, lw['norm1'], w, tm)
    pa, pool_u, zdt, xbc, gates = nm(lw['w_attn']), nm(lw['w_pool']), nm(lw['w_zdt']), nm(lw['w_xbc']), nm(lw['w_gate'])

    q_r, k_r, qi_r, ki2, wi_s = _attn_prep(pa, tab(cos_t), tab(sin_t), tab_idx, lw['qn'], lw['kn'], consts['seg'], tm)
    v_new = pa[:, D_ATTN + D_KV:D_ATTN + 2 * D_KV]
    seq3 = lambda a: a.reshape(b, t, a.shape[-1])
    if past is None:
        tq, tk = ATTN_Q_ROWS, min(ATTN_KEYS, t)
        k_all, v_all, ki_all = seq3(k_r).astype(BF16), seq3(v_new).astype(BF16), seq3(ki2).astype(BF16)
        s_valid = t
        qpad = lambda a: seq3(a)
    else:
        tq = 2 * SUBLANES
        page_table, pool_off, ck, cv, ci = past
        k_all, v_all, ki_all = _page_gather(page_table, pool_off, ck, cv, ci, seq3(k_r), seq3(v_new), seq3(ki2))
        tk = min(ATTN_KEYS_DECODE, k_all.shape[1] // 2)
        s_valid = pos0 + t
        qpad = lambda a: _pad_rows(seq3(a), tq)
    assert k_all.shape[1] % tk == 0
    kk = min(INDEX_TOPK, s_valid // 4)
    o_attn = _attention(qpad(q_r), qpad(qi_r), qpad(wi_s), k_all, v_all, ki_all,
                        tq=tq, tk=tk, pos0=pos0, s_valid=s_valid, kk=kk)[:, :t].reshape(n, D_ATTN)

    st_pool, st_sconv, (st_ssd, ssd_off), st_fconv = states
    seqp = lambda a: _pad_rows(seq3(a), tp)
    o_pool = _pool(seqp(pool_u), _pad_rows(st_pool, POOL_HALO, front=True), lw['pool_w'], lw['pool_scale'], pos0=pos0)
    o_ssd, ssd_state = _ssd(seqp(zdt), seqp(xbc), _pad_rows(st_sconv, CONV_HALO, front=True),
                            st_ssd, ssd_off, lw['cw'], lw['cb'], lw['dtb'], lw['alog'], lw['dsk'],
                            lw['nw'], consts['ex'], consts['ext_t'], t_valid=t)
    unpad = lambda a: a[:, :t].reshape(n, a.shape[-1])
    x1 = _merge(xf, o_attn, unpad(o_pool), unpad(o_ssd), gates, view(g1), mv_idx,
                lw['wa'], lw['wp'], lw['ws'], lw['wo'], tm)

    st_f = _pad_rows(st_fconv, CONV_HALO, front=True)
    ffn_w = (lw['norm2'], lw['w_up'], lw['cwa'], lw['cwg'], lw['cba'], lw['cbg'], lw['wd'])
    if per_seq or t > FFN_SLOT - CONV_HALO:
        per_b = lambda m: m.reshape(b, 1, d)
        x2, tail_a, tail_g = _ffn(seqp(x1), st_f[:, :, :D_FF], st_f[:, :, D_FF:], per_b(sh2), per_b(sc2), per_b(g2),
                                  *ffn_w, tb=min(FFN_ROWS, tp), t_valid=t)
        x2 = x2[:, :t]
    else:
        in_slot = lambda a: jnp.pad(a, ((0, 0), (CONV_HALO, FFN_SLOT - CONV_HALO - t), (0, 0))).reshape(1, b * FFN_SLOT, d)
        per_row = lambda m: jnp.repeat(m, FFN_SLOT, axis=0).reshape(1, b * FFN_SLOT, d)
        slots_per_step = max(k for k in range(1, FFN_ROWS // FFN_SLOT + 1) if b % k == 0)
        x2, tail_a, tail_g = _ffn(in_slot(seq3(x1)), st_f[:, :, :D_FF], st_f[:, :, D_FF:], per_row(sh2), per_row(sc2),
                                  per_row(g2), *ffn_w, tb=slots_per_step * FFN_SLOT, t_valid=t, slot=FFN_SLOT)
        x2 = x2.reshape(b, FFN_SLOT, d)[:, CONV_HALO:CONV_HALO + t]
    fconv_state = jnp.concatenate([tail_a, tail_g], axis=-1)[:, -(FFN_CONV - 1):]

    def tail(prev, news, keep):
        last = jnp.concatenate([seq3(a)[:, -min(keep, t):] for a in news], axis=-1)
        return jnp.concatenate([prev, last], axis=1)[:, -keep:]

    new_states = (seq3(k_r).reshape(b, t, N_KV_HEADS, HEAD_DIM), seq3(v_new).reshape(b, t, N_KV_HEADS, HEAD_DIM),
                  seq3(ki2)[:, :, :IDX_DIM], tail(st_pool, [pool_u], POOL_CTX), tail(st_sconv, [xbc], SSD_CONV - 1),
                  ssd_state.reshape(b, SSD_HEADS, SSD_HEAD_DIM, SSD_STATE), fconv_state)
    return x2, new_states


def kernel(x_prompt, x_sample, c_prompt, c_sample, cache_k, cache_v, cache_kidx, page_table, state_pool, state_ssd_conv, state_ssd, state_ffn_conv, w_ada, b_ada, norm1, norm2, w_in, q_norm, k_norm, pool_w, pool_scale, ssd_conv_w, ssd_conv_b, ssd_dt_bias, ssd_a_log, ssd_d, ssd_norm, w_branch_attn, w_branch_pool, w_branch_ssd, w_out, ffn_up, ffn_conv_w, ffn_conv_b, ffn_down):
    bp, tp, d = x_prompt.shape
    bs, ts, _ = x_sample.shape
    depth = w_ada.shape[0]
    n_pool, page = cache_k.shape[1], cache_k.shape[2]
    past_len = page_table.shape[1] * page
    dt = x_prompt.dtype

    r = lax.broadcasted_iota(jnp.int32, (LANES, LANES), 0)
    c = lax.broadcasted_iota(jnp.int32, (LANES, LANES), 1)
    hr = lax.broadcasted_iota(jnp.int32, (LANES, D_SSD), 0)
    hc = lax.broadcasted_iota(jnp.int32, (LANES, D_SSD), 1)
    ex = (hr == hc // SSD_HEAD_DIM).astype(BF16)
    consts = dict(seg=(r // HEAD_DIM == c // HEAD_DIM).astype(BF16), ex=ex, ext_t=ex.T)

    c_all = jnp.concatenate([c_prompt, c_sample], axis=0)
    rows = -(-c_all.shape[0] // SUBLANES) * SUBLANES
    c_pad = jnp.pad(c_all, ((0, rows - c_all.shape[0]), (0, 0)))

    ck_t = jnp.transpose(cache_k, (0, 1, 3, 4, 2)).reshape(depth * n_pool, D_KV, page)
    cv_t = jnp.transpose(cache_v, (0, 1, 3, 4, 2)).reshape(depth * n_pool, D_KV, page)
    ci_t = jnp.transpose(cache_kidx, (0, 1, 3, 2)).reshape(depth * n_pool, IDX_DIM, page)

    ssd_all = state_ssd.reshape(depth * bs, D_SSD, SSD_STATE)
    zeros = lambda *s: jnp.zeros(s, dt)
    xp, xs = x_prompt, x_sample
    new_p, new_s = [[] for _ in range(7)], [[] for _ in range(7)]
    for l in range(depth):
        lw = _layer_weights(l, w_ada, b_ada, norm1, norm2, w_in, q_norm, k_norm, pool_w, pool_scale, ssd_conv_w,
                            ssd_conv_b, ssd_dt_bias, ssd_a_log, ssd_d, ssd_norm, w_branch_attn, w_branch_pool,
                            w_branch_ssd, w_out, ffn_up, ffn_conv_w, ffn_conv_b, ffn_down)
        mod = _mod(c_pad, lw['w_ada'], lw['b_ada'])
        xp, st_p = _group_layer(
            xp, mod[:bp], lw, consts, pos0=0, past=None,
            states=(zeros(bp, POOL_CTX, D_POOL), zeros(bp, SSD_CONV - 1, D_SSD_CONV),
                    (zeros(bp, D_SSD, SSD_STATE), 0), zeros(bp, FFN_CONV - 1, 2 * D_FF)))
        past = (page_table, l * n_pool, ck_t, cv_t, ci_t)
        xs, st_s = _group_layer(xs, mod[bp:bp + bs], lw, consts, pos0=past_len, past=past,
                                states=(state_pool[l], state_ssd_conv[l], (ssd_all, l * bs), state_ffn_conv[l]))
        for j in range(7):
            new_p[j].append(st_p[j])
            new_s[j].append(st_s[j])
    outs_p = [jnp.stack(s) for s in new_p]
    outs_s = [jnp.stack(s) for s in new_s]
    return (xp, xs, *outs_p, *outs_s)
```

```python
import functools

import numpy as np
import jax
import jax.numpy as jnp
from jax import lax
from jax.experimental import pallas as pl
from jax.experimental.pallas import tpu as pltpu

D_MODEL = 1024
N_HEADS = 16
N_KV_HEADS = 4
HEAD_DIM = 64
D_ATTN = N_HEADS * HEAD_DIM
D_KV = N_KV_HEADS * HEAD_DIM
IDX_HEADS = 8
IDX_DIM = 64
INDEX_TOPK = 256
ROPE_THETA = 10000.0
POOL_WINDOWS = (2, 4, 8, 16)
D_POOL = D_MODEL
POOL_GROUP = D_POOL // len(POOL_WINDOWS)
POOL_CTX = max(POOL_WINDOWS) - 1
D_SSD = 2 * D_MODEL
SSD_HEAD_DIM = 64
SSD_HEADS = D_SSD // SSD_HEAD_DIM
SSD_GROUPS = 4
SSD_STATE = 128
SSD_CONV = 4
D_SSD_CONV = D_SSD + 2 * SSD_GROUPS * SSD_STATE
D_FF = 2816
FFN_CONV = 3
N_BRANCH = 3
EPS = 1e-6
_SPLITS = (D_ATTN, D_KV, D_KV, IDX_HEADS * IDX_DIM, IDX_DIM, IDX_HEADS, D_POOL, D_SSD, D_SSD_CONV,
           SSD_HEADS, N_BRANCH * D_MODEL)
_OFFS = tuple(int(v) for v in np.cumsum((0,) + _SPLITS))

F32 = jnp.float32
BF16 = jnp.bfloat16
LANES = 128
SUBLANES = 8
SEQ_BLOCK = 128
ROW_TILE = 512
ATTN_Q_ROWS = 256
ATTN_KEYS = 256
ATTN_KEYS_DECODE = 1024
VMEM_LIMIT = 56 * 1024 * 1024
NEG = -1e30
LOG2E = 1.4426950408889634
N_BISECT = 20
NT = (((1,), (1,)), ((), ()))
TN = (((0,), (0,)), ((), ()))


def _cparams(*sem):
    return pltpu.CompilerParams(dimension_semantics=sem, vmem_limit_bytes=VMEM_LIMIT)


def _silu(v):
    return (0.5 * v) * (1.0 + jnp.tanh(0.5 * v))


def _split3(v):
    h1 = v.astype(BF16)
    r1 = v - h1.astype(F32)
    h2 = r1.astype(BF16)
    h3 = (r1 - h2.astype(F32)).astype(BF16)
    return h1, h2, h3


def _dot3(v, w):
    h1, h2, h3 = _split3(v)
    return (jnp.dot(h1, w, preferred_element_type=F32) + jnp.dot(h2, w, preferred_element_type=F32)
            + jnp.dot(h3, w, preferred_element_type=F32))


def _mod_kernel(c_ref, w_ref, b_ref, o_ref):
    s = _silu(c_ref[...])
    o_ref[...] = jnp.dot(s.astype(BF16), w_ref[...], preferred_element_type=F32) + b_ref[...]


def _mod(c_pad, w, b):
    rows, d = c_pad.shape
    n = w.shape[1]
    tn = n // 4
    return pl.pallas_call(
        _mod_kernel,
        name="adaln_mod",
        out_shape=jax.ShapeDtypeStruct((rows, n), F32),
        grid=(n // tn,),
        in_specs=[pl.BlockSpec((rows, d), lambda j: (0, 0)),
                  pl.BlockSpec((d, tn), lambda j: (0, j)),
                  pl.BlockSpec((1, tn), lambda j: (0, j))],
        out_specs=pl.BlockSpec((rows, tn), lambda j: (0, j)),
        compiler_params=_cparams("parallel"),
    )(c_pad, w, b)


def _norm_matmul_kernel(x_ref, sh_ref, sc_ref, g_ref, w_ref, o_ref):
    x = x_ref[...]
    y = x * lax.rsqrt(jnp.mean(x * x, axis=-1, keepdims=True) + EPS) * g_ref[...]
    h = y * (1.0 + sc_ref[0]) + sh_ref[0]
    o_ref[...] = jnp.dot(h.astype(BF16), w_ref[...], preferred_element_type=F32).astype(o_ref.dtype)


def _norm_matmul(x, mv_sh, mv_sc, mv_idx, g, w, tm):
    n, d = x.shape
    nout = w.shape[1]
    rb = mv_sh.shape[1]
    return pl.pallas_call(
        _norm_matmul_kernel,
        name="norm_matmul",
        out_shape=jax.ShapeDtypeStruct((n, nout), F32),
        grid=(n // tm,),
        in_specs=[pl.BlockSpec((tm, d), lambda i: (i, 0)),
                  pl.BlockSpec((1, rb, d), lambda i: (mv_idx(i), 0, 0)),
                  pl.BlockSpec((1, rb, d), lambda i: (mv_idx(i), 0, 0)),
                  pl.BlockSpec((1, d), lambda i: (0, 0)),
                  pl.BlockSpec((d, nout), lambda i: (0, 0))],
        out_specs=pl.BlockSpec((tm, nout), lambda i: (i, 0)),
        compiler_params=_cparams("parallel"),
    )(x, mv_sh, mv_sc, g, w)


def _attn_prep_kernel(q_ref, k_ref, qi_ref, kw_ref, cos_ref, sin_ref, qn_ref, kn_ref, seg_ref,
                      qo_ref, ko_ref, qio_ref, kio_ref, wio_ref):
    tm = q_ref.shape[0]
    lane = lax.broadcasted_iota(jnp.int32, (tm, LANES), 1)
    first_half = (lane % HEAD_DIM) < (HEAD_DIM // 2)
    cos = cos_ref[0]
    sin = sin_ref[0]
    seg = seg_ref[...]

    def rope(v):
        up = pltpu.roll(v, LANES - HEAD_DIM // 2, 1)
        dn = pltpu.roll(v, HEAD_DIM // 2, 1)
        return v * cos + jnp.where(first_half, up, dn) * sin

    def headnorm(v, gain):
        sq = v * v
        hi = sq.astype(BF16)
        lo = (sq - hi.astype(F32)).astype(BF16)
        ss = jnp.dot(hi, seg, preferred_element_type=F32) + jnp.dot(lo, seg, preferred_element_type=F32)
        return v * lax.rsqrt(ss * (1.0 / HEAD_DIM) + EPS) * gain

    qn = qn_ref[...]
    kn = kn_ref[...]
    for j in range(D_ATTN // LANES):
        sl = slice(j * LANES, (j + 1) * LANES)
        qo_ref[:, sl] = rope(headnorm(q_ref[:, sl], qn)) * (HEAD_DIM ** -0.5 * LOG2E)
    for j in range(D_KV // LANES):
        sl = slice(j * LANES, (j + 1) * LANES)
        ko_ref[:, sl] = rope(headnorm(k_ref[:, sl], kn))
    for j in range(IDX_HEADS * IDX_DIM // LANES):
        sl = slice(j * LANES, (j + 1) * LANES)
        qio_ref[:, sl] = rope(qi_ref[:, sl])
    kw = kw_ref[...]
    kr = rope(kw)
    kio_ref[...] = jnp.where(lane < IDX_DIM, kr, pltpu.roll(kr, IDX_DIM, 1))
    wio_ref[...] = pltpu.roll(kw, LANES - IDX_DIM, 1) * (IDX_HEADS ** -0.5 * IDX_DIM ** -0.5)


def _attn_prep(pa, cos_t, sin_t, tab_idx, qn, kn, seg, tm):
    n = pa.shape[0]
    blk = lambda w, c: pl.BlockSpec((tm, w), lambda i: (i, c))
    tab = pl.BlockSpec((1, tm, LANES), lambda i: (tab_idx(i), 0, 0))
    cst = lambda r, c: pl.BlockSpec((r, c), lambda i: (0, 0))
    return pl.pallas_call(
        _attn_prep_kernel,
        name="attn_prep",
        out_shape=[jax.ShapeDtypeStruct((n, D_ATTN), F32), jax.ShapeDtypeStruct((n, D_KV), F32),
                   jax.ShapeDtypeStruct((n, IDX_HEADS * IDX_DIM), F32),
                   jax.ShapeDtypeStruct((n, LANES), F32), jax.ShapeDtypeStruct((n, LANES), F32)],
        grid=(n // tm,),
        in_specs=[blk(D_ATTN, 0), blk(D_KV, D_ATTN // D_KV), blk(IDX_HEADS * IDX_DIM, 3), blk(LANES, 16),
                  tab, tab, cst(1, LANES), cst(1, LANES), cst(LANES, LANES)],
        out_specs=[blk(D_ATTN, 0), blk(D_KV, 0), blk(IDX_HEADS * IDX_DIM, 0), blk(LANES, 0), blk(LANES, 0)],
        compiler_params=_cparams("parallel"),
    )(pa, pa, pa, pa, cos_t, sin_t, qn, kn, seg)


GATHER_PAGES = 16


def _page_gather_kernel(pt_ref, *refs, n_steps, pg, page):
    ins, outs = refs[:3 * pg + 3], refs[3 * pg + 3:]
    past = pl.program_id(1) < n_steps
    for t in range(3):
        new_ref, out_ref = ins[3 * pg + t], outs[t]
        for i in range(pg):
            rows = slice(i * page, (i + 1) * page)
            tile = ins[t * pg + i][0]
            if t == 2:
                tile = jnp.concatenate([tile, tile], axis=0)
            out_ref[0, rows, :] = jnp.where(past, tile.T, new_ref[0, rows, :]).astype(out_ref.dtype)


def _page_gather(page_table, pool_off, ck, cv, ci, nk, nv, ni):
    bd, n_pages = page_table.shape
    page = ck.shape[2]
    pg = max(g for g in range(1, GATHER_PAGES + 1) if n_pages % g == 0)
    n_steps = n_pages // pg
    nk, nv, ni = [_pad_rows(a, pg * page) for a in (nk, nv, ni)]

    def cache(w, i):
        return pl.BlockSpec((1, w, page),
                            lambda b, p, pt: (pool_off + pt[b, jnp.minimum(p, n_steps - 1) * pg + i], 0, 0))

    new = lambda w: pl.BlockSpec((1, pg * page, w), lambda b, p, pt: (b, 0, 0))
    out = lambda w: pl.BlockSpec((1, pg * page, w), lambda b, p, pt: (b, p, 0))
    s_pad = (n_steps + 1) * pg * page
    in_specs = ([cache(D_KV, i) for i in range(pg)] + [cache(D_KV, i) for i in range(pg)]
                + [cache(IDX_DIM, i) for i in range(pg)] + [new(D_KV), new(D_KV), new(2 * IDX_DIM)])
    return pl.pallas_call(
        functools.partial(_page_gather_kernel, n_steps=n_steps, pg=pg, page=page),
        out_shape=[jax.ShapeDtypeStruct((bd, s_pad, D_KV), BF16), jax.ShapeDtypeStruct((bd, s_pad, D_KV), BF16),
                   jax.ShapeDtypeStruct((bd, s_pad, 2 * IDX_DIM), BF16)],
        grid_spec=pltpu.PrefetchScalarGridSpec(
            num_scalar_prefetch=1, grid=(bd, n_steps + 1), in_specs=in_specs,
            out_specs=[out(D_KV), out(D_KV), out(2 * IDX_DIM)]),
        compiler_params=_cparams("parallel", "parallel"),
        name="page_gather",
    )(page_table, *([ck] * pg), *([cv] * pg), *([ci] * pg), nk, nv, ni)


GQA = N_HEADS // N_KV_HEADS
N_PAIRS = D_KV // LANES
HEADS_PER_PAIR = N_HEADS // N_PAIRS


def _attn_kernel(q_ref, qi_ref, wi_ref, k_ref, v_ref, ki_ref, o_ref,
                 sc_ref, qh_ref, qim_ref, wb_ref, m_ref, l_ref, acc_ref, *, tq, tk, nblk, n_static, pos0, s_valid, kk):
    q0 = pl.program_id(1) * tq
    nch = tk // LANES
    hpg = HEADS_PER_PAIR
    lane = lax.broadcasted_iota(jnp.int32, (tq, LANES), 1)
    row = lax.broadcasted_iota(jnp.int32, (tq, LANES), 0)
    qpos = pos0 + q0 + row
    lo_half = lane < HEAD_DIM
    n_kv = jnp.minimum(nblk, (pos0 + q0 + tq - 1) // tk + 1)
    kkf = float(kk)
    chunk = lambda a, j: a[..., j * LANES:(j + 1) * LANES]
    rg = min(tq, LANES)

    def kv_loop(body, init):
        if n_static is None:
            return lax.fori_loop(0, n_kv, body, init)
        for c in range(n_static):
            init = body(c, init)
        return init

    def key_off(c):
        return c * tk if n_static is not None else pl.multiple_of(c * tk, tk)

    def bcast(col):
        return jnp.broadcast_to(col, col.shape[:-1] + (LANES,))

    wi = wi_ref[0]
    for j in range(IDX_HEADS // 2):
        t = qi_ref[0, :, j * LANES:(j + 1) * LANES]
        qim_ref[2 * j * tq:(2 * j + 1) * tq] = jnp.where(lo_half, t, 0.0).astype(BF16)
        qim_ref[(2 * j + 1) * tq:(2 * j + 2) * tq] = jnp.where(lo_half, 0.0, t).astype(BF16)
    for h in range(IDX_HEADS):
        wb_ref[h] = bcast(wi[:, h:h + 1])
    for h in range(N_HEADS):
        t = q_ref[0, :, (h // 2) * LANES:(h // 2 + 1) * LANES]
        kv_half = (h // GQA) % 2
        if h % 2 != kv_half:
            t = pltpu.roll(t, HEAD_DIM, 1)
        keep = lo_half if kv_half == 0 else jnp.logical_not(lo_half)
        hh = h % hpg
        qh_ref[h // hpg, hh * tq:(hh + 1) * tq] = jnp.where(keep, t, 0.0).astype(BF16)
    m_ref[...] = jnp.full(m_ref.shape, NEG, F32)
    l_ref[...] = jnp.zeros(l_ref.shape, F32)
    acc_ref[...] = jnp.zeros(acc_ref.shape, F32)

    def p1(c, carry):
        mn, mx = carry
        kid = ki_ref[0, pl.ds(key_off(c), tk), :]
        y = lax.dot_general(qim_ref[...], kid, NT, preferred_element_type=F32)
        y = jnp.maximum(y, 0.0).reshape(IDX_HEADS, tq, tk)
        for j in range(nch):
            acc = wb_ref[0] * chunk(y[0], j)
            for h in range(1, IDX_HEADS):
                acc = acc + wb_ref[h] * chunk(y[h], j)
            kpos = c * tk + j * LANES + lane
            vis = jnp.logical_and(kpos <= qpos, kpos < s_valid)
            sc_ref[c, :, j * LANES:(j + 1) * LANES] = jnp.where(vis, acc, -jnp.inf)
            mn = jnp.minimum(mn, jnp.where(vis, acc, jnp.inf))
            mx = jnp.maximum(mx, jnp.where(vis, acc, -jnp.inf))
        return mn, mx

    mn, mx = kv_loop(p1, (jnp.full((tq, LANES), jnp.inf, F32), jnp.full((tq, LANES), -jnp.inf, F32)))
    rmin = bcast(jnp.min(mn, axis=1, keepdims=True))
    rmax = bcast(jnp.max(mx, axis=1, keepdims=True))

    def row_pass(step, init, reduce):
        accs = []
        for r in range(tq // rg):
            rows = slice(r * rg, (r + 1) * rg)

            def body(c, acc, rows=rows):
                x = sc_ref[c, rows, :]
                for j in range(nch):
                    acc = step(acc, chunk(x, j), rows)
                return acc
            accs.append(kv_loop(body, jnp.full((rg, LANES), init, F32)))
        outs = [bcast(reduce(a, axis=1, keepdims=True)) for a in accs]
        return outs[0] if len(outs) == 1 else jnp.concatenate(outs, axis=0)

    def count_gt(t_b):
        return row_pass(lambda acc, xj, rows: acc + jnp.where(xj > t_b[rows], 1.0, 0.0), 0.0, jnp.sum)

    def next_above(t_b):
        return row_pass(lambda acc, xj, rows: jnp.minimum(acc, jnp.where(xj > t_b[rows], xj, jnp.inf)), jnp.inf, jnp.min)

    at_min = jnp.minimum(qpos + 1, s_valid).astype(F32) < kkf
    below = rmin - jnp.maximum((rmax - rmin) * 2.0 ** -8, jnp.abs(rmin) * 2.0 ** -20 + 1e-30)

    def bisect(i, carry):
        lo, hi = carry
        mid = lo + 0.5 * (hi - lo)
        up = count_gt(mid) >= kkf
        return jnp.where(up, mid, lo), jnp.where(up, hi, mid)

    lo, hi = lax.fori_loop(0, N_BISECT, bisect, (below, rmax))
    tau = next_above(lo)
    ct = count_gt(tau)

    def pending(ct_):
        return jnp.logical_and(ct_ >= kkf, jnp.logical_not(at_min))

    def w_cond(st):
        return jnp.max(jnp.where(pending(st[2]), 1.0, 0.0)) > 0.5

    def w_body(st):
        lo_, tau_, ct_ = st
        need = pending(ct_)
        lo2 = jnp.where(need, tau_, lo_)
        tau2 = next_above(lo2)
        ct2 = count_gt(tau2)
        return lo2, jnp.where(need, tau2, tau_), jnp.where(need, ct2, ct_)

    lo, tau, ct = lax.while_loop(w_cond, w_body, (lo, tau, ct))
    tau = jnp.where(at_min, rmin, tau)
    m_need = jnp.where(at_min, kkf, kkf - ct)

    tie_w = tk if tq >= LANES else LANES
    rk = lax.broadcasted_iota(jnp.int32, (tie_w, tie_w), 0)
    ck = lax.broadcasted_iota(jnp.int32, (tie_w, tie_w), 1)
    before = jnp.where(rk < ck, 1.0, 0.0).astype(BF16)
    ones16 = jnp.ones((LANES, LANES), BF16)

    def p2(c, cum):
        x = sc_ref[c]
        eqs = [chunk(x, j) == tau for j in range(nch)]
        eqf = [jnp.where(e, 1.0, 0.0) for e in eqs]
        if tie_w == tk:
            pre = jnp.dot(jnp.concatenate(eqf, axis=-1).astype(BF16), before, preferred_element_type=F32)
            pres = [chunk(pre, j) + cum for j in range(nch)]
            tot = eqf[0]
            for j in range(1, nch):
                tot = tot + eqf[j]
            cum = cum + bcast(jnp.sum(tot, axis=1, keepdims=True))
        else:
            pres = []
            for j in range(nch):
                e16 = eqf[j].astype(BF16)
                pres.append(jnp.dot(e16, before, preferred_element_type=F32) + cum)
                cum = cum + jnp.dot(e16, ones16, preferred_element_type=F32)
        bias = [jnp.where(chunk(x, j) > tau, 0.0, jnp.where(eqs[j], jnp.where(pres[j] < m_need, 0.0, NEG), NEG))
                for j in range(nch)]
        koff = key_off(c)
        for p in range(N_PAIRS):
            kp = k_ref[0, pl.ds(koff, tk), p * LANES:(p + 1) * LANES]
            vp = v_ref[0, pl.ds(koff, tk), p * LANES:(p + 1) * LANES]
            s = lax.dot_general(qh_ref[p], kp, NT, preferred_element_type=F32).reshape(hpg, tq, tk)
            sm = [chunk(s, j) + bias[j][None] for j in range(nch)]
            mx_ = sm[0]
            for j in range(1, nch):
                mx_ = jnp.maximum(mx_, sm[j])
            m_old = m_ref[p]
            m_new = jnp.maximum(m_old, bcast(jnp.max(mx_, axis=-1, keepdims=True)))
            pe = [jnp.exp2(sm[j] - m_new) for j in range(nch)]
            ps = pe[0]
            for j in range(1, nch):
                ps = ps + pe[j]
            alpha = jnp.exp2(m_old - m_new)
            l_ref[p] = alpha * l_ref[p] + bcast(jnp.sum(ps, axis=-1, keepdims=True))
            pv = jnp.dot(jnp.concatenate(pe, axis=-1).reshape(hpg * tq, tk).astype(BF16), vp,
                         preferred_element_type=F32)
            acc_ref[p] = alpha.reshape(hpg * tq, LANES) * acc_ref[p] + pv
            m_ref[p] = m_new
        return cum

    kv_loop(p2, jnp.zeros((tq, LANES), F32))

    def head_out(h):
        p, hh = h // hpg, h % hpg
        r = acc_ref[p, hh * tq:(hh + 1) * tq] / l_ref[p, hh]
        if (h // GQA) % 2 != h % 2:
            r = pltpu.roll(r, HEAD_DIM, 1)
        return r

    for j in range(N_HEADS // 2):
        o_ref[0, :, j * LANES:(j + 1) * LANES] = jnp.where(lo_half, head_out(2 * j), head_out(2 * j + 1)).astype(o_ref.dtype)


def _attention(q, qi, wi, k_all, v_all, ki2, *, tq, tk, pos0, s_valid, kk):
    b, tq_pad, _ = q.shape
    s_pad = k_all.shape[1]
    nblk = s_pad // tk
    n_static = min(nblk, (pos0 + tq - 1) // tk + 1) if tq_pad == tq else None
    qblk = lambda w: pl.BlockSpec((1, tq, w), lambda bi, qi_: (bi, qi_, 0))
    seq = lambda w: pl.BlockSpec((1, s_pad, w), lambda bi, qi_: (bi, 0, 0))
    return pl.pallas_call(
        functools.partial(_attn_kernel, tq=tq, tk=tk, nblk=nblk, n_static=n_static, pos0=pos0, s_valid=s_valid, kk=kk),
        out_shape=jax.ShapeDtypeStruct((b, tq_pad, D_ATTN), BF16),
        grid=(b, tq_pad // tq),
        in_specs=[qblk(D_ATTN), qblk(IDX_HEADS * IDX_DIM), qblk(LANES), seq(D_KV), seq(D_KV), seq(LANES)],
        out_specs=qblk(D_ATTN),
        scratch_shapes=[pltpu.VMEM((nblk, tq, tk), F32), pltpu.VMEM((N_PAIRS, HEADS_PER_PAIR * tq, LANES), BF16),
                        pltpu.VMEM((IDX_HEADS * tq, LANES), BF16), pltpu.VMEM((IDX_HEADS, tq, LANES), F32),
                        pltpu.VMEM((N_PAIRS, HEADS_PER_PAIR, tq, LANES), F32),
                        pltpu.VMEM((N_PAIRS, HEADS_PER_PAIR, tq, LANES), F32),
                        pltpu.VMEM((N_PAIRS, HEADS_PER_PAIR * tq, LANES), F32)],
        compiler_params=_cparams("parallel", "parallel"),
        name="sparse_attention",
    )(q, qi, wi, k_all, v_all, ki2)


POOL_HALO = 16
POOL_ROWS = 512


def _pool_kernel(u_ref, halo_ref, st_ref, w_ref, scale_ref, o_ref, ext_ref, *, tb, pos0):
    i = pl.program_id(1)
    ext_ref[0:POOL_HALO] = jnp.where(i == 0, st_ref[0], halo_ref[0])
    ext_ref[POOL_HALO:POOL_HALO + tb] = u_ref[0]
    pos = pos0 + i * tb + lax.broadcasted_iota(jnp.int32, (tb, 1), 0)
    for gi, w in enumerate(POOL_WINDOWS):
        sl = slice(gi * POOL_GROUP, (gi + 1) * POOL_GROUP)
        tot = ext_ref[POOL_HALO:POOL_HALO + tb, sl]
        for k in range(1, w):
            tot = tot + ext_ref[POOL_HALO - k:POOL_HALO - k + tb, sl]
        cnt = jnp.minimum(w, pos + 1).astype(F32)
        pooled = tot / cnt - u_ref[0, :, sl]
        mixed = jnp.dot(pooled.astype(BF16), w_ref[gi], preferred_element_type=F32) * scale_ref[:, sl]
        o_ref[0, :, sl] = mixed.astype(o_ref.dtype)


def _pool(u, state16, w, scale, *, pos0):
    b, t, c = u.shape
    tb = POOL_ROWS if t % POOL_ROWS == 0 else SEQ_BLOCK
    hb = tb // POOL_HALO
    return pl.pallas_call(
        functools.partial(_pool_kernel, tb=tb, pos0=pos0),
        name="pool_mix",
        out_shape=jax.ShapeDtypeStruct((b, t, c), BF16),
        grid=(b, t // tb),
        in_specs=[pl.BlockSpec((1, tb, c), lambda bi, i: (bi, i, 0)),
                  pl.BlockSpec((1, POOL_HALO, c), lambda bi, i: (bi, jnp.maximum(i * hb - 1, 0), 0)),
                  pl.BlockSpec((1, POOL_HALO, c), lambda bi, i: (bi, 0, 0)),
                  pl.BlockSpec((len(POOL_WINDOWS), POOL_GROUP, POOL_GROUP), lambda bi, i: (0, 0, 0)),
                  pl.BlockSpec((1, c), lambda bi, i: (0, 0))],
        out_specs=pl.BlockSpec((1, tb, c), lambda bi, i: (bi, i, 0)),
        scratch_shapes=[pltpu.VMEM((POOL_HALO + tb, c), F32)],
        compiler_params=_cparams("parallel", "parallel"),
    )(u, u, state16, w, scale)


CONV_HALO = SUBLANES
SSD_GH = SSD_HEADS // SSD_GROUPS
SSD_GW = SSD_GH * SSD_HEAD_DIM
CONV_CHUNK = 512


def _ssd_kernel(z_ref, dt_ref, xbc_ref, halo_ref, st_ref, h0_ref, cw_ref, cb_ref, dtb_ref, alog_ref, dsk_ref,
                nw_ref, ex_ref, ext_t_ref, y_ref, hT_ref, ext_ref, xs_ref, h_ref, *, q, t_valid):
    c = pl.program_id(1)

    @pl.when(c == 0)
    def _():
        h_ref[...] = h0_ref[0]

    ext_ref[0:CONV_HALO] = jnp.where(c == 0, st_ref[0], halo_ref[0])
    ext_ref[CONV_HALO:CONV_HALO + q] = xbc_ref[0]
    for ch in range(D_SSD_CONV // CONV_CHUNK):
        cc = slice(ch * CONV_CHUNK, (ch + 1) * CONV_CHUNK)
        xc = cb_ref[:, cc]
        for j in range(SSD_CONV):
            off = CONV_HALO - (SSD_CONV - 1) + j
            xc = xc + ext_ref[off:off + q, cc] * cw_ref[j:j + 1, cc]
        xs_ref[:, cc] = _silu(xc)
    x = xs_ref[:, :D_SSD]

    lane = lax.broadcasted_iota(jnp.int32, (q, LANES), 1)
    row = lax.broadcasted_iota(jnp.int32, (q, LANES), 0)
    v = dt_ref[0] + dtb_ref[...]
    dt = jnp.maximum(v, 0.0) + jnp.log1p(jnp.exp(-jnp.abs(v)))
    dt = jnp.where(jnp.logical_and(lane < SSD_HEADS, c * q + row < t_valid), dt, 0.0)
    a = dt * (-jnp.exp(alog_ref[...]))
    tri = jnp.where(lane <= row, 1.0, 0.0).astype(BF16)
    causal = lane <= row
    cs = _dot3_left(tri, a)
    cs_t = cs.T
    ex = ex_ref[...]
    cs_last = cs[q - 1:q, :]
    dt_x = _dot3(dt, ex)
    ecs_x = jnp.exp(_dot3(cs, ex))
    dec_x = jnp.exp(_dot3(cs_last - cs, ex))
    xdt = x * dt_x
    xdec = (xdt * dec_x).astype(BF16)
    xdt16 = xdt.astype(BF16)
    rowdec = jnp.exp(_dot3_left(ext_t_ref[...], jnp.broadcast_to(cs_t[:, q - 1:q], (LANES, LANES))))
    lo_half = lane < SSD_HEAD_DIM

    for g in range(SSD_GROUPS):
        bg = xs_ref[:, D_SSD + g * SSD_STATE:D_SSD + (g + 1) * SSD_STATE].astype(BF16)
        cg = xs_ref[:, D_SSD + (SSD_GROUPS + g) * SSD_STATE:D_SSD + (SSD_GROUPS + g + 1) * SSD_STATE].astype(BF16)
        cb = lax.dot_general(cg, bg, NT, preferred_element_type=F32)
        hs = slice(g * SSD_GW, (g + 1) * SSD_GW)
        hprev = h_ref[hs, :]
        y_off = lax.dot_general(cg, hprev.astype(BF16), NT, preferred_element_type=F32) * ecs_x[:, hs]
        for jp in range(SSD_GH // 2):
            cols = slice(g * SSD_GW + jp * LANES, g * SSD_GW + (jp + 1) * LANES)
            xp = xdt16[:, cols]
            halves = []
            for e in range(2):
                hd = g * SSD_GH + 2 * jp + e
                seg = cs[:, hd:hd + 1] - cs_t[hd:hd + 1, :]
                lmat = jnp.where(causal, jnp.exp(jnp.where(causal, seg, 0.0)), 0.0)
                halves.append(jnp.dot((cb * lmat).astype(BF16), xp, preferred_element_type=F32))
            y_diag = jnp.where(lo_half, halves[0], halves[1])
            yv = y_diag + y_off[:, jp * LANES:(jp + 1) * LANES] + dsk_ref[:, cols] * xs_ref[:, cols]
            zv = z_ref[0, :, cols]
            ext_ref[0:q, cols] = yv * _silu(zv)
        st_new = lax.dot_general(xdec[:, hs], bg, TN, preferred_element_type=F32)
        h_ref[hs, :] = hprev * rowdec[hs, :] + st_new

    for g in range(SSD_GROUPS):
        hs = slice(g * SSD_GW, (g + 1) * SSD_GW)
        yg = ext_ref[0:q, hs]
        yn = yg * lax.rsqrt(jnp.mean(yg * yg, axis=-1, keepdims=True) + EPS) * nw_ref[:, hs]
        y_ref[0, :, hs] = yn.astype(y_ref.dtype)

    @pl.when(c == pl.num_programs(1) - 1)
    def _():
        hT_ref[0] = h_ref[...]


def _dot3_left(w, v):
    h1, h2, h3 = _split3(v)
    return (jnp.dot(w, h1, preferred_element_type=F32) + jnp.dot(w, h2, preferred_element_type=F32)
            + jnp.dot(w, h3, preferred_element_type=F32))


def _ssd(zdt, xbc, state8, h0, h_off, cw, cb, dtb, alog, dsk, nw, ex, ext_t, *, t_valid):
    b, t, _ = xbc.shape
    q = SEQ_BLOCK
    hb = q // CONV_HALO
    cst = lambda r, c_: pl.BlockSpec((r, c_), lambda bi, i: (0, 0))
    return pl.pallas_call(
        functools.partial(_ssd_kernel, q=q, t_valid=t_valid),
        name="ssd_scan",
        out_shape=[jax.ShapeDtypeStruct((b, t, D_SSD), BF16), jax.ShapeDtypeStruct((b, D_SSD, SSD_STATE), F32)],
        grid=(b, t // q),
        in_specs=[pl.BlockSpec((1, q, D_SSD), lambda bi, i: (bi, i, 0)),
                  pl.BlockSpec((1, q, LANES), lambda bi, i: (bi, i, D_SSD // LANES)),
                  pl.BlockSpec((1, q, D_SSD_CONV), lambda bi, i: (bi, i, 0)),
                  pl.BlockSpec((1, CONV_HALO, D_SSD_CONV), lambda bi, i: (bi, jnp.maximum(i * hb - 1, 0), 0)),
                  pl.BlockSpec((1, CONV_HALO, D_SSD_CONV), lambda bi, i: (bi, 0, 0)),
                  pl.BlockSpec((1, D_SSD, SSD_STATE), lambda bi, i: (h_off + bi, 0, 0)),
                  cst(SSD_CONV, D_SSD_CONV), cst(1, D_SSD_CONV), cst(1, LANES), cst(1, LANES), cst(1, D_SSD),
                  cst(1, D_SSD), cst(LANES, D_SSD), cst(D_SSD, LANES)],
        out_specs=[pl.BlockSpec((1, q, D_SSD), lambda bi, i: (bi, i, 0)),
                   pl.BlockSpec((1, D_SSD, SSD_STATE), lambda bi, i: (bi, 0, 0))],
        scratch_shapes=[pltpu.VMEM((CONV_HALO + q, D_SSD_CONV), F32), pltpu.VMEM((q, D_SSD_CONV), F32),
                        pltpu.VMEM((D_SSD, SSD_STATE), F32)],
        compiler_params=_cparams("parallel", "arbitrary"),
    )(zdt, zdt, xbc, xbc, state8, h0, cw, cb, dtb, alog, dsk, nw, ex, ext_t)


def _merge_kernel(x_ref, oa_ref, op_ref, os_ref, gt_ref, g1_ref, wa_ref, wp_ref, ws_ref, wo_ref, o_ref):
    def gate(bi):
        return jax.nn.sigmoid(gt_ref[:, bi * D_MODEL:(bi + 1) * D_MODEL])
    merged = (gate(0) * jnp.dot(oa_ref[...], wa_ref[...], preferred_element_type=F32)
              + gate(1) * jnp.dot(op_ref[...], wp_ref[...], preferred_element_type=F32)
              + gate(2) * jnp.dot(os_ref[...], ws_ref[...], preferred_element_type=F32))
    o_ref[...] = x_ref[...] + g1_ref[0] * jnp.dot(merged.astype(BF16), wo_ref[...], preferred_element_type=F32)


def _merge(x, oa, op, os_, gates, mv_g1, mv_idx, wa, wp, ws, wo, tm):
    n, d = x.shape
    rb = mv_g1.shape[1]
    rowb = lambda w: pl.BlockSpec((tm, w), lambda i: (i, 0))
    cst = lambda r, c: pl.BlockSpec((r, c), lambda i: (0, 0))
    return pl.pallas_call(
        _merge_kernel,
        name="merge_out",
        out_shape=jax.ShapeDtypeStruct((n, d), F32),
        grid=(n // tm,),
        in_specs=[rowb(d), rowb(D_ATTN), rowb(D_POOL), rowb(D_SSD), rowb(N_BRANCH * D_MODEL),
                  pl.BlockSpec((1, rb, d), lambda i: (mv_idx(i), 0, 0)),
                  cst(D_ATTN, d), cst(D_POOL, d), cst(D_SSD, d), cst(d, d)],
        out_specs=rowb(d),
        compiler_params=_cparams("parallel"),
    )(x, oa, op, os_, gates, mv_g1, wa, wp, ws, wo)


FFN_TILE = D_FF // 2
FFN_ROWS = 512
FFN_SLOT = 16


def _ffn_kernel(x_ref, hx_ref, sa_ref, sg_ref, sh_ref, sc_ref, g2_ref, gn_ref, wu_ref, cwa_ref, cwg_ref, cba_ref,
                cbg_ref, wd_ref, o_ref, ta_ref, tg_ref, xin_ref, ea_ref, eg_ref, acc_ref, *, tb, t_valid, tail_row, slot):
    i = pl.program_id(1)
    if slot is None:
        carried_in = [(0, 0)]
        tails = [(tail_row, 0)]
    else:
        carried_in = [(CONV_HALO + k * slot, k) for k in range(tb // slot)]
        tails = [(CONV_HALO + k * slot + t_valid, k) for k in range(tb // slot)]
    xin_ref[0:CONV_HALO] = hx_ref[0]
    xin_ref[CONV_HALO:CONV_HALO + tb] = x_ref[0]
    xin = xin_ref[...]
    y = xin * lax.rsqrt(jnp.mean(xin * xin, axis=-1, keepdims=True) + EPS) * gn_ref[...]

    def rows_of(m):
        return m if m.shape[0] == 1 else jnp.concatenate([m[:CONV_HALO], m], axis=0)

    h = (y * (1.0 + rows_of(sc_ref[0])) + rows_of(sh_ref[0])).astype(BF16)
    for f in range(D_FF // FFN_TILE):
        cols = slice(f * FFN_TILE, (f + 1) * FFN_TILE)

        def conv(e_ref, s_ref, t_ref, cw_ref, cb_ref, col0):
            u = jnp.dot(h, wu_ref[:, col0 + f * FFN_TILE:col0 + (f + 1) * FFN_TILE], preferred_element_type=F32)
            e_ref[...] = u

            def take_carried():
                for row, k in carried_in:
                    e_ref[row:row + CONV_HALO] = s_ref[k, :, cols]

            def emit_tails():
                for row, k in tails:
                    t_ref[k, :, cols] = e_ref[row:row + CONV_HALO]

            if slot is None:
                pl.when(i == 0)(take_carried)
                pl.when(i == pl.num_programs(1) - 1)(emit_tails)
            else:
                take_carried()
                emit_tails()
            out = cb_ref[:, cols]
            for j in range(FFN_CONV):
                off = CONV_HALO - (FFN_CONV - 1) + j
                out = out + e_ref[off:off + tb] * cw_ref[j:j + 1, cols]
            return out

        a = conv(ea_ref, sa_ref, ta_ref, cwa_ref, cba_ref, 0)
        g = conv(eg_ref, sg_ref, tg_ref, cwg_ref, cbg_ref, D_FF)
        part = jnp.dot((_silu(g) * a).astype(BF16), wd_ref[cols, :], preferred_element_type=F32)
        if f == 0:
            acc_ref[...] = part
        else:
            acc_ref[...] += part
    o_ref[0] = x_ref[0] + g2_ref[0] * acc_ref[...]


def _ffn(x, sa, sg, sh2, sc2, g2, gn, wu, cwa, cwg, cba, cbg, wd, *, tb, t_valid, slot=None):
    b, t, d = x.shape
    n = sa.shape[0]
    hb = tb // CONV_HALO
    blk = pl.BlockSpec((1, tb, d), lambda bi, i: (bi, i, 0))
    halo = pl.BlockSpec((1, CONV_HALO, d), lambda bi, i: (bi, jnp.maximum(i * hb - 1, 0), 0))
    if slot is None:
        st = pl.BlockSpec((1, CONV_HALO, D_FF), lambda bi, i: (bi, 0, 0))
        seqv = pl.BlockSpec((1, 1, d), lambda bi, i: (bi, 0, 0))
    else:
        st = pl.BlockSpec((tb // slot, CONV_HALO, D_FF), lambda bi, i: (i, 0, 0))
        seqv = blk
    cst = lambda r, c: pl.BlockSpec((r, c), lambda bi, i: (0, 0))
    return pl.pallas_call(
        functools.partial(_ffn_kernel, tb=tb, t_valid=t_valid, tail_row=t_valid - (t // tb - 1) * tb, slot=slot),
        name="conv_ffn",
        out_shape=[jax.ShapeDtypeStruct((b, t, d), F32), jax.ShapeDtypeStruct((n, CONV_HALO, D_FF), F32),
                   jax.ShapeDtypeStruct((n, CONV_HALO, D_FF), F32)],
        grid=(b, t // tb),
        in_specs=[blk, halo, st, st, seqv, seqv, seqv, cst(1, d), cst(d, 2 * D_FF), cst(FFN_CONV, D_FF),
                  cst(FFN_CONV, D_FF), cst(1, D_FF), cst(1, D_FF), cst(D_FF, d)],
        out_specs=[blk, st, st],
        scratch_shapes=[pltpu.VMEM((CONV_HALO + tb, d), F32), pltpu.VMEM((CONV_HALO + tb, FFN_TILE), F32),
                        pltpu.VMEM((CONV_HALO + tb, FFN_TILE), F32), pltpu.VMEM((tb, d), F32)],
        compiler_params=_cparams("parallel", "arbitrary"),
    )(x, x, sa, sg, sh2, sc2, g2, gn, wu, cwa, cwg, cba, cbg, wd)


def _pad_rows(a, rows, front=False):
    extra = rows - a.shape[1]
    if extra == 0:
        return a
    return jnp.pad(a, ((0, 0), (extra, 0) if front else (0, extra), (0, 0)))


def _rope_tables(pos):
    half = HEAD_DIM // 2
    freqs = ROPE_THETA ** (-jnp.arange(half, dtype=jnp.float32) / half)
    ang = pos.astype(jnp.float32)[:, None] * freqs[None, :]
    cos, sin = jnp.cos(ang), jnp.sin(ang)
    return jnp.concatenate([cos, cos, cos, cos], axis=-1), jnp.concatenate([-sin, sin, -sin, sin], axis=-1)


def _layer_weights(l, w_ada, b_ada, norm1, norm2, w_in, q_norm, k_norm, pool_w, pool_scale, ssd_conv_w, ssd_conv_b,
                   ssd_dt_bias, ssd_a_log, ssd_d, ssd_norm, w_branch_attn, w_branch_pool, w_branch_ssd, w_out,
                   ffn_up, ffn_conv_w, ffn_conv_b, ffn_down):
    o = _OFFS
    win = w_in[l]
    col = lambda a, b: win[:, a:b]
    zpad = lambda n: jnp.zeros((D_MODEL, n), win.dtype)
    row = lambda v: v.reshape(1, -1).astype(F32)
    lanepad = lambda v: jnp.pad(v.astype(F32), (0, LANES - v.shape[0])).reshape(1, LANES)
    return dict(
        w_ada=w_ada[l].astype(BF16), b_ada=row(b_ada[l]), norm1=row(norm1[l]), norm2=row(norm2[l]),
        w_attn=jnp.concatenate([col(o[0], o[6]), zpad(LANES - IDX_DIM - IDX_HEADS)], axis=1).astype(BF16),
        w_pool=col(o[6], o[7]).astype(BF16),
        w_zdt=jnp.concatenate([col(o[7], o[8]), col(o[9], o[10]), zpad(LANES - SSD_HEADS)], axis=1).astype(BF16),
        w_xbc=col(o[8], o[9]).astype(BF16),
        w_gate=col(o[10], o[11]).astype(BF16),
        qn=row(jnp.tile(q_norm[l], LANES // HEAD_DIM)), kn=row(jnp.tile(k_norm[l], LANES // HEAD_DIM)),
        pool_w=pool_w[l].astype(BF16), pool_scale=row(pool_scale[l]),
        cw=ssd_conv_w[l].astype(F32), cb=row(ssd_conv_b[l]), dtb=lanepad(ssd_dt_bias[l]), alog=lanepad(ssd_a_log[l]),
        dsk=row(jnp.repeat(ssd_d[l], SSD_HEAD_DIM)), nw=row(ssd_norm[l]),
        wa=w_branch_attn[l].astype(BF16), wp=w_branch_pool[l].astype(BF16), ws=w_branch_ssd[l].astype(BF16),
        wo=w_out[l].astype(BF16),
        w_up=ffn_up[l].astype(BF16),
        cwa=ffn_conv_w[l][:, :D_FF].astype(F32), cwg=ffn_conv_w[l][:, D_FF:].astype(F32),
        cba=row(ffn_conv_b[l][:D_FF]), cbg=row(ffn_conv_b[l][D_FF:]), wd=ffn_down[l].astype(BF16))


def _group_layer(x, mod, lw, consts, *, pos0, past, states):
    b, t, d = x.shape
    n = b * t
    tp = -(-t // SEQ_BLOCK) * SEQ_BLOCK
    per_seq = t % SEQ_BLOCK == 0
    tm = min(ROW_TILE, t) if per_seq else n
    sh1, sc1, g1, sh2, sc2, g2 = [mod[:, i * d:(i + 1) * d] for i in range(6)]
    pos = pos0 + jnp.arange(t)
    cos_t, sin_t = _rope_tables(pos)
    if per_seq:
        view = lambda m: m.reshape(b, 1, d)
        mv_idx = lambda i: i // (t // tm)
        tab = lambda a: a.reshape(t // tm, tm, LANES)
        tab_idx = lambda i: i % (t // tm)
    else:
        view = lambda m: jnp.repeat(m, t, axis=0).reshape(1, n, d)
        mv_idx = lambda i: 0
        tab = lambda a: jnp.tile(a, (b, 1)).reshape(1, n, LANES)
        tab_idx = lambda i: 0
    xf = x.reshape(n, d)
    nm = lambda w: _norm_matmul(xf, view(sh1), view(sc1), mv_idx, lw['norm1'], w, tm)
    pa, pool_u, zdt, xbc, gates = nm(lw['w_attn']), nm(lw['w_pool']), nm(lw['w_zdt']), nm(lw['w_xbc']), nm(lw['w_gate'])

    q_r, k_r, qi_r, ki2, wi_s = _attn_prep(pa, tab(cos_t), tab(sin_t), tab_idx, lw['qn'], lw['kn'], consts['seg'], tm)
    v_new = pa[:, D_ATTN + D_KV:D_ATTN + 2 * D_KV]
    seq3 = lambda a: a.reshape(b, t, a.shape[-1])
    if past is None:
        tq, tk = ATTN_Q_ROWS, min(ATTN_KEYS, t)
        k_all, v_all, ki_all = seq3(k_r).astype(BF16), seq3(v_new).astype(BF16), seq3(ki2).astype(BF16)
        s_valid = t
        qpad = lambda a: seq3(a)
    else:
        tq = 2 * SUBLANES
        page_table, pool_off, ck, cv, ci = past
        k_all, v_all, ki_all = _page_gather(page_table, pool_off, ck, cv, ci, seq3(k_r), seq3(v_new), seq3(ki2))
        tk = min(ATTN_KEYS_DECODE, k_all.shape[1] // 2)
        s_valid = pos0 + t
        qpad = lambda a: _pad_rows(seq3(a), tq)
    assert k_all.shape[1] % tk == 0
    kk = min(INDEX_TOPK, s_valid // 4)
    o_attn = _attention(qpad(q_r), qpad(qi_r), qpad(wi_s), k_all, v_all, ki_all,
                        tq=tq, tk=tk, pos0=pos0, s_valid=s_valid, kk=kk)[:, :t].reshape(n, D_ATTN)

    st_pool, st_sconv, (st_ssd, ssd_off), st_fconv = states
    seqp = lambda a: _pad_rows(seq3(a), tp)
    o_pool = _pool(seqp(pool_u), _pad_rows(st_pool, POOL_HALO, front=True), lw['pool_w'], lw['pool_scale'], pos0=pos0)
    o_ssd, ssd_state = _ssd(seqp(zdt), seqp(xbc), _pad_rows(st_sconv, CONV_HALO, front=True),
                            st_ssd, ssd_off, lw['cw'], lw['cb'], lw['dtb'], lw['alog'], lw['dsk'],
                            lw['nw'], consts['ex'], consts['ext_t'], t_valid=t)
    unpad = lambda a: a[:, :t].reshape(n, a.shape[-1])
    x1 = _merge(xf, o_attn, unpad(o_pool), unpad(o_ssd), gates, view(g1), mv_idx,
                lw['wa'], lw['wp'], lw['ws'], lw['wo'], tm)

    st_f = _pad_rows(st_fconv, CONV_HALO, front=True)
    ffn_w = (lw['norm2'], lw['w_up'], lw['cwa'], lw['cwg'], lw['cba'], lw['cbg'], lw['wd'])
    if per_seq or t > FFN_SLOT - CONV_HALO:
        per_b = lambda m: m.reshape(b, 1, d)
        x2, tail_a, tail_g = _ffn(seqp(x1), st_f[:, :, :D_FF], st_f[:, :, D_FF:], per_b(sh2), per_b(sc2), per_b(g2),
                                  *ffn_w, tb=min(FFN_ROWS, tp), t_valid=t)
        x2 = x2[:, :t]
    else:
        in_slot = lambda a: jnp.pad(a, ((0, 0), (CONV_HALO, FFN_SLOT - CONV_HALO - t), (0, 0))).reshape(1, b * FFN_SLOT, d)
        per_row = lambda m: jnp.repeat(m, FFN_SLOT, axis=0).reshape(1, b * FFN_SLOT, d)
        slots_per_step = max(k for k in range(1, FFN_ROWS // FFN_SLOT + 1) if b % k == 0)
        x2, tail_a, tail_g = _ffn(in_slot(seq3(x1)), st_f[:, :, :D_FF], st_f[:, :, D_FF:], per_row(sh2), per_row(sc2),
                                  per_row(g2), *ffn_w, tb=slots_per_step * FFN_SLOT, t_valid=t, slot=FFN_SLOT)
        x2 = x2.reshape(b, FFN_SLOT, d)[:, CONV_HALO:CONV_HALO + t]
    fconv_state = jnp.concatenate([tail_a, tail_g], axis=-1)[:, -(FFN_CONV - 1):]

    def tail(prev, news, keep):
        last = jnp.concatenate([seq3(a)[:, -min(keep, t):] for a in news], axis=-1)
        return jnp.concatenate([prev, last], axis=1)[:, -keep:]

    new_states = (seq3(k_r).reshape(b, t, N_KV_HEADS, HEAD_DIM), seq3(v_new).reshape(b, t, N_KV_HEADS, HEAD_DIM),
                  seq3(ki2)[:, :, :IDX_DIM], tail(st_pool, [pool_u], POOL_CTX), tail(st_sconv, [xbc], SSD_CONV - 1),
                  ssd_state.reshape(b, SSD_HEADS, SSD_HEAD_DIM, SSD_STATE), fconv_state)
    return x2, new_states


def kernel(x_prompt, x_sample, c_prompt, c_sample, cache_k, cache_v, cache_kidx, page_table, state_pool, state_ssd_conv, state_ssd, state_ffn_conv, w_ada, b_ada, norm1, norm2, w_in, q_norm, k_norm, pool_w, pool_scale, ssd_conv_w, ssd_conv_b, ssd_dt_bias, ssd_a_log, ssd_d, ssd_norm, w_branch_attn, w_branch_pool, w_branch_ssd, w_out, ffn_up, ffn_conv_w, ffn_conv_b, ffn_down):
    bp, tp, d = x_prompt.shape
    bs, ts, _ = x_sample.shape
    depth = w_ada.shape[0]
    n_pool, page = cache_k.shape[1], cache_k.shape[2]
    past_len = page_table.shape[1] * page
    dt = x_prompt.dtype

    r = lax.broadcasted_iota(jnp.int32, (LANES, LANES), 0)
    c = lax.broadcasted_iota(jnp.int32, (LANES, LANES), 1)
    hr = lax.broadcasted_iota(jnp.int32, (LANES, D_SSD), 0)
    hc = lax.broadcasted_iota(jnp.int32, (LANES, D_SSD), 1)
    ex = (hr == hc // SSD_HEAD_DIM).astype(BF16)
    consts = dict(seg=(r // HEAD_DIM == c // HEAD_DIM).astype(BF16), ex=ex, ext_t=ex.T)

    c_all = jnp.concatenate([c_prompt, c_sample], axis=0)
    rows = -(-c_all.shape[0] // SUBLANES) * SUBLANES
    c_pad = jnp.pad(c_all, ((0, rows - c_all.shape[0]), (0, 0)))

    ck_t = jnp.transpose(cache_k, (0, 1, 3, 4, 2)).reshape(depth * n_pool, D_KV, page)
    cv_t = jnp.transpose(cache_v, (0, 1, 3, 4, 2)).reshape(depth * n_pool, D_KV, page)
    ci_t = jnp.transpose(cache_kidx, (0, 1, 3, 2)).reshape(depth * n_pool, IDX_DIM, page)

    ssd_all = state_ssd.reshape(depth * bs, D_SSD, SSD_STATE)
    zeros = lambda *s: jnp.zeros(s, dt)
    xp, xs = x_prompt, x_sample
    new_p, new_s = [[] for _ in range(7)], [[] for _ in range(7)]
    for l in range(depth):
        lw = _layer_weights(l, w_ada, b_ada, norm1, norm2, w_in, q_norm, k_norm, pool_w, pool_scale, ssd_conv_w,
                            ssd_conv_b, ssd_dt_bias, ssd_a_log, ssd_d, ssd_norm, w_branch_attn, w_branch_pool,
                            w_branch_ssd, w_out, ffn_up, ffn_conv_w, ffn_conv_b, ffn_down)
        mod = _mod(c_pad, lw['w_ada'], lw['b_ada'])
        xp, st_p = _group_layer(
            xp, mod[:bp], lw, consts, pos0=0, past=None,
            states=(zeros(bp, POOL_CTX, D_POOL), zeros(bp, SSD_CONV - 1, D_SSD_CONV),
                    (zeros(bp, D_SSD, SSD_STATE), 0), zeros(bp, FFN_CONV - 1, 2 * D_FF)))
        past = (page_table, l * n_pool, ck_t, cv_t, ci_t)
        xs, st_s = _group_layer(xs, mod[bp:bp + bs], lw, consts, pos0=past_len, past=past,
                                states=(state_pool[l], state_ssd_conv[l], (ssd_all, l * bs), state_ffn_conv[l]))
        for j in range(7):
            new_p[j].append(st_p[j])
            new_s[j].append(st_s[j])
    outs_p = [jnp.stack(s) for s in new_p]
    outs_s = [jnp.stack(s) for s in new_s]
    return (xp, xs, *outs_p, *outs_s)
```

```python
import functools

import numpy as np
import jax
import jax.numpy as jnp
from jax import lax
from jax.experimental import pallas as pl
from jax.experimental.pallas import tpu as pltpu

D_MODEL = 1024
N_HEADS = 16
N_KV_HEADS = 4
HEAD_DIM = 64
D_ATTN = N_HEADS * HEAD_DIM
D_KV = N_KV_HEADS * HEAD_DIM
IDX_HEADS = 8
IDX_DIM = 64
INDEX_TOPK = 256
ROPE_THETA = 10000.0
POOL_WINDOWS = (2, 4, 8, 16)
D_POOL = D_MODEL
POOL_GROUP = D_POOL // len(POOL_WINDOWS)
POOL_CTX = max(POOL_WINDOWS) - 1
D_SSD = 2 * D_MODEL
SSD_HEAD_DIM = 64
SSD_HEADS = D_SSD // SSD_HEAD_DIM
SSD_GROUPS = 4
SSD_STATE = 128
SSD_CONV = 4
D_SSD_CONV = D_SSD + 2 * SSD_GROUPS * SSD_STATE
D_FF = 2816
FFN_CONV = 3
N_BRANCH = 3
EPS = 1e-6
_SPLITS = (D_ATTN, D_KV, D_KV, IDX_HEADS * IDX_DIM, IDX_DIM, IDX_HEADS, D_POOL, D_SSD, D_SSD_CONV,
           SSD_HEADS, N_BRANCH * D_MODEL)
_OFFS = tuple(int(v) for v in np.cumsum((0,) + _SPLITS))

F32 = jnp.float32
BF16 = jnp.bfloat16
LANES = 128
SUBLANES = 8
SEQ_BLOCK = 128
ROW_TILE = 512
ATTN_Q_ROWS = 256
ATTN_KEYS = 256
ATTN_KEYS_DECODE = 1024
VMEM_LIMIT = 56 * 1024 * 1024
NEG = -1e30
LOG2E = 1.4426950408889634
N_BISECT = 20
NT = (((1,), (1,)), ((), ()))
TN = (((0,), (0,)), ((), ()))


def _cparams(*sem):
    return pltpu.CompilerParams(dimension_semantics=sem, vmem_limit_bytes=VMEM_LIMIT)


def _silu(v):
    return (0.5 * v) * (1.0 + jnp.tanh(0.5 * v))


def _split3(v):
    h1 = v.astype(BF16)
    r1 = v - h1.astype(F32)
    h2 = r1.astype(BF16)
    h3 = (r1 - h2.astype(F32)).astype(BF16)
    return h1, h2, h3


def _dot3(v, w):
    h1, h2, h3 = _split3(v)
    return (jnp.dot(h1, w, preferred_element_type=F32) + jnp.dot(h2, w, preferred_element_type=F32)
            + jnp.dot(h3, w, preferred_element_type=F32))


def _mod_kernel(c_ref, w_ref, b_ref, o_ref):
    s = _silu(c_ref[...])
    o_ref[...] = jnp.dot(s.astype(BF16), w_ref[...], preferred_element_type=F32) + b_ref[...]


def _mod(c_pad, w, b):
    rows, d = c_pad.shape
    n = w.shape[1]
    tn = n // 4
    return pl.pallas_call(
        _mod_kernel,
        name="adaln_mod",
        out_shape=jax.ShapeDtypeStruct((rows, n), F32),
        grid=(n // tn,),
        in_specs=[pl.BlockSpec((rows, d), lambda j: (0, 0)),
                  pl.BlockSpec((d, tn), lambda j: (0, j)),
                  pl.BlockSpec((1, tn), lambda j: (0, j))],
        out_specs=pl.BlockSpec((rows, tn), lambda j: (0, j)),
        compiler_params=_cparams("parallel"),
    )(c_pad, w, b)


def _norm_matmul_kernel(x_ref, sh_ref, sc_ref, g_ref, w_ref, o_ref):
    x = x_ref[...]
    y = x * lax.rsqrt(jnp.mean(x * x, axis=-1, keepdims=True) + EPS) * g_ref[...]
    h = y * (1.0 + sc_ref[0]) + sh_ref[0]
    o_ref[...] = jnp.dot(h.astype(BF16), w_ref[...], preferred_element_type=F32).astype(o_ref.dtype)


def _norm_matmul(x, mv_sh, mv_sc, mv_idx, g, w, tm):
    n, d = x.shape
    nout = w.shape[1]
    rb = mv_sh.shape[1]
    return pl.pallas_call(
        _norm_matmul_kernel,
        name="norm_matmul",
        out_shape=jax.ShapeDtypeStruct((n, nout), F32),
        grid=(n // tm,),
        in_specs=[pl.BlockSpec((tm, d), lambda i: (i, 0)),
                  pl.BlockSpec((1, rb, d), lambda i: (mv_idx(i), 0, 0)),
                  pl.BlockSpec((1, rb, d), lambda i: (mv_idx(i), 0, 0)),
                  pl.BlockSpec((1, d), lambda i: (0, 0)),
                  pl.BlockSpec((d, nout), lambda i: (0, 0))],
        out_specs=pl.BlockSpec((tm, nout), lambda i: (i, 0)),
        compiler_params=_cparams("parallel"),
    )(x, mv_sh, mv_sc, g, w)


def _attn_prep_kernel(q_ref, k_ref, qi_ref, kw_ref, cos_ref, sin_ref, qn_ref, kn_ref, seg_ref,
                      qo_ref, ko_ref, qio_ref, kio_ref, wio_ref):
    tm = q_ref.shape[0]
    lane = lax.broadcasted_iota(jnp.int32, (tm, LANES), 1)
    first_half = (lane % HEAD_DIM) < (HEAD_DIM // 2)
    cos = cos_ref[0]
    sin = sin_ref[0]
    seg = seg_ref[...]

    def rope(v):
        up = pltpu.roll(v, LANES - HEAD_DIM // 2, 1)
        dn = pltpu.roll(v, HEAD_DIM // 2, 1)
        return v * cos + jnp.where(first_half, up, dn) * sin

    def headnorm(v, gain):
        sq = v * v
        hi = sq.astype(BF16)
        lo = (sq - hi.astype(F32)).astype(BF16)
        ss = jnp.dot(hi, seg, preferred_element_type=F32) + jnp.dot(lo, seg, preferred_element_type=F32)
        return v * lax.rsqrt(ss * (1.0 / HEAD_DIM) + EPS) * gain

    qn = qn_ref[...]
    kn = kn_ref[...]
    for j in range(D_ATTN // LANES):
        sl = slice(j * LANES, (j + 1) * LANES)
        qo_ref[:, sl] = rope(headnorm(q_ref[:, sl], qn)) * (HEAD_DIM ** -0.5 * LOG2E)
    for j in range(D_KV // LANES):
        sl = slice(j * LANES, (j + 1) * LANES)
        ko_ref[:, sl] = rope(headnorm(k_ref[:, sl], kn))
    for j in range(IDX_HEADS * IDX_DIM // LANES):
        sl = slice(j * LANES, (j + 1) * LANES)
        qio_ref[:, sl] = rope(qi_ref[:, sl])
    kw = kw_ref[...]
    kr = rope(kw)
    kio_ref[...] = jnp.where(lane < IDX_DIM, kr, pltpu.roll(kr, IDX_DIM, 1))
    wio_ref[...] = pltpu.roll(kw, LANES - IDX_DIM, 1) * (IDX_HEADS ** -0.5 * IDX_DIM ** -0.5)


def _attn_prep(pa, cos_t, sin_t, tab_idx, qn, kn, seg, tm):
    n = pa.shape[0]
    blk = lambda w, c: pl.BlockSpec((tm, w), lambda i: (i, c))
    tab = pl.BlockSpec((1, tm, LANES), lambda i: (tab_idx(i), 0, 0))
    cst = lambda r, c: pl.BlockSpec((r, c), lambda i: (0, 0))
    return pl.pallas_call(
        _attn_prep_kernel,
        name="attn_prep",
        out_shape=[jax.ShapeDtypeStruct((n, D_ATTN), F32), jax.ShapeDtypeStruct((n, D_KV), F32),
                   jax.ShapeDtypeStruct((n, IDX_HEADS * IDX_DIM), F32),
                   jax.ShapeDtypeStruct((n, LANES), F32), jax.ShapeDtypeStruct((n, LANES), F32)],
        grid=(n // tm,),
        in_specs=[blk(D_ATTN, 0), blk(D_KV, D_ATTN // D_KV), blk(IDX_HEADS * IDX_DIM, 3), blk(LANES, 16),
                  tab, tab, cst(1, LANES), cst(1, LANES), cst(LANES, LANES)],
        out_specs=[blk(D_ATTN, 0), blk(D_KV, 0), blk(IDX_HEADS * IDX_DIM, 0), blk(LANES, 0), blk(LANES, 0)],
        compiler_params=_cparams("parallel"),
    )(pa, pa, pa, pa, cos_t, sin_t, qn, kn, seg)


GATHER_PAGES = 16


def _page_gather_kernel(pt_ref, *refs, n_steps, pg, page):
    ins, outs = refs[:3 * pg + 3], refs[3 * pg + 3:]
    past = pl.program_id(1) < n_steps
    for t in range(3):
        new_ref, out_ref = ins[3 * pg + t], outs[t]
        for i in range(pg):
            rows = slice(i * page, (i + 1) * page)
            tile = ins[t * pg + i][0]
            if t == 2:
                tile = jnp.concatenate([tile, tile], axis=0)
            out_ref[0, rows, :] = jnp.where(past, tile.T, new_ref[0, rows, :]).astype(out_ref.dtype)


def _page_gather(page_table, pool_off, ck, cv, ci, nk, nv, ni):
    bd, n_pages = page_table.shape
    page = ck.shape[2]
    pg = max(g for g in range(1, GATHER_PAGES + 1) if n_pages % g == 0)
    n_steps = n_pages // pg
    nk, nv, ni = [_pad_rows(a, pg * page) for a in (nk, nv, ni)]

    def cache(w, i):
        return pl.BlockSpec((1, w, page),
                            lambda b, p, pt: (pool_off + pt[b, jnp.minimum(p, n_steps - 1) * pg + i], 0, 0))

    new = lambda w: pl.BlockSpec((1, pg * page, w), lambda b, p, pt: (b, 0, 0))
    out = lambda w: pl.BlockSpec((1, pg * page, w), lambda b, p, pt: (b, p, 0))
    s_pad = (n_steps + 1) * pg * page
    in_specs = ([cache(D_KV, i) for i in range(pg)] + [cache(D_KV, i) for i in range(pg)]
                + [cache(IDX_DIM, i) for i in range(pg)] + [new(D_KV), new(D_KV), new(2 * IDX_DIM)])
    return pl.pallas_call(
        functools.partial(_page_gather_kernel, n_steps=n_steps, pg=pg, page=page),
        out_shape=[jax.ShapeDtypeStruct((bd, s_pad, D_KV), BF16), jax.ShapeDtypeStruct((bd, s_pad, D_KV), BF16),
                   jax.ShapeDtypeStruct((bd, s_pad, 2 * IDX_DIM), BF16)],
        grid_spec=pltpu.PrefetchScalarGridSpec(
            num_scalar_prefetch=1, grid=(bd, n_steps + 1), in_specs=in_specs,
            out_specs=[out(D_KV), out(D_KV), out(2 * IDX_DIM)]),
        compiler_params=_cparams("parallel", "parallel"),
        name="page_gather",
    )(page_table, *([ck] * pg), *([cv] * pg), *([ci] * pg), nk, nv, ni)


GQA = N_HEADS // N_KV_HEADS
N_PAIRS = D_KV // LANES
HEADS_PER_PAIR = N_HEADS // N_PAIRS


def _attn_kernel(q_ref, qi_ref, wi_ref, k_ref, v_ref, ki_ref, o_ref,
                 sc_ref, qh_ref, qim_ref, wb_ref, m_ref, l_ref, acc_ref, *, tq, tk, nblk, n_static, pos0, s_valid, kk):
    q0 = pl.program_id(1) * tq
    nch = tk // LANES
    hpg = HEADS_PER_PAIR
    lane = lax.broadcasted_iota(jnp.int32, (tq, LANES), 1)
    row = lax.broadcasted_iota(jnp.int32, (tq, LANES), 0)
    qpos = pos0 + q0 + row
    lo_half = lane < HEAD_DIM
    n_kv = jnp.minimum(nblk, (pos0 + q0 + tq - 1) // tk + 1)
    kkf = float(kk)
    chunk = lambda a, j: a[..., j * LANES:(j + 1) * LANES]
    rg = min(tq, LANES)

    def kv_loop(body, init):
        if n_static is None:
            return lax.fori_loop(0, n_kv, body, init)
        for c in range(n_static):
            init = body(c, init)
        return init

    def key_off(c):
        return c * tk if n_static is not None else pl.multiple_of(c * tk, tk)

    def bcast(col):
        return jnp.broadcast_to(col, col.shape[:-1] + (LANES,))

    wi = wi_ref[0]
    for j in range(IDX_HEADS // 2):
        t = qi_ref[0, :, j * LANES:(j + 1) * LANES]
        qim_ref[2 * j * tq:(2 * j + 1) * tq] = jnp.where(lo_half, t, 0.0).astype(BF16)
        qim_ref[(2 * j + 1) * tq:(2 * j + 2) * tq] = jnp.where(lo_half, 0.0, t).astype(BF16)
    for h in range(IDX_HEADS):
        wb_ref[h] = bcast(wi[:, h:h + 1])
    for h in range(N_HEADS):
        t = q_ref[0, :, (h // 2) * LANES:(h // 2 + 1) * LANES]
        kv_half = (h // GQA) % 2
        if h % 2 != kv_half:
            t = pltpu.roll(t, HEAD_DIM, 1)
        keep = lo_half if kv_half == 0 else jnp.logical_not(lo_half)
        hh = h % hpg
        qh_ref[h // hpg, hh * tq:(hh + 1) * tq] = jnp.where(keep, t, 0.0).astype(BF16)
    m_ref[...] = jnp.full(m_ref.shape, NEG, F32)
    l_ref[...] = jnp.zeros(l_ref.shape, F32)
    acc_ref[...] = jnp.zeros(acc_ref.shape, F32)

    def p1(c, carry):
        mn, mx = carry
        kid = ki_ref[0, pl.ds(key_off(c), tk), :]
        y = lax.dot_general(qim_ref[...], kid, NT, preferred_element_type=F32)
        y = jnp.maximum(y, 0.0).reshape(IDX_HEADS, tq, tk)
        for j in range(nch):
            acc = wb_ref[0] * chunk(y[0], j)
            for h in range(1, IDX_HEADS):
                acc = acc + wb_ref[h] * chunk(y[h], j)
            kpos = c * tk + j * LANES + lane
            vis = jnp.logical_and(kpos <= qpos, kpos < s_valid)
            sc_ref[c, :, j * LANES:(j + 1) * LANES] = jnp.where(vis, acc, -jnp.inf)
            mn = jnp.minimum(mn, jnp.where(vis, acc, jnp.inf))
            mx = jnp.maximum(mx, jnp.where(vis, acc, -jnp.inf))
        return mn, mx

    mn, mx = kv_loop(p1, (jnp.full((tq, LANES), jnp.inf, F32), jnp.full((tq, LANES), -jnp.inf, F32)))
    rmin = bcast(jnp.min(mn, axis=1, keepdims=True))
    rmax = bcast(jnp.max(mx, axis=1, keepdims=True))

    def row_pass(step, init, reduce):
        accs = []
        for r in range(tq // rg):
            rows = slice(r * rg, (r + 1) * rg)

            def body(c, acc, rows=rows):
                x = sc_ref[c, rows, :]
                for j in range(nch):
                    acc = step(acc, chunk(x, j), rows)
                return acc
            accs.append(kv_loop(body, jnp.full((rg, LANES), init, F32)))
        outs = [bcast(reduce(a, axis=1, keepdims=True)) for a in accs]
        return outs[0] if len(outs) == 1 else jnp.concatenate(outs, axis=0)

    def count_gt(t_b):
        return row_pass(lambda acc, xj, rows: acc + jnp.where(xj > t_b[rows], 1.0, 0.0), 0.0, jnp.sum)

    def next_above(t_b):
        return row_pass(lambda acc, xj, rows: jnp.minimum(acc, jnp.where(xj > t_b[rows], xj, jnp.inf)), jnp.inf, jnp.min)

    at_min = jnp.minimum(qpos + 1, s_valid).astype(F32) < kkf
    below = rmin - jnp.maximum((rmax - rmin) * 2.0 ** -8, jnp.abs(rmin) * 2.0 ** -20 + 1e-30)

    def bisect(i, carry):
        lo, hi = carry
        mid = lo + 0.5 * (hi - lo)
        up = count_gt(mid) >= kkf
        return jnp.where(up, mid, lo), jnp.where(up, hi, mid)

    lo, hi = lax.fori_loop(0, N_BISECT, bisect, (below, rmax))
    tau = next_above(lo)
    ct = count_gt(tau)

    def pending(ct_):
        return jnp.logical_and(ct_ >= kkf, jnp.logical_not(at_min))

    def w_cond(st):
        return jnp.max(jnp.where(pending(st[2]), 1.0, 0.0)) > 0.5

    def w_body(st):
        lo_, tau_, ct_ = st
        need = pending(ct_)
        lo2 = jnp.where(need, tau_, lo_)
        tau2 = next_above(lo2)
        ct2 = count_gt(tau2)
        return lo2, jnp.where(need, tau2, tau_), jnp.where(need, ct2, ct_)

    lo, tau, ct = lax.while_loop(w_cond, w_body, (lo, tau, ct))
    tau = jnp.where(at_min, rmin, tau)
    m_need = jnp.where(at_min, kkf, kkf - ct)

    tie_w = tk if tq >= LANES else LANES
    rk = lax.broadcasted_iota(jnp.int32, (tie_w, tie_w), 0)
    ck = lax.broadcasted_iota(jnp.int32, (tie_w, tie_w), 1)
    before = jnp.where(rk < ck, 1.0, 0.0).astype(BF16)
    ones16 = jnp.ones((LANES, LANES), BF16)

    def p2(c, cum):
        x = sc_ref[c]
        eqs = [chunk(x, j) == tau for j in range(nch)]
        eqf = [jnp.where(e, 1.0, 0.0) for e in eqs]
        if tie_w == tk:
            pre = jnp.dot(jnp.concatenate(eqf, axis=-1).astype(BF16), before, preferred_element_type=F32)
            pres = [chunk(pre, j) + cum for j in range(nch)]
            tot = eqf[0]
            for j in range(1, nch):
                tot = tot + eqf[j]
            cum = cum + bcast(jnp.sum(tot, axis=1, keepdims=True))
        else:
            pres = []
            for j in range(nch):
                e16 = eqf[j].astype(BF16)
                pres.append(jnp.dot(e16, before, preferred_element_type=F32) + cum)
                cum = cum + jnp.dot(e16, ones16, preferred_element_type=F32)
        bias = [jnp.where(chunk(x, j) > tau, 0.0, jnp.where(eqs[j], jnp.where(pres[j] < m_need, 0.0, NEG), NEG))
                for j in range(nch)]
        koff = key_off(c)
        for p in range(N_PAIRS):
            kp = k_ref[0, pl.ds(koff, tk), p * LANES:(p + 1) * LANES]
            vp = v_ref[0, pl.ds(koff, tk), p * LANES:(p + 1) * LANES]
            s = lax.dot_general(qh_ref[p], kp, NT, preferred_element_type=F32).reshape(hpg, tq, tk)
            sm = [chunk(s, j) + bias[j][None] for j in range(nch)]
            mx_ = sm[0]
            for j in range(1, nch):
                mx_ = jnp.maximum(mx_, sm[j])
            m_old = m_ref[p]
            m_new = jnp.maximum(m_old, bcast(jnp.max(mx_, axis=-1, keepdims=True)))
            pe = [jnp.exp2(sm[j] - m_new) for j in range(nch)]
            ps = pe[0]
            for j in range(1, nch):
                ps = ps + pe[j]
            alpha = jnp.exp2(m_old - m_new)
            l_ref[p] = alpha * l_ref[p] + bcast(jnp.sum(ps, axis=-1, keepdims=True))
            pv = jnp.dot(jnp.concatenate(pe, axis=-1).reshape(hpg * tq, tk).astype(BF16), vp,
                         preferred_element_type=F32)
            acc_ref[p] = alpha.reshape(hpg * tq, LANES) * acc_ref[p] + pv
            m_ref[p] = m_new
        return cum

    kv_loop(p2, jnp.zeros((tq, LANES), F32))

    def head_out(h):
        p, hh = h // hpg, h % hpg
        r = acc_ref[p, hh * tq:(hh + 1) * tq] / l_ref[p, hh]
        if (h // GQA) % 2 != h % 2:
            r = pltpu.roll(r, HEAD_DIM, 1)
        return r

    for j in range(N_HEADS // 2):
        o_ref[0, :, j * LANES:(j + 1) * LANES] = jnp.where(lo_half, head_out(2 * j), head_out(2 * j + 1)).astype(o_ref.dtype)


def _attention(q, qi, wi, k_all, v_all, ki2, *, tq, tk, pos0, s_valid, kk):
    b, tq_pad, _ = q.shape
    s_pad = k_all.shape[1]
    nblk = s_pad // tk
    n_static = min(nblk, (pos0 + tq - 1) // tk + 1) if tq_pad == tq else None
    qblk = lambda w: pl.BlockSpec((1, tq, w), lambda bi, qi_: (bi, qi_, 0))
    seq = lambda w: pl.BlockSpec((1, s_pad, w), lambda bi, qi_: (bi, 0, 0))
    return pl.pallas_call(
        functools.partial(_attn_kernel, tq=tq, tk=tk, nblk=nblk, n_static=n_static, pos0=pos0, s_valid=s_valid, kk=kk),
        out_shape=jax.ShapeDtypeStruct((b, tq_pad, D_ATTN), BF16),
        grid=(b, tq_pad // tq),
        in_specs=[qblk(D_ATTN), qblk(IDX_HEADS * IDX_DIM), qblk(LANES), seq(D_KV), seq(D_KV), seq(LANES)],
        out_specs=qblk(D_ATTN),
        scratch_shapes=[pltpu.VMEM((nblk, tq, tk), F32), pltpu.VMEM((N_PAIRS, HEADS_PER_PAIR * tq, LANES), BF16),
                        pltpu.VMEM((IDX_HEADS * tq, LANES), BF16), pltpu.VMEM((IDX_HEADS, tq, LANES), F32),
                        pltpu.VMEM((N_PAIRS, HEADS_PER_PAIR, tq, LANES), F32),
                        pltpu.VMEM((N_PAIRS, HEADS_PER_PAIR, tq, LANES), F32),
                        pltpu.VMEM((N_PAIRS, HEADS_PER_PAIR * tq, LANES), F32)],
        compiler_params=_cparams("parallel", "parallel"),
        name="sparse_attention",
    )(q, qi, wi, k_all, v_all, ki2)


POOL_HALO = 16
POOL_ROWS = 512


def _pool_kernel(u_ref, halo_ref, st_ref, w_ref, scale_ref, o_ref, ext_ref, *, tb, pos0):
    i = pl.program_id(1)
    ext_ref[0:POOL_HALO] = jnp.where(i == 0, st_ref[0], halo_ref[0])
    ext_ref[POOL_HALO:POOL_HALO + tb] = u_ref[0]
    pos = pos0 + i * tb + lax.broadcasted_iota(jnp.int32, (tb, 1), 0)
    for gi, w in enumerate(POOL_WINDOWS):
        sl = slice(gi * POOL_GROUP, (gi + 1) * POOL_GROUP)
        tot = ext_ref[POOL_HALO:POOL_HALO + tb, sl]
        for k in range(1, w):
            tot = tot + ext_ref[POOL_HALO - k:POOL_HALO - k + tb, sl]
        cnt = jnp.minimum(w, pos + 1).astype(F32)
        pooled = tot / cnt - u_ref[0, :, sl]
        mixed = jnp.dot(pooled.astype(BF16), w_ref[gi], preferred_element_type=F32) * scale_ref[:, sl]
        o_ref[0, :, sl] = mixed.astype(o_ref.dtype)


def _pool(u, state16, w, scale, *, pos0):
    b, t, c = u.shape
    tb = POOL_ROWS if t % POOL_ROWS == 0 else SEQ_BLOCK
    hb = tb // POOL_HALO
    return pl.pallas_call(
        functools.partial(_pool_kernel, tb=tb, pos0=pos0),
        name="pool_mix",
        out_shape=jax.ShapeDtypeStruct((b, t, c), BF16),
        grid=(b, t // tb),
        in_specs=[pl.BlockSpec((1, tb, c), lambda bi, i: (bi, i, 0)),
                  pl.BlockSpec((1, POOL_HALO, c), lambda bi, i: (bi, jnp.maximum(i * hb - 1, 0), 0)),
                  pl.BlockSpec((1, POOL_HALO, c), lambda bi, i: (bi, 0, 0)),
                  pl.BlockSpec((len(POOL_WINDOWS), POOL_GROUP, POOL_GROUP), lambda bi, i: (0, 0, 0)),
                  pl.BlockSpec((1, c), lambda bi, i: (0, 0))],
        out_specs=pl.BlockSpec((1, tb, c), lambda bi, i: (bi, i, 0)),
        scratch_shapes=[pltpu.VMEM((POOL_HALO + tb, c), F32)],
        compiler_params=_cparams("parallel", "parallel"),
    )(u, u, state16, w, scale)


CONV_HALO = SUBLANES
SSD_GH = SSD_HEADS // SSD_GROUPS
SSD_GW = SSD_GH * SSD_HEAD_DIM
CONV_CHUNK = 512


def _ssd_kernel(z_ref, dt_ref, xbc_ref, halo_ref, st_ref, h0_ref, cw_ref, cb_ref, dtb_ref, alog_ref, dsk_ref,
                nw_ref, ex_ref, ext_t_ref, y_ref, hT_ref, ext_ref, xs_ref, h_ref, *, q, t_valid):
    c = pl.program_id(1)

    @pl.when(c == 0)
    def _():
        h_ref[...] = h0_ref[0]

    ext_ref[0:CONV_HALO] = jnp.where(c == 0, st_ref[0], halo_ref[0])
    ext_ref[CONV_HALO:CONV_HALO + q] = xbc_ref[0]
    for ch in range(D_SSD_CONV // CONV_CHUNK):
        cc = slice(ch * CONV_CHUNK, (ch + 1) * CONV_CHUNK)
        xc = cb_ref[:, cc]
        for j in range(SSD_CONV):
            off = CONV_HALO - (SSD_CONV - 1) + j
            xc = xc + ext_ref[off:off + q, cc] * cw_ref[j:j + 1, cc]
        xs_ref[:, cc] = _silu(xc)
    x = xs_ref[:, :D_SSD]

    lane = lax.broadcasted_iota(jnp.int32, (q, LANES), 1)
    row = lax.broadcasted_iota(jnp.int32, (q, LANES), 0)
    v = dt_ref[0] + dtb_ref[...]
    dt = jnp.maximum(v, 0.0) + jnp.log1p(jnp.exp(-jnp.abs(v)))
    dt = jnp.where(jnp.logical_and(lane < SSD_HEADS, c * q + row < t_valid), dt, 0.0)
    a = dt * (-jnp.exp(alog_ref[...]))
    tri = jnp.where(lane <= row, 1.0, 0.0).astype(BF16)
    causal = lane <= row
    cs = _dot3_left(tri, a)
    cs_t = cs.T
    ex = ex_ref[...]
    cs_last = cs[q - 1:q, :]
    dt_x = _dot3(dt, ex)
    ecs_x = jnp.exp(_dot3(cs, ex))
    dec_x = jnp.exp(_dot3(cs_last - cs, ex))
    xdt = x * dt_x
    xdec = (xdt * dec_x).astype(BF16)
    xdt16 = xdt.astype(BF16)
    rowdec = jnp.exp(_dot3_left(ext_t_ref[...], jnp.broadcast_to(cs_t[:, q - 1:q], (LANES, LANES))))
    lo_half = lane < SSD_HEAD_DIM

    for g in range(SSD_GROUPS):
        bg = xs_ref[:, D_SSD + g * SSD_STATE:D_SSD + (g + 1) * SSD_STATE].astype(BF16)
        cg = xs_ref[:, D_SSD + (SSD_GROUPS + g) * SSD_STATE:D_SSD + (SSD_GROUPS + g + 1) * SSD_STATE].astype(BF16)
        cb = lax.dot_general(cg, bg, NT, preferred_element_type=F32)
        hs = slice(g * SSD_GW, (g + 1) * SSD_GW)
        hprev = h_ref[hs, :]
        y_off = lax.dot_general(cg, hprev.astype(BF16), NT, preferred_element_type=F32) * ecs_x[:, hs]
        for jp in range(SSD_GH // 2):
            cols = slice(g * SSD_GW + jp * LANES, g * SSD_GW + (jp + 1) * LANES)
            xp = xdt16[:, cols]
            halves = []
            for e in range(2):
                hd = g * SSD_GH + 2 * jp + e
                seg = cs[:, hd:hd + 1] - cs_t[hd:hd + 1, :]
                lmat = jnp.where(causal, jnp.exp(jnp.where(causal, seg, 0.0)), 0.0)
                halves.append(jnp.dot((cb * lmat).astype(BF16), xp, preferred_element_type=F32))
            y_diag = jnp.where(lo_half, halves[0], halves[1])
            yv = y_diag + y_off[:, jp * LANES:(jp + 1) * LANES] + dsk_ref[:, cols] * xs_ref[:, cols]
            zv = z_ref[0, :, cols]
            ext_ref[0:q, cols] = yv * _silu(zv)
        st_new = lax.dot_general(xdec[:, hs], bg, TN, preferred_element_type=F32)
        h_ref[hs, :] = hprev * rowdec[hs, :] + st_new

    for g in range(SSD_GROUPS):
        hs = slice(g * SSD_GW, (g + 1) * SSD_GW)
        yg = ext_ref[0:q, hs]
        yn = yg * lax.rsqrt(jnp.mean(yg * yg, axis=-1, keepdims=True) + EPS) * nw_ref[:, hs]
        y_ref[0, :, hs] = yn.astype(y_ref.dtype)

    @pl.when(c == pl.num_programs(1) - 1)
    def _():
        hT_ref[0] = h_ref[...]


def _dot3_left(w, v):
    h1, h2, h3 = _split3(v)
    return (jnp.dot(w, h1, preferred_element_type=F32) + jnp.dot(w, h2, preferred_element_type=F32)
            + jnp.dot(w, h3, preferred_element_type=F32))


def _ssd(zdt, xbc, state8, h0, h_off, cw, cb, dtb, alog, dsk, nw, ex, ext_t, *, t_valid):
    b, t, _ = xbc.shape
    q = SEQ_BLOCK
    hb = q // CONV_HALO
    cst = lambda r, c_: pl.BlockSpec((r, c_), lambda bi, i: (0, 0))
    return pl.pallas_call(
        functools.partial(_ssd_kernel, q=q, t_valid=t_valid),
        name="ssd_scan",
        out_shape=[jax.ShapeDtypeStruct((b, t, D_SSD), BF16), jax.ShapeDtypeStruct((b, D_SSD, SSD_STATE), F32)],
        grid=(b, t // q),
        in_specs=[pl.BlockSpec((1, q, D_SSD), lambda bi, i: (bi, i, 0)),
                  pl.BlockSpec((1, q, LANES), lambda bi, i: (bi, i, D_SSD // LANES)),
                  pl.BlockSpec((1, q, D_SSD_CONV), lambda bi, i: (bi, i, 0)),
                  pl.BlockSpec((1, CONV_HALO, D_SSD_CONV), lambda bi, i: (bi, jnp.maximum(i * hb - 1, 0), 0)),
                  pl.BlockSpec((1, CONV_HALO, D_SSD_CONV), lambda bi, i: (bi, 0, 0)),
                  pl.BlockSpec((1, D_SSD, SSD_STATE), lambda bi, i: (h_off + bi, 0, 0)),
                  cst(SSD_CONV, D_SSD_CONV), cst(1, D_SSD_CONV), cst(1, LANES), cst(1, LANES), cst(1, D_SSD),
                  cst(1, D_SSD), cst(LANES, D_SSD), cst(D_SSD, LANES)],
        out_specs=[pl.BlockSpec((1, q, D_SSD), lambda bi, i: (bi, i, 0)),
                   pl.BlockSpec((1, D_SSD, SSD_STATE), lambda bi, i: (bi, 0, 0))],
        scratch_shapes=[pltpu.VMEM((CONV_HALO + q, D_SSD_CONV), F32), pltpu.VMEM((q, D_SSD_CONV), F32),
                        pltpu.VMEM((D_SSD, SSD_STATE), F32)],
        compiler_params=_cparams("parallel", "arbitrary"),
    )(zdt, zdt, xbc, xbc, state8, h0, cw, cb, dtb, alog, dsk, nw, ex, ext_t)


def _merge_kernel(x_ref, oa_ref, op_ref, os_ref, gt_ref, g1_ref, wa_ref, wp_ref, ws_ref, wo_ref, o_ref):
    def gate(bi):
        return jax.nn.sigmoid(gt_ref[:, bi * D_MODEL:(bi + 1) * D_MODEL])
    merged = (gate(0) * jnp.dot(oa_ref[...], wa_ref[...], preferred_element_type=F32)
              + gate(1) * jnp.dot(op_ref[...], wp_ref[...], preferred_element_type=F32)
              + gate(2) * jnp.dot(os_ref[...], ws_ref[...], preferred_element_type=F32))
    o_ref[...] = x_ref[...] + g1_ref[0] * jnp.dot(merged.astype(BF16), wo_ref[...], preferred_element_type=F32)


def _merge(x, oa, op, os_, gates, mv_g1, mv_idx, wa, wp, ws, wo, tm):
    n, d = x.shape
    rb = mv_g1.shape[1]
    rowb = lambda w: pl.BlockSpec((tm, w), lambda i: (i, 0))
    cst = lambda r, c: pl.BlockSpec((r, c), lambda i: (0, 0))
    return pl.pallas_call(
        _merge_kernel,
        name="merge_out",
        out_shape=jax.ShapeDtypeStruct((n, d), F32),
        grid=(n // tm,),
        in_specs=[rowb(d), rowb(D_ATTN), rowb(D_POOL), rowb(D_SSD), rowb(N_BRANCH * D_MODEL),
                  pl.BlockSpec((1, rb, d), lambda i: (mv_idx(i), 0, 0)),
                  cst(D_ATTN, d), cst(D_POOL, d), cst(D_SSD, d), cst(d, d)],
        out_specs=rowb(d),
        compiler_params=_cparams("parallel"),
    )(x, oa, op, os_, gates, mv_g1, wa, wp, ws, wo)


FFN_TILE = D_FF // 2
FFN_ROWS = 512
FFN_SLOT = 16


def _ffn_kernel(x_ref, hx_ref, sa_ref, sg_ref, sh_ref, sc_ref, g2_ref, gn_ref, wu_ref, cwa_ref, cwg_ref, cba_ref,
                cbg_ref, wd_ref, o_ref, ta_ref, tg_ref, xin_ref, ea_ref, eg_ref, acc_ref, *, tb, t_valid, tail_row, slot):
    i = pl.program_id(1)
    if slot is None:
        carried_in = [(0, 0)]
        tails = [(tail_row, 0)]
    else:
        carried_in = [(CONV_HALO + k * slot, k) for k in range(tb // slot)]
        tails = [(CONV_HALO + k * slot + t_valid, k) for k in range(tb // slot)]
    xin_ref[0:CONV_HALO] = hx_ref[0]
    xin_ref[CONV_HALO:CONV_HALO + tb] = x_ref[0]
    xin = xin_ref[...]
    y = xin * lax.rsqrt(jnp.mean(xin * xin, axis=-1, keepdims=True) + EPS) * gn_ref[...]

    def rows_of(m):
        return m if m.shape[0] == 1 else jnp.concatenate([m[:CONV_HALO], m], axis=0)

    h = (y * (1.0 + rows_of(sc_ref[0])) + rows_of(sh_ref[0])).astype(BF16)
    for f in range(D_FF // FFN_TILE):
        cols = slice(f * FFN_TILE, (f + 1) * FFN_TILE)

        def conv(e_ref, s_ref, t_ref, cw_ref, cb_ref, col0):
            u = jnp.dot(h, wu_ref[:, col0 + f * FFN_TILE:col0 + (f + 1) * FFN_TILE], preferred_element_type=F32)
            e_ref[...] = u

            def take_carried():
                for row, k in carried_in:
                    e_ref[row:row + CONV_HALO] = s_ref[k, :, cols]

            def emit_tails():
                for row, k in tails:
                    t_ref[k, :, cols] = e_ref[row:row + CONV_HALO]

            if slot is None:
                pl.when(i == 0)(take_carried)
                pl.when(i == pl.num_programs(1) - 1)(emit_tails)
            else:
                take_carried()
                emit_tails()
            out = cb_ref[:, cols]
            for j in range(FFN_CONV):
                off = CONV_HALO - (FFN_CONV - 1) + j
                out = out + e_ref[off:off + tb] * cw_ref[j:j + 1, cols]
            return out

        a = conv(ea_ref, sa_ref, ta_ref, cwa_ref, cba_ref, 0)
        g = conv(eg_ref, sg_ref, tg_ref, cwg_ref, cbg_ref, D_FF)
        part = jnp.dot((_silu(g) * a).astype(BF16), wd_ref[cols, :], preferred_element_type=F32)
        if f == 0:
            acc_ref[...] = part
        else:
            acc_ref[...] += part
    o_ref[0] = x_ref[0] + g2_ref[0] * acc_ref[...]


def _ffn(x, sa, sg, sh2, sc2, g2, gn, wu, cwa, cwg, cba, cbg, wd, *, tb, t_valid, slot=None):
    b, t, d = x.shape
    n = sa.shape[0]
    hb = tb // CONV_HALO
    blk = pl.BlockSpec((1, tb, d), lambda bi, i: (bi, i, 0))
    halo = pl.BlockSpec((1, CONV_HALO, d), lambda bi, i: (bi, jnp.maximum(i * hb - 1, 0), 0))
    if slot is None:
        st = pl.BlockSpec((1, CONV_HALO, D_FF), lambda bi, i: (bi, 0, 0))
        seqv = pl.BlockSpec((1, 1, d), lambda bi, i: (bi, 0, 0))
    else:
        st = pl.BlockSpec((tb // slot, CONV_HALO, D_FF), lambda bi, i: (i, 0, 0))
        seqv = blk
    cst = lambda r, c: pl.BlockSpec((r, c), lambda bi, i: (0, 0))
    return pl.pallas_call(
        functools.partial(_ffn_kernel, tb=tb, t_valid=t_valid, tail_row=t_valid - (t // tb - 1) * tb, slot=slot),
        name="conv_ffn",
        out_shape=[jax.ShapeDtypeStruct((b, t, d), F32), jax.ShapeDtypeStruct((n, CONV_HALO, D_FF), F32),
                   jax.ShapeDtypeStruct((n, CONV_HALO, D_FF), F32)],
        grid=(b, t // tb),
        in_specs=[blk, halo, st, st, seqv, seqv, seqv, cst(1, d), cst(d, 2 * D_FF), cst(FFN_CONV, D_FF),
                  cst(FFN_CONV, D_FF), cst(1, D_FF), cst(1, D_FF), cst(D_FF, d)],
        out_specs=[blk, st, st],
        scratch_shapes=[pltpu.VMEM((CONV_HALO + tb, d), F32), pltpu.VMEM((CONV_HALO + tb, FFN_TILE), F32),
                        pltpu.VMEM((CONV_HALO + tb, FFN_TILE), F32), pltpu.VMEM((tb, d), F32)],
        compiler_params=_cparams("parallel", "arbitrary"),
    )(x, x, sa, sg, sh2, sc2, g2, gn, wu, cwa, cwg, cba, cbg, wd)


def _pad_rows(a, rows, front=False):
    extra = rows - a.shape[1]
    if extra == 0:
        return a
    return jnp.pad(a, ((0, 0), (extra, 0) if front else (0, extra), (0, 0)))


def _rope_tables(pos):
    half = HEAD_DIM // 2
    freqs = ROPE_THETA ** (-jnp.arange(half, dtype=jnp.float32) / half)
    ang = pos.astype(jnp.float32)[:, None] * freqs[None, :]
    cos, sin = jnp.cos(ang), jnp.sin(ang)
    return jnp.concatenate([cos, cos, cos, cos], axis=-1), jnp.concatenate([-sin, sin, -sin, sin], axis=-1)


def _layer_weights(l, w_ada, b_ada, norm1, norm2, w_in, q_norm, k_norm, pool_w, pool_scale, ssd_conv_w, ssd_conv_b,
                   ssd_dt_bias, ssd_a_log, ssd_d, ssd_norm, w_branch_attn, w_branch_pool, w_branch_ssd, w_out,
                   ffn_up, ffn_conv_w, ffn_conv_b, ffn_down):
    o = _OFFS
    win = w_in[l]
    col = lambda a, b: win[:, a:b]
    zpad = lambda n: jnp.zeros((D_MODEL, n), win.dtype)
    row = lambda v: v.reshape(1, -1).astype(F32)
    lanepad = lambda v: jnp.pad(v.astype(F32), (0, LANES - v.shape[0])).reshape(1, LANES)
    return dict(
        w_ada=w_ada[l].astype(BF16), b_ada=row(b_ada[l]), norm1=row(norm1[l]), norm2=row(norm2[l]),
        w_attn=jnp.concatenate([col(o[0], o[6]), zpad(LANES - IDX_DIM - IDX_HEADS)], axis=1).astype(BF16),
        w_pool=col(o[6], o[7]).astype(BF16),
        w_zdt=jnp.concatenate([col(o[7], o[8]), col(o[9], o[10]), zpad(LANES - SSD_HEADS)], axis=1).astype(BF16),
        w_xbc=col(o[8], o[9]).astype(BF16),
        w_gate=col(o[10], o[11]).astype(BF16),
        qn=row(jnp.tile(q_norm[l], LANES // HEAD_DIM)), kn=row(jnp.tile(k_norm[l], LANES // HEAD_DIM)),
        pool_w=pool_w[l].astype(BF16), pool_scale=row(pool_scale[l]),
        cw=ssd_conv_w[l].astype(F32), cb=row(ssd_conv_b[l]), dtb=lanepad(ssd_dt_bias[l]), alog=lanepad(ssd_a_log[l]),
        dsk=row(jnp.repeat(ssd_d[l], SSD_HEAD_DIM)), nw=row(ssd_norm[l]),
        wa=w_branch_attn[l].astype(BF16), wp=w_branch_pool[l].astype(BF16), ws=w_branch_ssd[l].astype(BF16),
        wo=w_out[l].astype(BF16),
        w_up=ffn_up[l].astype(BF16),
        cwa=ffn_conv_w[l][:, :D_FF].astype(F32), cwg=ffn_conv_w[l][:, D_FF:].astype(F32),
        cba=row(ffn_conv_b[l][:D_FF]), cbg=row(ffn_conv_b[l][D_FF:]), wd=ffn_down[l].astype(BF16))


def _group_layer(x, mod, lw, consts, *, pos0, past, states):
    b, t, d = x.shape
    n = b * t
    tp = -(-t // SEQ_BLOCK) * SEQ_BLOCK
    per_seq = t % SEQ_BLOCK == 0
    tm = min(ROW_TILE, t) if per_seq else n
    sh1, sc1, g1, sh2, sc2, g2 = [mod[:, i * d:(i + 1) * d] for i in range(6)]
    pos = pos0 + jnp.arange(t)
    cos_t, sin_t = _rope_tables(pos)
    if per_seq:
        view = lambda m: m.reshape(b, 1, d)
        mv_idx = lambda i: i // (t // tm)
        tab = lambda a: a.reshape(t // tm, tm, LANES)
        tab_idx = lambda i: i % (t // tm)
    else:
        view = lambda m: jnp.repeat(m, t, axis=0).reshape(1, n, d)
        mv_idx = lambda i: 0
        tab = lambda a: jnp.tile(a, (b, 1)).reshape(1, n, LANES)
        tab_idx = lambda i: 0
    xf = x.reshape(n, d)
    nm = lambda w: _norm_matmul(xf, view(sh1), view(sc1), mv_idx, lw['norm1'], w, tm)
    pa, pool_u, zdt, xbc, gates = nm(lw['w_attn']), nm(lw['w_pool']), nm(lw['w_zdt']), nm(lw['w_xbc']), nm(lw['w_gate'])

    q_r, k_r, qi_r, ki2, wi_s = _attn_prep(pa, tab(cos_t), tab(sin_t), tab_idx, lw['qn'], lw['kn'], consts['seg'], tm)
    v_new = pa[:, D_ATTN + D_KV:D_ATTN + 2 * D_KV]
    seq3 = lambda a: a.reshape(b, t, a.shape[-1])
    if past is None:
        tq, tk = ATTN_Q_ROWS, min(ATTN_KEYS, t)
        k_all, v_all, ki_all = seq3(k_r).astype(BF16), seq3(v_new).astype(BF16), seq3(ki2).astype(BF16)
        s_valid = t
        qpad = lambda a: seq3(a)
    else:
        tq = 2 * SUBLANES
        page_table, pool_off, ck, cv, ci = past
        k_all, v_all, ki_all = _page_gather(page_table, pool_off, ck, cv, ci, seq3(k_r), seq3(v_new), seq3(ki2))
        tk = min(ATTN_KEYS_DECODE, k_all.shape[1] // 2)
        s_valid = pos0 + t
        qpad = lambda a: _pad_rows(seq3(a), tq)
    assert k_all.shape[1] % tk == 0
    kk = min(INDEX_TOPK, s_valid // 4)
    o_attn = _attention(qpad(q_r), qpad(qi_r), qpad(wi_s), k_all, v_all, ki_all,
                        tq=tq, tk=tk, pos0=pos0, s_valid=s_valid, kk=kk)[:, :t].reshape(n, D_ATTN)

    st_pool, st_sconv, (st_ssd, ssd_off), st_fconv = states
    seqp = lambda a: _pad_rows(seq3(a), tp)
    unpad = lambda a: a[:, :t].reshape(n, a.shape[-1])
    st16 = _pad_rows(st_pool, POOL_HALO, front=True)
    if per_seq or t > POOL_HALO or pos0 < POOL_CTX:
        o_pool = unpad(_pool(seqp(pool_u), st16, lw['pool_w'], lw['pool_scale'], pos0=pos0))
    else:
        slot = 2 * POOL_HALO
        ext = jnp.concatenate([st16, _pad_rows(seq3(pool_u), POOL_HALO)], axis=1).reshape(1, b * slot, D_POOL)
        ext = _pad_rows(ext, -(-b * slot // SEQ_BLOCK) * SEQ_BLOCK)
        o_pool = _pool(ext, jnp.zeros((1, POOL_HALO, D_POOL), ext.dtype), lw['pool_w'], lw['pool_scale'], pos0=pos0)
        o_pool = o_pool[0, :b * slot].reshape(b, slot, D_POOL)[:, POOL_HALO:POOL_HALO + t].reshape(n, D_POOL)
    o_ssd, ssd_state = _ssd(seqp(zdt), seqp(xbc), _pad_rows(st_sconv, CONV_HALO, front=True),
                            st_ssd, ssd_off, lw['cw'], lw['cb'], lw['dtb'], lw['alog'], lw['dsk'],
                            lw['nw'], consts['ex'], consts['ext_t'], t_valid=t)
    x1 = _merge(xf, o_attn, o_pool, unpad(o_ssd), gates, view(g1), mv_idx,
                lw['wa'], lw['wp'], lw['ws'], lw['wo'], tm)

    st_f = _pad_rows(st_fconv, CONV_HALO, front=True)
    ffn_w = (lw['norm2'], lw['w_up'], lw['cwa'], lw['cwg'], lw['cba'], lw['cbg'], lw['wd'])
    if per_seq or t > FFN_SLOT - CONV_HALO:
        per_b = lambda m: m.reshape(b, 1, d)
        x2, tail_a, tail_g = _ffn(seqp(x1), st_f[:, :, :D_FF], st_f[:, :, D_FF:], per_b(sh2), per_b(sc2), per_b(g2),
                                  *ffn_w, tb=min(FFN_ROWS, tp), t_valid=t)
        x2 = x2[:, :t]
    else:
        in_slot = lambda a: jnp.pad(a, ((0, 0), (CONV_HALO, FFN_SLOT - CONV_HALO - t), (0, 0))).reshape(1, b * FFN_SLOT, d)
        per_row = lambda m: jnp.repeat(m, FFN_SLOT, axis=0).reshape(1, b * FFN_SLOT, d)
        slots_per_step = max(k for k in range(1, FFN_ROWS // FFN_SLOT + 1) if b % k == 0)
        x2, tail_a, tail_g = _ffn(in_slot(seq3(x1)), st_f[:, :, :D_FF], st_f[:, :, D_FF:], per_row(sh2), per_row(sc2),
                                  per_row(g2), *ffn_w, tb=slots_per_step * FFN_SLOT, t_valid=t, slot=FFN_SLOT)
        x2 = x2.reshape(b, FFN_SLOT, d)[:, CONV_HALO:CONV_HALO + t]
    fconv_state = jnp.concatenate([tail_a, tail_g], axis=-1)[:, -(FFN_CONV - 1):]

    def tail(prev, news, keep):
        last = jnp.concatenate([seq3(a)[:, -min(keep, t):] for a in news], axis=-1)
        return jnp.concatenate([prev, last], axis=1)[:, -keep:]

    new_states = (seq3(k_r).reshape(b, t, N_KV_HEADS, HEAD_DIM), seq3(v_new).reshape(b, t, N_KV_HEADS, HEAD_DIM),
                  seq3(ki2)[:, :, :IDX_DIM], tail(st_pool, [pool_u], POOL_CTX), tail(st_sconv, [xbc], SSD_CONV - 1),
                  ssd_state.reshape(b, SSD_HEADS, SSD_HEAD_DIM, SSD_STATE), fconv_state)
    return x2, new_states


def kernel(x_prompt, x_sample, c_prompt, c_sample, cache_k, cache_v, cache_kidx, page_table, state_pool, state_ssd_conv, state_ssd, state_ffn_conv, w_ada, b_ada, norm1, norm2, w_in, q_norm, k_norm, pool_w, pool_scale, ssd_conv_w, ssd_conv_b, ssd_dt_bias, ssd_a_log, ssd_d, ssd_norm, w_branch_attn, w_branch_pool, w_branch_ssd, w_out, ffn_up, ffn_conv_w, ffn_conv_b, ffn_down):
    bp, tp, d = x_prompt.shape
    bs, ts, _ = x_sample.shape
    depth = w_ada.shape[0]
    n_pool, page = cache_k.shape[1], cache_k.shape[2]
    past_len = page_table.shape[1] * page
    dt = x_prompt.dtype

    r = lax.broadcasted_iota(jnp.int32, (LANES, LANES), 0)
    c = lax.broadcasted_iota(jnp.int32, (LANES, LANES), 1)
    hr = lax.broadcasted_iota(jnp.int32, (LANES, D_SSD), 0)
    hc = lax.broadcasted_iota(jnp.int32, (LANES, D_SSD), 1)
    ex = (hr == hc // SSD_HEAD_DIM).astype(BF16)
    consts = dict(seg=(r // HEAD_DIM == c // HEAD_DIM).astype(BF16), ex=ex, ext_t=ex.T)

    c_all = jnp.concatenate([c_prompt, c_sample], axis=0)
    rows = -(-c_all.shape[0] // SUBLANES) * SUBLANES
    c_pad = jnp.pad(c_all, ((0, rows - c_all.shape[0]), (0, 0)))

    ck_t = jnp.transpose(cache_k, (0, 1, 3, 4, 2)).reshape(depth * n_pool, D_KV, page)
    cv_t = jnp.transpose(cache_v, (0, 1, 3, 4, 2)).reshape(depth * n_pool, D_KV, page)
    ci_t = jnp.transpose(cache_kidx, (0, 1, 3, 2)).reshape(depth * n_pool, IDX_DIM, page)

    ssd_all = state_ssd.reshape(depth * bs, D_SSD, SSD_STATE)
    zeros = lambda *s: jnp.zeros(s, dt)
    xp, xs = x_prompt, x_sample
    new_p, new_s = [[] for _ in range(7)], [[] for _ in range(7)]
    for l in range(depth):
        lw = _layer_weights(l, w_ada, b_ada, norm1, norm2, w_in, q_norm, k_norm, pool_w, pool_scale, ssd_conv_w,
                            ssd_conv_b, ssd_dt_bias, ssd_a_log, ssd_d, ssd_norm, w_branch_attn, w_branch_pool,
                            w_branch_ssd, w_out, ffn_up, ffn_conv_w, ffn_conv_b, ffn_down)
        mod = _mod(c_pad, lw['w_ada'], lw['b_ada'])
        xp, st_p = _group_layer(
            xp, mod[:bp], lw, consts, pos0=0, past=None,
            states=(zeros(bp, POOL_CTX, D_POOL), zeros(bp, SSD_CONV - 1, D_SSD_CONV),
                    (zeros(bp, D_SSD, SSD_STATE), 0), zeros(bp, FFN_CONV - 1, 2 * D_FF)))
        past = (page_table, l * n_pool, ck_t, cv_t, ci_t)
        xs, st_s = _group_layer(xs, mod[bp:bp + bs], lw, consts, pos0=past_len, past=past,
                                states=(state_pool[l], state_ssd_conv[l], (ssd_all, l * bs), state_ffn_conv[l]))
        for j in range(7):
            new_p[j].append(st_p[j])
            new_s[j].append(st_s[j])
    outs_p = [jnp.stack(s) for s in new_p]
    outs_s = [jnp.stack(s) for s in new_s]
    return (xp, xs, *outs_p, *outs_s)
```
